```python
import jax, jax.numpy as jnp
from jax import lax
import numpy as np

D_MODEL = 2048
BATCH = 8
SEQ = 2048
DEPTH = 1

CTX_LEN = 256
GRID_W = 64
HEAD_DIM = 128
N_HEADS = 8
N_KV_HEADS = 2
Q_BLOCK = 128
ROPE_THETA = 10000.0
AXIS_DIM = HEAD_DIM // 2
ATTN_SCALE = HEAD_DIM ** -0.5
GM_GROUPS = 8
GM_GROUP_DIM = 128
CHUNK = 128
N_BRANCHES = 2
Q_W = N_HEADS * HEAD_DIM
KV_W = N_KV_HEADS * HEAD_DIM
GM_W = GM_GROUPS * GM_GROUP_DIM
IN_W = 2 * KV_W + Q_W + 2 * GM_W + N_BRANCHES * D_MODEL
N_GROUPS = 4
EXPERTS_PER_GROUP = 8
N_EXPERTS = N_GROUPS * EXPERTS_PER_GROUP
TOP_K = 2
D_EXPERT = D_MODEL // 2
N_MOD = 6
EPS = 1e-6

kernel_name = "hybrid_gated_gmlp_axial_gqa_hmoe_block"


def rms_norm(x, g):
    xf = x.astype(jnp.float32)
    y = xf * lax.rsqrt(jnp.mean(jnp.square(xf), axis=-1, keepdims=True) + EPS)
    return (y * g.astype(jnp.float32)).astype(x.dtype)


def layer_norm(x, g):
    xf = x.astype(jnp.float32)
    xc = xf - jnp.mean(xf, axis=-1, keepdims=True)
    y = xc * lax.rsqrt(jnp.mean(jnp.square(xc), axis=-1, keepdims=True) + EPS)
    return (y * g.astype(jnp.float32)).astype(x.dtype)


def adaln(cond, w, b, n):
    m = jax.nn.silu(cond) @ w[:, :n * D_MODEL] + b[:n * D_MODEL]
    return jnp.split(m[..., None, :], n, axis=-1)


def modulate(h, shift, scale):
    return h * (1.0 + scale) + shift


def axial_rope(rows):
    row = jnp.repeat(jnp.arange(rows, dtype=jnp.int32), GRID_W)
    col = jnp.tile(jnp.arange(GRID_W, dtype=jnp.int32), rows)
    inv_freq = ROPE_THETA ** (-jnp.arange(0, AXIS_DIM, 2, dtype=jnp.float32) / AXIS_DIM)
    pos = jnp.stack([row, col], axis=-1).astype(jnp.float32)
    ang = pos[:, :, None] * inv_freq
    return jnp.cos(ang), jnp.sin(ang)


def apply_rope(x, cos, sin):
    B, L, H, _ = x.shape
    xr = x.reshape(B, L, H, 2, 2, AXIS_DIM // 2)
    x1, x2 = xr[..., 0, :], xr[..., 1, :]
    cs = cos[None, :, None].astype(x.dtype)
    sn = sin[None, :, None].astype(x.dtype)
    out = jnp.stack([x1 * cs - x2 * sn, x2 * cs + x1 * sn], axis=-2)
    return out.reshape(B, L, H, HEAD_DIM)


def block_attention(q, k, v):
    B, Lq, _, _ = q.shape
    n_blk = Lq // Q_BLOCK
    grp = N_HEADS // N_KV_HEADS
    qb = q.reshape(B, n_blk, Q_BLOCK, N_KV_HEADS, grp, HEAD_DIM).transpose(1, 0, 2, 3, 4, 5)

    def attend(q_blk):
        s = jnp.einsum('bqkgd,bskd->bkgqs', q_blk, k).astype(jnp.float32) * ATTN_SCALE
        p = jax.nn.softmax(s, axis=-1).astype(v.dtype)
        return jnp.einsum('bkgqs,bskd->bqkgd', p, v)

    o = lax.map(attend, qb)
    return o.transpose(1, 0, 2, 3, 4, 5).reshape(B, Lq, N_HEADS * HEAD_DIM)


def chunk_spatial_gating(u, v, ln_g, w_s, b_s):
    B, L, _ = u.shape
    n = L // CHUNK
    vn = layer_norm(v.reshape(B, L, GM_GROUPS, GM_GROUP_DIM), ln_g.reshape(GM_GROUPS, GM_GROUP_DIM))
    vc = vn.reshape(B, n, CHUNK, GM_GROUPS, GM_GROUP_DIM)
    s = jnp.einsum('gpq,bnqgc->bnpgc', w_s, vc) + b_s.T[:, :, None]
    return u * s.reshape(B, L, GM_W)


def project_kv(h, w_in, k_norm):
    B, L, _ = h.shape
    k, v = jnp.split(h @ w_in[:, :2 * KV_W], 2, axis=-1)
    k = rms_norm(k.reshape(B, L, N_KV_HEADS, HEAD_DIM), k_norm)
    return k, v.reshape(B, L, N_KV_HEADS, HEAD_DIM)


def project_rest(h, w_in, q_norm):
    B, L, _ = h.shape
    r = h @ w_in[:, 2 * KV_W:]
    q, u, vg, ga, gg = jnp.split(r, [Q_W, Q_W + GM_W, Q_W + 2 * GM_W, Q_W + 2 * GM_W + D_MODEL], axis=-1)
    q = rms_norm(q.reshape(B, L, N_HEADS, HEAD_DIM), q_norm)
    return q, u, vg, ga, gg


def merge_branches(attn, u, vg, ga, gg, gm_ln, w_s, b_s, w_ba, w_bg, w_o):
    gm = chunk_spatial_gating(jax.nn.gelu(u), jax.nn.gelu(vg), gm_ln, w_s, b_s)
    merged = jax.nn.sigmoid(ga) * (attn @ w_ba) + jax.nn.sigmoid(gg) * (gm @ w_bg)
    return merged @ w_o


def hier_moe(h, w_rg, b_rg, w_re, b_re, w_gate, w_up, w_down):
    B, L, D = h.shape
    t = h.reshape(B * L, D)
    tf = t.astype(jnp.float32)
    p_group = jax.nn.softmax(tf @ w_rg.astype(jnp.float32) + b_rg.astype(jnp.float32), axis=-1)
    p_top, g_idx = lax.top_k(p_group, 1)
    e_logits = jnp.einsum('nd,dge->nge', tf, w_re.astype(jnp.float32)) + b_re.astype(jnp.float32)
    sel = e_logits[jnp.arange(B * L), g_idx[:, 0]]
    p_two, e_idx = lax.top_k(jax.nn.softmax(sel, axis=-1), TOP_K)
    weight = p_top * p_two / jnp.sum(p_two, axis=-1, keepdims=True)
    expert_id = g_idx * EXPERTS_PER_GROUP + e_idx
    gates = jnp.einsum('nk,nke->ne', weight,
                       jax.nn.one_hot(expert_id, N_EXPERTS, dtype=jnp.float32)).astype(t.dtype)
    y = jnp.zeros_like(t)
    for e in range(N_EXPERTS):
        hid = jax.nn.silu(t @ w_gate[e]) * (t @ w_up[e])
        y = y + (gates[:, e:e + 1] * hid) @ w_down[e]
    return y.reshape(B, L, D)


def _normal(key, shape, scale):
    return jax.random.normal(key, shape, jnp.float32) * scale


def setup_inputs(seed: int = 0) -> dict:
    key = jax.random.key(seed)
    ks = jax.random.split(key, 26)
    D = D_MODEL
    return {
        "x": _normal(ks[0], (BATCH, SEQ, D), 1.0),
        "c": _normal(ks[1], (BATCH, D), 1.0),
        "ctx": _normal(ks[2], (BATCH, CTX_LEN, D), 1.0),
        "c_ctx": _normal(ks[3], (D,), 1.0),
        "w_ada": _normal(ks[4], (DEPTH, D, N_MOD * D), 0.5 * D ** -0.5),
        "b_ada": _normal(ks[5], (DEPTH, N_MOD * D), 0.02),
        "g_pre1": 1.0 + _normal(ks[6], (DEPTH, D), 0.05),
        "g_post1": 1.0 + _normal(ks[7], (DEPTH, D), 0.05),
        "g_pre2": 1.0 + _normal(ks[8], (DEPTH, D), 0.05),
        "g_post2": 1.0 + _normal(ks[9], (DEPTH, D), 0.05),
        "w_in": _normal(ks[10], (DEPTH, D, IN_W), D ** -0.5),
        "q_norm": 1.0 + _normal(ks[11], (DEPTH, HEAD_DIM), 0.05),
        "k_norm": 1.0 + _normal(ks[12], (DEPTH, HEAD_DIM), 0.05),
        "gm_ln": 1.0 + _normal(ks[13], (DEPTH, GM_W), 0.05),
        "w_s": _normal(ks[14], (DEPTH, GM_GROUPS, CHUNK, CHUNK), CHUNK ** -0.5),
        "b_s": 1.0 + _normal(ks[15], (DEPTH, GM_GROUPS, CHUNK), 0.05),
        "w_ba": _normal(ks[16], (DEPTH, Q_W, D), Q_W ** -0.5),
        "w_bg": _normal(ks[17], (DEPTH, GM_W, D), GM_W ** -0.5),
        "w_o": _normal(ks[18], (DEPTH, D, D), D ** -0.5),
        "w_rg": _normal(ks[19], (DEPTH, D, N_GROUPS), D ** -0.5),
        "b_rg": _normal(ks[20], (DEPTH, N_GROUPS), 0.01),
        "w_re": _normal(ks[21], (DEPTH, D, N_GROUPS, EXPERTS_PER_GROUP), D ** -0.5),
        "b_re": _normal(ks[22], (DEPTH, N_GROUPS, EXPERTS_PER_GROUP), 0.01),
        "w_gate": _normal(ks[23], (DEPTH, N_EXPERTS, D, D_EXPERT), D ** -0.5),
        "w_up": _normal(ks[24], (DEPTH, N_EXPERTS, D, D_EXPERT), D ** -0.5),
        "w_down": _normal(ks[25], (DEPTH, N_EXPERTS, D_EXPERT, D), D_EXPERT ** -0.5),
    }


def reference(x, c, ctx, c_ctx, w_ada, b_ada, g_pre1, g_post1, g_pre2, g_post2, w_in, q_norm, k_norm,
              gm_ln, w_s, b_s, w_ba, w_bg, w_o, w_rg, b_rg, w_re, b_re, w_gate, w_up, w_down):
    rows = x.shape[1] // GRID_W
    cos, sin = axial_rope(rows)
    for l in range(DEPTH):
        last = l + 1 == DEPTH
        sh1, sc1, g1, sh2, sc2, g2 = adaln(c, w_ada[l], b_ada[l], N_MOD)
        ctx_mod = adaln(c_ctx, w_ada[l], b_ada[l], 2 if last else N_MOD)

        hc = modulate(rms_norm(ctx, g_pre1[l]), ctx_mod[0], ctx_mod[1])
        hx = modulate(rms_norm(x, g_pre1[l]), sh1, sc1)
        kc, vc = project_kv(hc, w_in[l], k_norm[l])
        kx, vx = project_kv(hx, w_in[l], k_norm[l])
        qx, ux, vgx, gax, ggx = project_rest(hx, w_in[l], q_norm[l])
        qx = apply_rope(qx, cos, sin)
        kx = apply_rope(kx, cos, sin)
        attn_x = block_attention(qx, jnp.concatenate([kc, kx], axis=1), jnp.concatenate([vc, vx], axis=1))
        mix_x = merge_branches(attn_x, ux, vgx, gax, ggx, gm_ln[l], w_s[l], b_s[l], w_ba[l], w_bg[l], w_o[l])
        x = x + g1 * rms_norm(mix_x, g_post1[l])

        moe_x = hier_moe(modulate(rms_norm(x, g_pre2[l]), sh2, sc2),
                         w_rg[l], b_rg[l], w_re[l], b_re[l], w_gate[l], w_up[l], w_down[l])
        x = x + g2 * rms_norm(moe_x, g_post2[l])

        if not last:
            csh1, csc1, cg1, csh2, csc2, cg2 = ctx_mod
            qc, uc, vgc, gac, ggc = project_rest(hc, w_in[l], q_norm[l])
            attn_c = block_attention(qc, kc, vc)
            mix_c = merge_branches(attn_c, uc, vgc, gac, ggc, gm_ln[l], w_s[l], b_s[l], w_ba[l], w_bg[l], w_o[l])
            ctx = ctx + cg1 * rms_norm(mix_c, g_post1[l])
            moe_c = hier_moe(modulate(rms_norm(ctx, g_pre2[l]), csh2, csc2),
                             w_rg[l], b_rg[l], w_re[l], b_re[l], w_gate[l], w_up[l], w_down[l])
            ctx = ctx + cg2 * rms_norm(moe_c, g_post2[l])
    return x
```

```python
import functools

import jax
import jax.numpy as jnp
from jax import lax
from jax.experimental import pallas as pl
from jax.experimental.pallas import tpu as pltpu

GRID_W = 64
ROPE_THETA = 10000.0
EPS = 1e-6
N_MOD = 6
TOP_K = 2

LANES = 128
SUBLANES = 8
VMEM_LIMIT_BYTES = 56 * 1024 * 1024

EXPERT_TILE = 256

F32 = jnp.float32
BF16 = jnp.bfloat16


def _params(*sem):
    return pltpu.CompilerParams(dimension_semantics=sem, vmem_limit_bytes=VMEM_LIMIT_BYTES)


def _dot(a, b):
    return jnp.dot(a, b, preferred_element_type=F32)


def _split_bf16(a):
    hi = a.astype(BF16)
    lo = (a - hi.astype(F32)).astype(BF16)
    return hi, lo


def _dot3(a, w):
    a_hi, a_lo = _split_bf16(a)
    w_hi, w_lo = _split_bf16(w)
    return _dot(a_hi, w_hi) + _dot(a_lo, w_hi) + _dot(a_hi, w_lo)


def _rms(x, g):
    return x * lax.rsqrt(jnp.mean(x * x, axis=-1, keepdims=True) + EPS) * g


def _tile(n, pref):
    t = min(n, pref)
    while n % t:
        t //= 2
    return t


def _ada_kernel(c_ref, w_ref, b_ref, o_ref):
    c = c_ref[...]
    a = c * jax.nn.sigmoid(c)
    o_ref[...] = _dot3(a, w_ref[...]) + b_ref[...]


def _ada(cs, w, b):
    m, d = cs.shape
    n = w.shape[1]
    tn = _tile(n, 1024)
    return pl.pallas_call(
        _ada_kernel,
        grid=(n // tn,),
        in_specs=[pl.BlockSpec((m, d), lambda j: (0, 0)),
                  pl.BlockSpec((d, tn), lambda j: (0, j)),
                  pl.BlockSpec((1, tn), lambda j: (0, j))],
        out_specs=pl.BlockSpec((m, tn), lambda j: (0, j)),
        out_shape=jax.ShapeDtypeStruct((m, n), F32),
        compiler_params=_params("arbitrary"),
        name="ada",
    )(cs, w, b.reshape(1, n))


def _prenorm_kernel(x_ref, mod_ref, g_ref, o_ref):
    y = _rms(x_ref[0], g_ref[...])
    o_ref[0] = (y * (1.0 + mod_ref[0, 1:2, :]) + mod_ref[0, 0:1, :]).astype(BF16)


def _prenorm(x, mod, g, mod_row):
    bsz, l, d = x.shape
    tm = _tile(l, 512)
    return pl.pallas_call(
        _prenorm_kernel,
        grid=(bsz, l // tm),
        in_specs=[pl.BlockSpec((1, tm, d), lambda b, i: (b, i, 0)),
                  pl.BlockSpec((1, N_MOD, d), lambda b, i: (mod_row(b), 0, 0)),
                  pl.BlockSpec((1, d), lambda b, i: (0, 0))],
        out_specs=pl.BlockSpec((1, tm, d), lambda b, i: (b, i, 0)),
        out_shape=jax.ShapeDtypeStruct((bsz, l, d), BF16),
        compiler_params=_params("arbitrary", "arbitrary"),
        name="prenorm",
    )(x, mod, g.reshape(1, d))


def _swap32(x):
    lane = lax.broadcasted_iota(jnp.int32, x.shape, 1)
    fwd = pltpu.roll(x, LANES - 32, 1)
    bwd = pltpu.roll(x, 32, 1)
    return jnp.where((lane & 32) == 0, fwd, bwd)


def _norm_head(r, gain, cos, sin, scale):
    y = _rms(r, gain)
    if cos is not None:
        y = y * cos + _swap32(y) * sin
    if scale != 1.0:
        y = y * scale
    return y.astype(BF16)


def _kv_kernel(*refs, n_kv, rope):
    if rope:
        h_ref, w_ref, g_ref, cos_ref, sin_ref, k_ref, v_ref = refs
        cos, sin = cos_ref[...], sin_ref[...]
    else:
        h_ref, w_ref, g_ref, k_ref, v_ref = refs
        cos = sin = None
    kv_w = n_kv * LANES
    r = _dot(h_ref[0], w_ref[...])
    for hh in range(n_kv):
        sl = slice(hh * LANES, (hh + 1) * LANES)
        k_ref[0, :, sl] = _norm_head(r[:, sl], g_ref[...], cos, sin, 1.0)
    v_ref[0] = r[:, kv_w:].astype(BF16)


def _project_kv(h, w_in_b, k_norm, kv_w, cos, sin):
    bsz, l, d = h.shape
    tm = _tile(l, 512)
    rope = cos is not None
    in_specs = [pl.BlockSpec((1, tm, d), lambda b, i: (b, i, 0)),
                pl.BlockSpec((d, 2 * kv_w), lambda b, i: (0, 0)),
                pl.BlockSpec((1, LANES), lambda b, i: (0, 0))]
    args = [h, w_in_b, k_norm.reshape(1, LANES)]
    if rope:
        in_specs += [pl.BlockSpec((tm, LANES), lambda b, i: (i, 0))] * 2
        args += [cos, sin]
    out = jax.ShapeDtypeStruct((bsz, l, kv_w), BF16)
    return pl.pallas_call(
        functools.partial(_kv_kernel, n_kv=kv_w // LANES, rope=rope),
        grid=(bsz, l // tm),
        in_specs=in_specs,
        out_specs=[pl.BlockSpec((1, tm, kv_w), lambda b, i: (b, i, 0))] * 2,
        out_shape=[out, out],
        compiler_params=_params("arbitrary", "arbitrary"),
        name="proj_kv",
    )(*args)


def _q_kernel(h_ref, w_ref, g_ref, cos_ref, sin_ref, q_ref, *, n_heads, scale):
    r = _dot(h_ref[0], w_ref[...])
    cos, sin = cos_ref[...], sin_ref[...]
    for hh in range(n_heads):
        sl = slice(hh * LANES, (hh + 1) * LANES)
        q_ref[0, :, sl] = _norm_head(r[:, sl], g_ref[...], cos, sin, scale)


def _project_q(h, w_in_b, q_norm, col0, q_w, cos, sin, scale):
    bsz, l, d = h.shape
    tm = _tile(l, 512)
    tn = _tile(q_w, 512)
    assert col0 % tn == 0
    c0 = col0 // tn
    return pl.pallas_call(
        functools.partial(_q_kernel, n_heads=tn // LANES, scale=scale),
        grid=(bsz, l // tm, q_w // tn),
        in_specs=[pl.BlockSpec((1, tm, d), lambda b, i, j: (b, i, 0)),
                  pl.BlockSpec((d, tn), lambda b, i, j: (0, c0 + j)),
                  pl.BlockSpec((1, LANES), lambda b, i, j: (0, 0)),
                  pl.BlockSpec((tm, LANES), lambda b, i, j: (i, 0)),
                  pl.BlockSpec((tm, LANES), lambda b, i, j: (i, 0))],
        out_specs=pl.BlockSpec((1, tm, tn), lambda b, i, j: (b, i, j)),
        out_shape=jax.ShapeDtypeStruct((bsz, l, q_w), BF16),
        compiler_params=_params("arbitrary", "arbitrary", "arbitrary"),
        name="proj_q",
    )(h, w_in_b, q_norm.reshape(1, LANES), cos, sin)


def _gelu(x):
    c = 0.7978845608028654
    return x * (0.5 * (1.0 + jnp.tanh(c * (x + 0.044715 * (x * x * x)))))


def _gmlp_kernel(h_ref, wu_ref, wv_ref, ln_ref, ws_ref, bs_ref, o_ref, *, n_grp, n_chunk):
    h = h_ref[0]
    gu = _gelu(_dot(h, wu_ref[...]))
    gv = _gelu(_dot(h, wv_ref[...]))
    for g in range(n_grp):
        cs = slice(g * LANES, (g + 1) * LANES)
        v = gv[:, cs]
        vc = v - jnp.mean(v, axis=-1, keepdims=True)
        vn = vc * lax.rsqrt(jnp.mean(vc * vc, axis=-1, keepdims=True) + EPS) * ln_ref[:, cs]
        vn = vn.astype(BF16)
        w = ws_ref[g].astype(BF16)
        for c in range(n_chunk):
            rs = slice(c * LANES, (c + 1) * LANES)
            s = _dot(w, vn[rs, :]) + bs_ref[:, cs]
            o_ref[0, rs, cs] = (gu[rs, cs] * s).astype(BF16)


def _gmlp(h, w_in_b, gm_ln, w_s, bs_full, col_u, col_v, gm_w):
    bsz, l, d = h.shape
    tm = _tile(l, 512)
    tn = _tile(gm_w, 512)
    assert col_u % tn == 0 and col_v % tn == 0 and tm % LANES == 0
    cu, cv = col_u // tn, col_v // tn
    n_grp = tn // LANES
    return pl.pallas_call(
        functools.partial(_gmlp_kernel, n_grp=n_grp, n_chunk=tm // LANES),
        grid=(bsz, l // tm, gm_w // tn),
        in_specs=[pl.BlockSpec((1, tm, d), lambda b, i, j: (b, i, 0)),
                  pl.BlockSpec((d, tn), lambda b, i, j: (0, cu + j)),
                  pl.BlockSpec((d, tn), lambda b, i, j: (0, cv + j)),
                  pl.BlockSpec((1, tn), lambda b, i, j: (0, j)),
                  pl.BlockSpec((n_grp, LANES, LANES), lambda b, i, j: (j, 0, 0)),
                  pl.BlockSpec((LANES, tn), lambda b, i, j: (0, j))],
        out_specs=pl.BlockSpec((1, tm, tn), lambda b, i, j: (b, i, j)),
        out_shape=jax.ShapeDtypeStruct((bsz, l, gm_w), BF16),
        compiler_params=_params("arbitrary", "arbitrary", "arbitrary"),
        name="gmlp",
    )(h, w_in_b, w_in_b, gm_ln.reshape(1, gm_w), w_s, bs_full)


def _attn_kernel(q_ref, kc_ref, vc_ref, kx_ref, vx_ref, o_ref, *, grp):
    kc, vc, kx, vx = kc_ref[0], vc_ref[0], kx_ref[0], vx_ref[0]
    nt = (((1,), (1,)), ((), ()))
    for hh in range(grp):
        sl = slice(hh * LANES, (hh + 1) * LANES)
        q = q_ref[0, :, sl]
        sc = lax.dot_general(q, kc, nt, preferred_element_type=F32)
        sx = lax.dot_general(q, kx, nt, preferred_element_type=F32)
        m = jnp.maximum(jnp.max(sc, axis=-1, keepdims=True), jnp.max(sx, axis=-1, keepdims=True))
        pc = jnp.exp(sc - m)
        px = jnp.exp(sx - m)
        den = jnp.sum(pc, axis=-1, keepdims=True) + jnp.sum(px, axis=-1, keepdims=True)
        o = _dot(pc.astype(BF16), vc) + _dot(px.astype(BF16), vx)
        o_ref[0, :, sl] = (o / den).astype(BF16)


def _attention(q, kc, vc, kx, vx, n_kv):
    bsz, l, q_w = q.shape
    lc = kc.shape[1]
    gw = q_w // n_kv
    tq = _tile(l, 256)
    return pl.pallas_call(
        functools.partial(_attn_kernel, grp=gw // LANES),
        grid=(bsz, n_kv, l // tq),
        in_specs=[pl.BlockSpec((1, tq, gw), lambda b, k, i: (b, i, k)),
                  pl.BlockSpec((1, lc, LANES), lambda b, k, i: (b, 0, k)),
                  pl.BlockSpec((1, lc, LANES), lambda b, k, i: (b, 0, k)),
                  pl.BlockSpec((1, l, LANES), lambda b, k, i: (b, 0, k)),
                  pl.BlockSpec((1, l, LANES), lambda b, k, i: (b, 0, k))],
        out_specs=pl.BlockSpec((1, tq, gw), lambda b, k, i: (b, i, k)),
        out_shape=jax.ShapeDtypeStruct((bsz, l, q_w), BF16),
        compiler_params=_params("arbitrary", "arbitrary", "arbitrary"),
        name="attention",
    )(q, kc, vc, kx, vx)


def _merge_kernel(h_ref, a_ref, g_ref, wga_ref, wgg_ref, wba_ref, wbg_ref, o_ref):
    h = h_ref[0]
    ga = jax.nn.sigmoid(_dot(h, wga_ref[...]))
    gg = jax.nn.sigmoid(_dot(h, wgg_ref[...]))
    pa = _dot(a_ref[0], wba_ref[...])
    pg = _dot(g_ref[0], wbg_ref[...])
    o_ref[0] = (ga * pa + gg * pg).astype(BF16)


def _merge(h, attn, gm, w_in_b, w_ba_b, w_bg_b, col_ga, col_gg):
    bsz, l, d = h.shape
    q_w, gm_w = attn.shape[2], gm.shape[2]
    tm = _tile(l, 512)
    tn = _tile(d, 512)
    assert col_ga % tn == 0 and col_gg % tn == 0
    ca, cg = col_ga // tn, col_gg // tn
    return pl.pallas_call(
        _merge_kernel,
        grid=(bsz, l // tm, d // tn),
        in_specs=[pl.BlockSpec((1, tm, d), lambda b, i, j: (b, i, 0)),
                  pl.BlockSpec((1, tm, q_w), lambda b, i, j: (b, i, 0)),
                  pl.BlockSpec((1, tm, gm_w), lambda b, i, j: (b, i, 0)),
                  pl.BlockSpec((d, tn), lambda b, i, j: (0, ca + j)),
                  pl.BlockSpec((d, tn), lambda b, i, j: (0, cg + j)),
                  pl.BlockSpec((q_w, tn), lambda b, i, j: (0, j)),
                  pl.BlockSpec((gm_w, tn), lambda b, i, j: (0, j))],
        out_specs=pl.BlockSpec((1, tm, tn), lambda b, i, j: (b, i, j)),
        out_shape=jax.ShapeDtypeStruct((bsz, l, d), BF16),
        compiler_params=_params("arbitrary", "arbitrary", "arbitrary"),
        name="merge",
    )(h, attn, gm, w_in_b, w_in_b, w_ba_b, w_bg_b)


R_E0, R_E1, R_W0, R_W1, R_RANK0, R_RANK1 = range(6)


def _out_kernel(m_ref, x_ref, mod_ref, gp1_ref, gp2_ref, wo_ref, wr_ref, br_ref,
                x1_ref, h2_ref, ri_ref, cnt_ref, carry_ref, *, n_groups, per_group):
    @pl.when((pl.program_id(0) == 0) & (pl.program_id(1) == 0))
    def _():
        carry_ref[...] = jnp.zeros_like(carry_ref)

    mix = _dot(m_ref[0], wo_ref[...])
    x1 = x_ref[0] + mod_ref[0, 2:3, :] * _rms(mix, gp1_ref[...])
    x1_ref[0] = x1
    h2 = _rms(x1, gp2_ref[...]) * (1.0 + mod_ref[0, 4:5, :]) + mod_ref[0, 3:4, :]
    h2_ref[0] = h2

    logits = _dot3(h2, wr_ref[...]) + br_ref[...]
    tm = logits.shape[0]
    lane = lax.broadcasted_iota(jnp.int32, logits.shape, 1)
    neg = jnp.float32(-jnp.inf)
    lg = jnp.where(lane < n_groups, logits, neg)
    gmax = jnp.max(lg, axis=-1, keepdims=True)
    p_top = 1.0 / jnp.sum(jnp.exp(lg - gmax), axis=-1, keepdims=True)
    gidx = jnp.min(jnp.where(lg == gmax, lane, LANES), axis=-1, keepdims=True)
    lo = n_groups + gidx * per_group
    le = jnp.where((lane >= lo) & (lane < lo + per_group), logits, neg)
    l1 = jnp.max(le, axis=-1, keepdims=True)
    i1 = jnp.min(jnp.where(le == l1, lane, LANES), axis=-1, keepdims=True)
    le2 = jnp.where(lane == i1, neg, le)
    l2 = jnp.max(le2, axis=-1, keepdims=True)
    i2 = jnp.min(jnp.where(le2 == l2, lane, LANES), axis=-1, keepdims=True)
    r = jnp.exp(l2 - l1)
    w0 = p_top / (1.0 + r)
    w1 = p_top * r / (1.0 + r)

    oh1 = lane == i1
    oh2 = lane == i2
    oh = jnp.where(oh1 | oh2, 1.0, 0.0)
    row = lax.broadcasted_iota(jnp.int32, (tm, tm), 0)
    col = lax.broadcasted_iota(jnp.int32, (tm, tm), 1)
    tri = jnp.where(col < row, 1.0, 0.0).astype(BF16)
    base = _dot(tri, oh.astype(BF16)) + carry_ref[...]
    rank0 = jnp.sum(jnp.where(oh1, base, 0.0), axis=-1, keepdims=True)
    rank1 = jnp.sum(jnp.where(oh2, base, 0.0), axis=-1, keepdims=True)
    carry_ref[...] += jnp.sum(oh, axis=0, keepdims=True)
    cnt_ref[...] = carry_ref[...]

    e0 = (i1 - n_groups).astype(F32)
    e1 = (i2 - n_groups).astype(F32)
    rec = jnp.zeros_like(logits)
    for idx, val in ((R_E0, e0), (R_E1, e1), (R_W0, w0), (R_W1, w1), (R_RANK0, rank0), (R_RANK1, rank1)):
        rec = jnp.where(lane == idx, val, rec)
    ri_ref[0] = rec


def _out_route(merged, x, mod, g_post1, g_pre2, w_o_b, w_r, b_r, n_groups, per_group):
    bsz, l, d = x.shape
    tm = _tile(l, 256)
    tok = lambda b, i: (b, i, 0)
    const2 = lambda b, i: (0, 0)
    return pl.pallas_call(
        functools.partial(_out_kernel, n_groups=n_groups, per_group=per_group),
        grid=(bsz, l // tm),
        in_specs=[pl.BlockSpec((1, tm, d), tok),
                  pl.BlockSpec((1, tm, d), tok),
                  pl.BlockSpec((1, N_MOD, d), lambda b, i: (b, 0, 0)),
                  pl.BlockSpec((1, d), const2),
                  pl.BlockSpec((1, d), const2),
                  pl.BlockSpec((d, d), const2),
                  pl.BlockSpec((d, LANES), const2),
                  pl.BlockSpec((1, LANES), const2)],
        out_specs=[pl.BlockSpec((1, tm, d), tok),
                   pl.BlockSpec((1, tm, d), tok),
                   pl.BlockSpec((1, tm, LANES), tok),
                   pl.BlockSpec((1, LANES), const2)],
        out_shape=[jax.ShapeDtypeStruct((bsz, l, d), F32),
                   jax.ShapeDtypeStruct((bsz, l, d), F32),
                   jax.ShapeDtypeStruct((bsz, l, LANES), F32),
                   jax.ShapeDtypeStruct((1, LANES), F32)],
        scratch_shapes=[pltpu.VMEM((1, LANES), F32)],
        compiler_params=_params("arbitrary", "arbitrary"),
        name="out_route",
    )(merged, x, mod, g_post1.reshape(1, d), g_pre2.reshape(1, d), w_o_b, w_r, b_r)


def _dispatch_kernel(pos_ref, zf_ref, h_ref, xs_ref, zbuf, sem, zsem, *, tm, n_tiles):
    i = pl.program_id(0)

    def zero_copy(t):
        return pltpu.make_async_copy(zbuf, xs_ref.at[pl.ds(t * EXPERT_TILE, EXPERT_TILE)], zsem)

    @pl.when(i == 0)
    def _():
        zbuf[...] = jnp.zeros_like(zbuf)

        def issue(t, c):
            @pl.when(zf_ref[t] != 0)
            def _():
                zero_copy(t).start()
            return c

        def drain(t, c):
            @pl.when(zf_ref[t] != 0)
            def _():
                zero_copy(t).wait()
            return c

        lax.fori_loop(0, n_tiles, issue, 0)
        lax.fori_loop(0, n_tiles, drain, 0)

    def row_copy(r, p):
        return pltpu.make_async_copy(h_ref.at[pl.ds(r, 1)], xs_ref.at[pl.ds(p, 1)], sem)

    def issue_rows(r, c):
        for k in range(TOP_K):
            row_copy(r, pos_ref[TOP_K * (i * tm + r) + k]).start()
        return c

    def drain_rows(r, c):
        for k in range(TOP_K):
            row_copy(r, pos_ref[TOP_K * (i * tm + r) + k]).wait()
        return c

    lax.fori_loop(0, tm, issue_rows, 0)
    lax.fori_loop(0, tm, drain_rows, 0)


def _dispatch(h2, pos, zflag, n_rows):
    n, d = h2.shape
    tm = _tile(n, 256)
    n_tiles = n_rows // EXPERT_TILE
    return pl.pallas_call(
        functools.partial(_dispatch_kernel, tm=tm, n_tiles=n_tiles),
        grid_spec=pltpu.PrefetchScalarGridSpec(
            num_scalar_prefetch=2,
            grid=(n // tm,),
            in_specs=[pl.BlockSpec((tm, d), lambda i, p, z: (i, 0))],
            out_specs=pl.BlockSpec(memory_space=pl.ANY),
            scratch_shapes=[pltpu.VMEM((EXPERT_TILE, d), F32),
                            pltpu.SemaphoreType.DMA(()),
                            pltpu.SemaphoreType.DMA(())]),
        out_shape=jax.ShapeDtypeStruct((n_rows, d), F32),
        compiler_params=_params("arbitrary"),
        name="dispatch",
    )(pos, zflag, h2)


def _cast_rows(src_ref, dst_ref, chunk=256):
    rows = dst_ref.shape[0]
    chunk = min(chunk, rows)

    def body(c, carry):
        r0 = pl.multiple_of(c * chunk, chunk)
        dst_ref[pl.ds(r0, chunk), :] = src_ref[0, pl.ds(r0, chunk), :].astype(BF16)
        return carry

    lax.fori_loop(0, rows // chunk, body, 0)


def _m1_kernel(te_ref, chg_ref, nt_ref, xs_ref, wg_ref, wu_ref, o_ref, wgb, wub):
    t = pl.program_id(0)

    @pl.when(t < nt_ref[0])
    def _():
        @pl.when(chg_ref[t] != 0)
        def _():
            _cast_rows(wg_ref, wgb)
            _cast_rows(wu_ref, wub)

        x = xs_ref[...].astype(BF16)
        g = _dot(x, wgb[...])
        u = _dot(x, wub[...])
        o_ref[...] = (g * jax.nn.sigmoid(g) * u).astype(BF16)

    @pl.when(t >= nt_ref[0])
    def _():
        o_ref[...] = jnp.zeros_like(o_ref)


def _expert_up(xs, w_gate, w_up, te, chg, nt):
    n_rows, d = xs.shape
    n_e, _, de = w_gate.shape
    wspec = pl.BlockSpec((1, d, de), lambda t, te, chg, nt: (te[t], 0, 0))
    return pl.pallas_call(
        _m1_kernel,
        grid_spec=pltpu.PrefetchScalarGridSpec(
            num_scalar_prefetch=3,
            grid=(n_rows // EXPERT_TILE,),
            in_specs=[pl.BlockSpec((EXPERT_TILE, d), lambda t, te, chg, nt: (t, 0)), wspec, wspec],
            out_specs=pl.BlockSpec((EXPERT_TILE, de), lambda t, te, chg, nt: (t, 0)),
            scratch_shapes=[pltpu.VMEM((d, de), BF16), pltpu.VMEM((d, de), BF16)]),
        out_shape=jax.ShapeDtypeStruct((n_rows, de), BF16),
        compiler_params=_params("arbitrary"),
        name="expert_up",
    )(te, chg, nt, xs, w_gate, w_up)


def _m2_kernel(te_ref, chg_ref, nt_ref, h_ref, wd_ref, o_ref, wdb):
    t = pl.program_id(0)

    @pl.when(t < nt_ref[0])
    def _():
        @pl.when(chg_ref[t] != 0)
        def _():
            _cast_rows(wd_ref, wdb)

        o_ref[...] = _dot(h_ref[...], wdb[...])

    @pl.when(t >= nt_ref[0])
    def _():
        o_ref[...] = jnp.zeros_like(o_ref)


def _expert_down(hid, w_down, te, chg, nt):
    n_rows, de = hid.shape
    d = w_down.shape[2]
    return pl.pallas_call(
        _m2_kernel,
        grid_spec=pltpu.PrefetchScalarGridSpec(
            num_scalar_prefetch=3,
            grid=(n_rows // EXPERT_TILE,),
            in_specs=[pl.BlockSpec((EXPERT_TILE, de), lambda t, te, chg, nt: (t, 0)),
                      pl.BlockSpec((1, de, d), lambda t, te, chg, nt: (te[t], 0, 0))],
            out_specs=pl.BlockSpec((EXPERT_TILE, d), lambda t, te, chg, nt: (t, 0)),
            scratch_shapes=[pltpu.VMEM((de, d), BF16)]),
        out_shape=jax.ShapeDtypeStruct((n_rows, d), F32),
        compiler_params=_params("arbitrary"),
        name="expert_down",
    )(te, chg, nt, hid, w_down)


def _combine_kernel(pos_ref, x1_ref, ri_ref, mod_ref, gp_ref, ys_ref, o_ref, buf0, buf1, sem, *, tm, n_l):
    base = (pl.program_id(0) * n_l + pl.program_id(1)) * tm
    bufs = (buf0, buf1)

    def row_copy(r, k):
        p = pos_ref[TOP_K * (base + r) + k]
        return pltpu.make_async_copy(ys_ref.at[pl.ds(p, 1)], bufs[k].at[pl.ds(r, 1)], sem)

    def issue(r, c):
        for k in range(TOP_K):
            row_copy(r, k).start()
        return c

    def drain(r, c):
        for k in range(TOP_K):
            row_copy(r, k).wait()
        return c

    lax.fori_loop(0, tm, issue, 0)
    lax.fori_loop(0, tm, drain, 0)

    ri = ri_ref[0]
    moe = ri[:, R_W0:R_W0 + 1] * buf0[...] + ri[:, R_W1:R_W1 + 1] * buf1[...]
    o_ref[0] = x1_ref[0] + mod_ref[0, 5:6, :] * _rms(moe, gp_ref[...])


def _combine(x1, rinfo, mod, g_post2, ys, pos):
    bsz, l, d = x1.shape
    tm = _tile(l, 256)
    n_l = l // tm
    tok = lambda b, i, p: (b, i, 0)
    return pl.pallas_call(
        functools.partial(_combine_kernel, tm=tm, n_l=n_l),
        grid_spec=pltpu.PrefetchScalarGridSpec(
            num_scalar_prefetch=1,
            grid=(bsz, n_l),
            in_specs=[pl.BlockSpec((1, tm, d), tok),
                      pl.BlockSpec((1, tm, LANES), tok),
                      pl.BlockSpec((1, N_MOD, d), lambda b, i, p: (b, 0, 0)),
                      pl.BlockSpec((1, d), lambda b, i, p: (0, 0)),
                      pl.BlockSpec(memory_space=pl.ANY)],
            out_specs=pl.BlockSpec((1, tm, d), tok),
            scratch_shapes=[pltpu.VMEM((tm, d), F32), pltpu.VMEM((tm, d), F32),
                            pltpu.SemaphoreType.DMA(())]),
        out_shape=jax.ShapeDtypeStruct((bsz, l, d), F32),
        compiler_params=_params("arbitrary", "arbitrary"),
        name="combine",
    )(pos, x1, rinfo, mod, g_post2.reshape(1, d), ys)


def _rope_tables(seq, head_dim):
    axis_dim = head_dim // 2
    t = jnp.arange(seq, dtype=jnp.int32)
    pos = jnp.stack([t // GRID_W, t % GRID_W], axis=-1).astype(F32)
    inv_freq = ROPE_THETA ** (-jnp.arange(0, axis_dim, 2, dtype=F32) / axis_dim)
    ang = pos[:, :, None] * inv_freq
    cos, sin = jnp.cos(ang), jnp.sin(ang)
    cos_t = jnp.concatenate([cos, cos], axis=-1).reshape(seq, head_dim)
    sin_t = jnp.concatenate([-sin, sin], axis=-1).reshape(seq, head_dim)
    return cos_t, sin_t


def _route_plan(rinfo, cnt, n_groups, n_experts, n_tiles):
    ri = rinfo.reshape(-1, LANES)
    e = ri[:, R_E0:R_E1 + 1].astype(jnp.int32)
    rank = ri[:, R_RANK0:R_RANK1 + 1].astype(jnp.int32)
    counts = cnt[0, n_groups:n_groups + n_experts].astype(jnp.int32)
    tiles_e = (counts + EXPERT_TILE - 1) // EXPERT_TILE
    tile_end = jnp.cumsum(tiles_e)
    tile_start = tile_end - tiles_e
    nt = tile_end[-1]
    pos = (tile_start[e] * EXPERT_TILE + rank).reshape(-1)
    t = jnp.arange(n_tiles, dtype=jnp.int32)
    te = jnp.searchsorted(tile_end, jnp.minimum(t, nt - 1), side="right").astype(jnp.int32)
    chg = jnp.concatenate([jnp.ones((1,), jnp.int32), (te[1:] != te[:-1]).astype(jnp.int32)])
    partial_last = (t == tile_end[te] - 1) & (counts[te] % EXPERT_TILE != 0)
    zflag = ((t >= nt) | partial_last).astype(jnp.int32)
    return pos, te, chg, nt.reshape(1), zflag


def kernel(x, c, ctx, c_ctx, w_ada, b_ada, g_pre1, g_post1, g_pre2, g_post2, w_in, q_norm, k_norm,
           gm_ln, w_s, b_s, w_ba, w_bg, w_o, w_rg, b_rg, w_re, b_re, w_gate, w_up, w_down):
    bsz, seq, d = x.shape
    depth = w_ada.shape[0]
    head_dim = q_norm.shape[-1]
    q_w, gm_w = w_ba.shape[1], w_bg.shape[1]
    kv_w = (w_in.shape[2] - q_w - 2 * gm_w - 2 * d) // 2
    n_groups, per_group = w_re.shape[2], w_re.shape[3]
    n_experts = n_groups * per_group
    assert head_dim == LANES and w_s.shape[2] == LANES and gm_w // w_s.shape[1] == LANES
    assert n_groups + n_experts <= LANES and seq % GRID_W == 0
    col_q = 2 * kv_w
    col_u, col_v = col_q + q_w, col_q + q_w + gm_w
    col_ga, col_gg = col_v + gm_w, col_v + gm_w + d
    n_tok = bsz * seq
    n_rows = n_tok * TOP_K + n_experts * EXPERT_TILE
    n_tiles = n_rows // EXPERT_TILE

    cos_t, sin_t = _rope_tables(seq, head_dim)
    pad = (-(bsz + 1)) % (2 * SUBLANES)
    cs = jnp.concatenate([c, c_ctx[None, :], jnp.zeros((pad, d), F32)], axis=0)

    for l in range(depth):
        assert l + 1 == depth, "context-stream update for non-final layers is not implemented"
        mod = _ada(cs, w_ada[l], b_ada[l]).reshape(cs.shape[0], N_MOD, d)
        w_in_b = w_in[l].astype(BF16)
        w_ba_b, w_bg_b, w_o_b = w_ba[l].astype(BF16), w_bg[l].astype(BF16), w_o[l].astype(BF16)
        bs_full = jnp.repeat(b_s[l].T, LANES, axis=1)
        w_r = jnp.concatenate([w_rg[l], w_re[l].reshape(d, n_experts),
                               jnp.zeros((d, LANES - n_groups - n_experts), F32)], axis=1)
        b_r = jnp.concatenate([b_rg[l], b_re[l].reshape(n_experts),
                               jnp.zeros((LANES - n_groups - n_experts,), F32)]).reshape(1, LANES)

        hc = _prenorm(ctx, mod, g_pre1[l], lambda b: bsz)
        hx = _prenorm(x, mod, g_pre1[l], lambda b: b)
        kc, vc = _project_kv(hc, w_in_b, k_norm[l], kv_w, None, None)
        kx, vx = _project_kv(hx, w_in_b, k_norm[l], kv_w, cos_t, sin_t)
        qx = _project_q(hx, w_in_b, q_norm[l], col_q, q_w, cos_t, sin_t, head_dim ** -0.5)
        gm = _gmlp(hx, w_in_b, gm_ln[l], w_s[l], bs_full, col_u, col_v, gm_w)
        attn = _attention(qx, kc, vc, kx, vx, kv_w // head_dim)
        merged = _merge(hx, attn, gm, w_in_b, w_ba_b, w_bg_b, col_ga, col_gg)

        x1, h2, rinfo, cnt = _out_route(merged, x, mod, g_post1[l], g_pre2[l], w_o_b, w_r, b_r,
                                        n_groups, per_group)
        pos, te, chg, nt, zflag = _route_plan(rinfo, cnt, n_groups, n_experts, n_tiles)

        xs = _dispatch(h2.reshape(n_tok, d), pos, zflag, n_rows)
        hid = _expert_up(xs, w_gate[l], w_up[l], te, chg, nt)
        ys = _expert_down(hid, w_down[l], te, chg, nt)
        x = _combine(x1, rinfo, mod, g_post2[l], ys, pos)
    return x
```

```python
import functools

import jax
import jax.numpy as jnp
from jax import lax
from jax.experimental import pallas as pl
from jax.experimental.pallas import tpu as pltpu

GRID_W = 64
ROPE_THETA = 10000.0
EPS = 1e-6
N_MOD = 6
TOP_K = 2

LANES = 128
SUBLANES = 8
VMEM_LIMIT_BYTES = 56 * 1024 * 1024

EXPERT_TILE = 256
DMA_UNROLL = SUBLANES

F32 = jnp.float32
BF16 = jnp.bfloat16


def _params(*sem):
    return pltpu.CompilerParams(dimension_semantics=sem, vmem_limit_bytes=VMEM_LIMIT_BYTES)


def _dot(a, b):
    return jnp.dot(a, b, preferred_element_type=F32)


def _split_bf16(a):
    hi = a.astype(BF16)
    lo = (a - hi.astype(F32)).astype(BF16)
    return hi, lo


def _dot3(a, w):
    a_hi, a_lo = _split_bf16(a)
    w_hi, w_lo = _split_bf16(w)
    return _dot(a_hi, w_hi) + _dot(a_lo, w_hi) + _dot(a_hi, w_lo)


def _rms(x, g):
    return x * lax.rsqrt(jnp.mean(x * x, axis=-1, keepdims=True) + EPS) * g


def _tile(n, pref):
    t = min(n, pref)
    while n % t:
        t //= 2
    return t


def _ada_kernel(c_ref, w_ref, b_ref, o_ref):
    c = c_ref[...]
    a = c * jax.nn.sigmoid(c)
    o_ref[...] = _dot3(a, w_ref[...]) + b_ref[...]


def _ada(cs, w, b):
    m, d = cs.shape
    n = w.shape[1]
    tn = _tile(n, 1024)
    return pl.pallas_call(
        _ada_kernel,
        grid=(n // tn,),
        in_specs=[pl.BlockSpec((m, d), lambda j: (0, 0)),
                  pl.BlockSpec((d, tn), lambda j: (0, j)),
                  pl.BlockSpec((1, tn), lambda j: (0, j))],
        out_specs=pl.BlockSpec((m, tn), lambda j: (0, j)),
        out_shape=jax.ShapeDtypeStruct((m, n), F32),
        compiler_params=_params("arbitrary"),
        name="ada",
    )(cs, w, b.reshape(1, n))


def _prenorm_kernel(x_ref, mod_ref, g_ref, o_ref):
    y = _rms(x_ref[0], g_ref[...])
    o_ref[0] = (y * (1.0 + mod_ref[0, 1:2, :]) + mod_ref[0, 0:1, :]).astype(BF16)


def _prenorm(x, mod, g, mod_row):
    bsz, l, d = x.shape
    tm = _tile(l, 512)
    return pl.pallas_call(
        _prenorm_kernel,
        grid=(bsz, l // tm),
        in_specs=[pl.BlockSpec((1, tm, d), lambda b, i: (b, i, 0)),
                  pl.BlockSpec((1, N_MOD, d), lambda b, i: (mod_row(b), 0, 0)),
                  pl.BlockSpec((1, d), lambda b, i: (0, 0))],
        out_specs=pl.BlockSpec((1, tm, d), lambda b, i: (b, i, 0)),
        out_shape=jax.ShapeDtypeStruct((bsz, l, d), BF16),
        compiler_params=_params("arbitrary", "arbitrary"),
        name="prenorm",
    )(x, mod, g.reshape(1, d))


def _swap32(x):
    lane = lax.broadcasted_iota(jnp.int32, x.shape, 1)
    fwd = pltpu.roll(x, LANES - 32, 1)
    bwd = pltpu.roll(x, 32, 1)
    return jnp.where((lane & 32) == 0, fwd, bwd)


def _norm_head(r, gain, cos, sin, scale):
    y = _rms(r, gain)
    if cos is not None:
        y = y * cos + _swap32(y) * sin
    if scale != 1.0:
        y = y * scale
    return y.astype(BF16)


def _kv_kernel(*refs, n_kv, rope):
    if rope:
        h_ref, w_ref, g_ref, cos_ref, sin_ref, k_ref, v_ref = refs
        cos, sin = cos_ref[...], sin_ref[...]
    else:
        h_ref, w_ref, g_ref, k_ref, v_ref = refs
        cos = sin = None
    kv_w = n_kv * LANES
    r = _dot(h_ref[0], w_ref[...])
    for hh in range(n_kv):
        sl = slice(hh * LANES, (hh + 1) * LANES)
        k_ref[0, :, sl] = _norm_head(r[:, sl], g_ref[...], cos, sin, 1.0)
    v_ref[0] = r[:, kv_w:].astype(BF16)


def _project_kv(h, w_in_b, k_norm, kv_w, cos, sin):
    bsz, l, d = h.shape
    tm = _tile(l, 512)
    rope = cos is not None
    in_specs = [pl.BlockSpec((1, tm, d), lambda b, i: (b, i, 0)),
                pl.BlockSpec((d, 2 * kv_w), lambda b, i: (0, 0)),
                pl.BlockSpec((1, LANES), lambda b, i: (0, 0))]
    args = [h, w_in_b, k_norm.reshape(1, LANES)]
    if rope:
        in_specs += [pl.BlockSpec((tm, LANES), lambda b, i: (i, 0))] * 2
        args += [cos, sin]
    out = jax.ShapeDtypeStruct((bsz, l, kv_w), BF16)
    return pl.pallas_call(
        functools.partial(_kv_kernel, n_kv=kv_w // LANES, rope=rope),
        grid=(bsz, l // tm),
        in_specs=in_specs,
        out_specs=[pl.BlockSpec((1, tm, kv_w), lambda b, i: (b, i, 0))] * 2,
        out_shape=[out, out],
        compiler_params=_params("arbitrary", "arbitrary"),
        name="proj_kv",
    )(*args)


def _q_kernel(h_ref, w_ref, g_ref, cos_ref, sin_ref, q_ref, *, n_heads, scale):
    r = _dot(h_ref[0], w_ref[...])
    cos, sin = cos_ref[...], sin_ref[...]
    for hh in range(n_heads):
        sl = slice(hh * LANES, (hh + 1) * LANES)
        q_ref[0, :, sl] = _norm_head(r[:, sl], g_ref[...], cos, sin, scale)


def _project_q(h, w_in_b, q_norm, col0, q_w, cos, sin, scale):
    bsz, l, d = h.shape
    tm = _tile(l, 512)
    tn = _tile(q_w, 512)
    assert col0 % tn == 0
    c0 = col0 // tn
    return pl.pallas_call(
        functools.partial(_q_kernel, n_heads=tn // LANES, scale=scale),
        grid=(bsz, l // tm, q_w // tn),
        in_specs=[pl.BlockSpec((1, tm, d), lambda b, i, j: (b, i, 0)),
                  pl.BlockSpec((d, tn), lambda b, i, j: (0, c0 + j)),
                  pl.BlockSpec((1, LANES), lambda b, i, j: (0, 0)),
                  pl.BlockSpec((tm, LANES), lambda b, i, j: (i, 0)),
                  pl.BlockSpec((tm, LANES), lambda b, i, j: (i, 0))],
        out_specs=pl.BlockSpec((1, tm, tn), lambda b, i, j: (b, i, j)),
        out_shape=jax.ShapeDtypeStruct((bsz, l, q_w), BF16),
        compiler_params=_params("arbitrary", "arbitrary", "arbitrary"),
        name="proj_q",
    )(h, w_in_b, q_norm.reshape(1, LANES), cos, sin)


def _gelu(x):
    c = 0.7978845608028654
    return x * (0.5 * (1.0 + jnp.tanh(c * (x + 0.044715 * (x * x * x)))))


def _gmlp_kernel(h_ref, wu_ref, wv_ref, ln_ref, ws_ref, bs_ref, o_ref, *, n_grp, n_chunk):
    h = h_ref[0]
    gu = _gelu(_dot(h, wu_ref[...]))
    gv = _gelu(_dot(h, wv_ref[...]))
    for g in range(n_grp):
        cs = slice(g * LANES, (g + 1) * LANES)
        v = gv[:, cs]
        vc = v - jnp.mean(v, axis=-1, keepdims=True)
        vn = vc * lax.rsqrt(jnp.mean(vc * vc, axis=-1, keepdims=True) + EPS) * ln_ref[:, cs]
        vn = vn.astype(BF16)
        w = ws_ref[g].astype(BF16)
        for c in range(n_chunk):
            rs = slice(c * LANES, (c + 1) * LANES)
            s = _dot(w, vn[rs, :]) + bs_ref[:, cs]
            o_ref[0, rs, cs] = (gu[rs, cs] * s).astype(BF16)


def _gmlp(h, w_in_b, gm_ln, w_s, bs_full, col_u, col_v, gm_w):
    bsz, l, d = h.shape
    tm = _tile(l, 512)
    tn = _tile(gm_w, 512)
    assert col_u % tn == 0 and col_v % tn == 0 and tm % LANES == 0
    cu, cv = col_u // tn, col_v // tn
    n_grp = tn // LANES
    return pl.pallas_call(
        functools.partial(_gmlp_kernel, n_grp=n_grp, n_chunk=tm // LANES),
        grid=(bsz, l // tm, gm_w // tn),
        in_specs=[pl.BlockSpec((1, tm, d), lambda b, i, j: (b, i, 0)),
                  pl.BlockSpec((d, tn), lambda b, i, j: (0, cu + j)),
                  pl.BlockSpec((d, tn), lambda b, i, j: (0, cv + j)),
                  pl.BlockSpec((1, tn), lambda b, i, j: (0, j)),
                  pl.BlockSpec((n_grp, LANES, LANES), lambda b, i, j: (j, 0, 0)),
                  pl.BlockSpec((LANES, tn), lambda b, i, j: (0, j))],
        out_specs=pl.BlockSpec((1, tm, tn), lambda b, i, j: (b, i, j)),
        out_shape=jax.ShapeDtypeStruct((bsz, l, gm_w), BF16),
        compiler_params=_params("arbitrary", "arbitrary", "arbitrary"),
        name="gmlp",
    )(h, w_in_b, w_in_b, gm_ln.reshape(1, gm_w), w_s, bs_full)


def _attn_kernel(q_ref, kc_ref, vc_ref, kx_ref, vx_ref, o_ref, *, grp):
    kc, vc, kx, vx = kc_ref[0], vc_ref[0], kx_ref[0], vx_ref[0]
    nt = (((1,), (1,)), ((), ()))
    for hh in range(grp):
        sl = slice(hh * LANES, (hh + 1) * LANES)
        q = q_ref[0, :, sl]
        sc = lax.dot_general(q, kc, nt, preferred_element_type=F32)
        sx = lax.dot_general(q, kx, nt, preferred_element_type=F32)
        m = jnp.maximum(jnp.max(sc, axis=-1, keepdims=True), jnp.max(sx, axis=-1, keepdims=True))
        pc = jnp.exp(sc - m)
        px = jnp.exp(sx - m)
        den = jnp.sum(pc, axis=-1, keepdims=True) + jnp.sum(px, axis=-1, keepdims=True)
        o = _dot(pc.astype(BF16), vc) + _dot(px.astype(BF16), vx)
        o_ref[0, :, sl] = (o / den).astype(BF16)


def _attention(q, kc, vc, kx, vx, n_kv):
    bsz, l, q_w = q.shape
    lc = kc.shape[1]
    gw = q_w // n_kv
    tq = _tile(l, 256)
    return pl.pallas_call(
        functools.partial(_attn_kernel, grp=gw // LANES),
        grid=(bsz, n_kv, l // tq),
        in_specs=[pl.BlockSpec((1, tq, gw), lambda b, k, i: (b, i, k)),
                  pl.BlockSpec((1, lc, LANES), lambda b, k, i: (b, 0, k)),
                  pl.BlockSpec((1, lc, LANES), lambda b, k, i: (b, 0, k)),
                  pl.BlockSpec((1, l, LANES), lambda b, k, i: (b, 0, k)),
                  pl.BlockSpec((1, l, LANES), lambda b, k, i: (b, 0, k))],
        out_specs=pl.BlockSpec((1, tq, gw), lambda b, k, i: (b, i, k)),
        out_shape=jax.ShapeDtypeStruct((bsz, l, q_w), BF16),
        compiler_params=_params("arbitrary", "arbitrary", "arbitrary"),
        name="attention",
    )(q, kc, vc, kx, vx)


def _merge_kernel(h_ref, a_ref, g_ref, wga_ref, wgg_ref, wba_ref, wbg_ref, o_ref):
    h = h_ref[0]
    ga = jax.nn.sigmoid(_dot(h, wga_ref[...]))
    gg = jax.nn.sigmoid(_dot(h, wgg_ref[...]))
    pa = _dot(a_ref[0], wba_ref[...])
    pg = _dot(g_ref[0], wbg_ref[...])
    o_ref[0] = (ga * pa + gg * pg).astype(BF16)


def _merge(h, attn, gm, w_in_b, w_ba_b, w_bg_b, col_ga, col_gg):
    bsz, l, d = h.shape
    q_w, gm_w = attn.shape[2], gm.shape[2]
    tm = _tile(l, 512)
    tn = _tile(d, 512)
    assert col_ga % tn == 0 and col_gg % tn == 0
    ca, cg = col_ga // tn, col_gg // tn
    return pl.pallas_call(
        _merge_kernel,
        grid=(bsz, l // tm, d // tn),
        in_specs=[pl.BlockSpec((1, tm, d), lambda b, i, j: (b, i, 0)),
                  pl.BlockSpec((1, tm, q_w), lambda b, i, j: (b, i, 0)),
                  pl.BlockSpec((1, tm, gm_w), lambda b, i, j: (b, i, 0)),
                  pl.BlockSpec((d, tn), lambda b, i, j: (0, ca + j)),
                  pl.BlockSpec((d, tn), lambda b, i, j: (0, cg + j)),
                  pl.BlockSpec((q_w, tn), lambda b, i, j: (0, j)),
                  pl.BlockSpec((gm_w, tn), lambda b, i, j: (0, j))],
        out_specs=pl.BlockSpec((1, tm, tn), lambda b, i, j: (b, i, j)),
        out_shape=jax.ShapeDtypeStruct((bsz, l, d), BF16),
        compiler_params=_params("arbitrary", "arbitrary", "arbitrary"),
        name="merge",
    )(h, attn, gm, w_in_b, w_in_b, w_ba_b, w_bg_b)


R_W0, R_W1 = range(2)
T_E0, T_E1, T_R0_HI, T_R0_LO, T_R1_HI, T_R1_LO = range(6)


def _out_kernel(m_ref, x_ref, mod_ref, gp1_ref, gp2_ref, wo_ref, wr_ref, br_ref,
                x1_ref, h2_ref, ri_ref, rt_ref, cnt_ref, carry_ref, *, n_groups, per_group):
    @pl.when((pl.program_id(0) == 0) & (pl.program_id(1) == 0))
    def _():
        carry_ref[...] = jnp.zeros_like(carry_ref)

    mix = _dot(m_ref[0], wo_ref[...])
    x1 = x_ref[0] + mod_ref[0, 2:3, :] * _rms(mix, gp1_ref[...])
    x1_ref[0] = x1
    h2 = _rms(x1, gp2_ref[...]) * (1.0 + mod_ref[0, 4:5, :]) + mod_ref[0, 3:4, :]
    h2_ref[0] = h2

    logits = _dot3(h2, wr_ref[...]) + br_ref[...]
    tm = logits.shape[0]
    lane = lax.broadcasted_iota(jnp.int32, logits.shape, 1)
    neg = jnp.float32(-jnp.inf)
    lg = jnp.where(lane < n_groups, logits, neg)
    gmax = jnp.max(lg, axis=-1, keepdims=True)
    p_top = 1.0 / jnp.sum(jnp.exp(lg - gmax), axis=-1, keepdims=True)
    gidx = jnp.min(jnp.where(lg == gmax, lane, LANES), axis=-1, keepdims=True)
    lo = n_groups + gidx * per_group
    le = jnp.where((lane >= lo) & (lane < lo + per_group), logits, neg)
    l1 = jnp.max(le, axis=-1, keepdims=True)
    i1 = jnp.min(jnp.where(le == l1, lane, LANES), axis=-1, keepdims=True)
    le2 = jnp.where(lane == i1, neg, le)
    l2 = jnp.max(le2, axis=-1, keepdims=True)
    i2 = jnp.min(jnp.where(le2 == l2, lane, LANES), axis=-1, keepdims=True)
    r = jnp.exp(l2 - l1)
    w0 = p_top / (1.0 + r)
    w1 = p_top * r / (1.0 + r)

    oh1 = lane == i1
    oh2 = lane == i2
    oh = jnp.where(oh1 | oh2, 1.0, 0.0)
    row = lax.broadcasted_iota(jnp.int32, (tm, tm), 0)
    col = lax.broadcasted_iota(jnp.int32, (tm, tm), 1)
    tri = jnp.where(col < row, 1.0, 0.0).astype(BF16)
    base = _dot(tri, oh.astype(BF16)) + carry_ref[...]
    rank0 = jnp.sum(jnp.where(oh1, base, 0.0), axis=-1, keepdims=True)
    rank1 = jnp.sum(jnp.where(oh2, base, 0.0), axis=-1, keepdims=True)
    carry_ref[...] += jnp.sum(oh, axis=0, keepdims=True)
    cnt_ref[...] = carry_ref[...]

    e0 = (i1 - n_groups).astype(F32)
    e1 = (i2 - n_groups).astype(F32)
    rec = jnp.zeros_like(logits)
    for idx, val in ((R_W0, w0), (R_W1, w1)):
        rec = jnp.where(lane == idx, val, rec)
    ri_ref[0] = rec

    r0_hi = jnp.floor(rank0 * (1.0 / 256.0))
    r1_hi = jnp.floor(rank1 * (1.0 / 256.0))
    ints = jnp.zeros_like(logits)
    for idx, val in ((T_E0, e0), (T_E1, e1), (T_R0_HI, r0_hi), (T_R0_LO, rank0 - 256.0 * r0_hi),
                     (T_R1_HI, r1_hi), (T_R1_LO, rank1 - 256.0 * r1_hi)):
        ints = jnp.where(lane == idx, val, ints)
    sel = jnp.where(lax.broadcasted_iota(jnp.int32, (SUBLANES, LANES), 0)
                    == lax.broadcasted_iota(jnp.int32, (SUBLANES, LANES), 1), 1.0, 0.0).astype(BF16)
    rt = lax.dot_general(sel, ints.astype(BF16), (((1,), (1,)), ((), ())), preferred_element_type=F32)
    rt_ref[...] = rt.astype(jnp.int32)


def _out_route(merged, x, mod, g_post1, g_pre2, w_o_b, w_r, b_r, n_groups, per_group):
    bsz, l, d = x.shape
    tm = _tile(l, 512)
    tok = lambda b, i: (b, i, 0)
    const2 = lambda b, i: (0, 0)
    return pl.pallas_call(
        functools.partial(_out_kernel, n_groups=n_groups, per_group=per_group),
        grid=(bsz, l // tm),
        in_specs=[pl.BlockSpec((1, tm, d), tok),
                  pl.BlockSpec((1, tm, d), tok),
                  pl.BlockSpec((1, N_MOD, d), lambda b, i: (b, 0, 0)),
                  pl.BlockSpec((1, d), const2),
                  pl.BlockSpec((1, d), const2),
                  pl.BlockSpec((d, d), const2),
                  pl.BlockSpec((d, LANES), const2),
                  pl.BlockSpec((1, LANES), const2)],
        out_specs=[pl.BlockSpec((1, tm, d), tok),
                   pl.BlockSpec((1, tm, d), tok),
                   pl.BlockSpec((1, tm, LANES), tok),
                   pl.BlockSpec((SUBLANES, tm), lambda b, i: (0, b * (l // tm) + i)),
                   pl.BlockSpec((1, LANES), const2)],
        out_shape=[jax.ShapeDtypeStruct((bsz, l, d), F32),
                   jax.ShapeDtypeStruct((bsz, l, d), F32),
                   jax.ShapeDtypeStruct((bsz, l, LANES), F32),
                   jax.ShapeDtypeStruct((SUBLANES, bsz * l), jnp.int32),
                   jax.ShapeDtypeStruct((1, LANES), F32)],
        scratch_shapes=[pltpu.VMEM((1, LANES), F32)],
        compiler_params=_params("arbitrary", "arbitrary"),
        name="out_route",
    )(merged, x, mod, g_post1.reshape(1, d), g_pre2.reshape(1, d), w_o_b, w_r, b_r)


def _dispatch_kernel(pos_ref, zf_ref, h_ref, xs_ref, zbuf, sem, zsem, *, tm, n_tiles, n_tok):
    i = pl.program_id(0)

    def zero_copy(t):
        return pltpu.make_async_copy(zbuf, xs_ref.at[pl.ds(t * EXPERT_TILE, EXPERT_TILE)], zsem)

    @pl.when(i == 0)
    def _():
        zbuf[...] = jnp.zeros_like(zbuf)

        def issue(t, c):
            @pl.when(zf_ref[t] != 0)
            def _():
                zero_copy(t).start()
            return c

        def drain(t, c):
            @pl.when(zf_ref[t] != 0)
            def _():
                zero_copy(t).wait()
            return c

        lax.fori_loop(0, n_tiles, issue, 0)
        lax.fori_loop(0, n_tiles, drain, 0)

    def issue_rows(rb, c):
        for u in range(DMA_UNROLL):
            r = rb * DMA_UNROLL + u
            for k in range(TOP_K):
                p = pos_ref[k * n_tok + i * tm + r]
                pltpu.make_async_copy(h_ref.at[pl.ds(r, 1)], xs_ref.at[pl.ds(p, 1)], sem).start()
        return c

    lax.fori_loop(0, tm // DMA_UNROLL, issue_rows, 0)
    for k in range(TOP_K):
        pltpu.make_async_copy(h_ref, xs_ref.at[pl.ds(0, tm)], sem).wait()


def _dispatch(h2, pos, zflag, n_rows):
    n, d = h2.shape
    tm = _tile(n, 256)
    n_tiles = n_rows // EXPERT_TILE
    return pl.pallas_call(
        functools.partial(_dispatch_kernel, tm=tm, n_tiles=n_tiles, n_tok=n),
        grid_spec=pltpu.PrefetchScalarGridSpec(
            num_scalar_prefetch=2,
            grid=(n // tm,),
            in_specs=[pl.BlockSpec((tm, d), lambda i, p, z: (i, 0))],
            out_specs=pl.BlockSpec(memory_space=pl.ANY),
            scratch_shapes=[pltpu.VMEM((EXPERT_TILE, d), F32),
                            pltpu.SemaphoreType.DMA(()),
                            pltpu.SemaphoreType.DMA(())]),
        out_shape=jax.ShapeDtypeStruct((n_rows, d), F32),
        compiler_params=_params("arbitrary"),
        name="dispatch",
    )(pos, zflag, h2)


def _cast_rows(src_ref, dst_ref, chunk=256):
    rows = dst_ref.shape[0]
    chunk = min(chunk, rows)

    def body(c, carry):
        r0 = pl.multiple_of(c * chunk, chunk)
        dst_ref[pl.ds(r0, chunk), :] = src_ref[pl.ds(r0, chunk), :].astype(BF16)
        return carry

    lax.fori_loop(0, rows // chunk, body, 0)


def _stage_expert_weights(t, te_ref, nxt_ref, par_ref, w_hbm, wf, wb, sem):
    n_w = len(w_hbm)
    slot = par_ref[t]

    def copy(e, sl, w):
        return pltpu.make_async_copy(w_hbm[w].at[e], wf.at[sl, w], sem.at[sl])

    @pl.when(t == 0)
    def _():
        for w in range(n_w):
            copy(te_ref[0], slot, w).start()

    for w in range(n_w):
        copy(te_ref[t], slot, w).wait()

    @pl.when(nxt_ref[t] >= 0)
    def _():
        for w in range(n_w):
            copy(nxt_ref[t], 1 - slot, w).start()

    for w in range(n_w):
        _cast_rows(wf.at[slot, w], wb.at[w])


def _m1_kernel(te_ref, chg_ref, nxt_ref, par_ref, nt_ref, xs_ref, wg_hbm, wu_hbm, o_ref, wf, wb, sem):
    t = pl.program_id(0)

    @pl.when(t < nt_ref[0])
    def _():
        @pl.when(chg_ref[t] != 0)
        def _():
            _stage_expert_weights(t, te_ref, nxt_ref, par_ref, (wg_hbm, wu_hbm), wf, wb, sem)

        x = xs_ref[...].astype(BF16)
        g = _dot(x, wb[0])
        u = _dot(x, wb[1])
        o_ref[...] = (g * jax.nn.sigmoid(g) * u).astype(BF16)

    @pl.when(t >= nt_ref[0])
    def _():
        o_ref[...] = jnp.zeros_like(o_ref)


def _expert_up(xs, w_gate, w_up, plan):
    n_rows, d = xs.shape
    de = w_gate.shape[2]
    tile = lambda t, *_: (t, 0)
    return pl.pallas_call(
        _m1_kernel,
        grid_spec=pltpu.PrefetchScalarGridSpec(
            num_scalar_prefetch=len(plan),
            grid=(n_rows // EXPERT_TILE,),
            in_specs=[pl.BlockSpec((EXPERT_TILE, d), tile),
                      pl.BlockSpec(memory_space=pl.ANY), pl.BlockSpec(memory_space=pl.ANY)],
            out_specs=pl.BlockSpec((EXPERT_TILE, de), tile),
            scratch_shapes=[pltpu.VMEM((2, 2, d, de), F32), pltpu.VMEM((2, d, de), BF16),
                            pltpu.SemaphoreType.DMA((2,))]),
        out_shape=jax.ShapeDtypeStruct((n_rows, de), BF16),
        compiler_params=_params("arbitrary"),
        name="expert_up",
    )(*plan, xs, w_gate, w_up)


def _m2_kernel(te_ref, chg_ref, nxt_ref, par_ref, nt_ref, h_ref, wd_hbm, o_ref, wf, wb, sem):
    t = pl.program_id(0)

    @pl.when(t < nt_ref[0])
    def _():
        @pl.when(chg_ref[t] != 0)
        def _():
            _stage_expert_weights(t, te_ref, nxt_ref, par_ref, (wd_hbm,), wf, wb, sem)

        o_ref[...] = _dot(h_ref[...], wb[0])

    @pl.when(t >= nt_ref[0])
    def _():
        o_ref[...] = jnp.zeros_like(o_ref)


def _expert_down(hid, w_down, plan):
    n_rows, de = hid.shape
    d = w_down.shape[2]
    tile = lambda t, *_: (t, 0)
    return pl.pallas_call(
        _m2_kernel,
        grid_spec=pltpu.PrefetchScalarGridSpec(
            num_scalar_prefetch=len(plan),
            grid=(n_rows // EXPERT_TILE,),
            in_specs=[pl.BlockSpec((EXPERT_TILE, de), tile), pl.BlockSpec(memory_space=pl.ANY)],
            out_specs=pl.BlockSpec((EXPERT_TILE, d), tile),
            scratch_shapes=[pltpu.VMEM((2, 1, de, d), F32), pltpu.VMEM((1, de, d), BF16),
                            pltpu.SemaphoreType.DMA((2,))]),
        out_shape=jax.ShapeDtypeStruct((n_rows, d), F32),
        compiler_params=_params("arbitrary"),
        name="expert_down",
    )(*plan, hid, w_down)


def _combine_kernel(pos_ref, x1_ref, ri_ref, mod_ref, gp_ref, ys_ref, o_ref, buf, sem, *, tm, n_l, n_tok):
    step = pl.program_id(0) * n_l + pl.program_id(1)
    n_steps = pl.num_programs(0) * n_l
    slot = step % 2

    def issue(s, sl):
        def body(rb, c):
            for u in range(DMA_UNROLL):
                r = rb * DMA_UNROLL + u
                for k in range(TOP_K):
                    p = pos_ref[k * n_tok + s * tm + r]
                    pltpu.make_async_copy(ys_ref.at[pl.ds(p, 1)], buf.at[sl, k, pl.ds(r, 1)], sem.at[sl]).start()
            return c

        lax.fori_loop(0, tm // DMA_UNROLL, body, 0)

    @pl.when(step == 0)
    def _():
        issue(0, 0)

    @pl.when(step + 1 < n_steps)
    def _():
        issue(step + 1, 1 - slot)

    for k in range(TOP_K):
        pltpu.make_async_copy(ys_ref.at[pl.ds(0, tm)], buf.at[slot, k], sem.at[slot]).wait()

    ri = ri_ref[0]
    moe = ri[:, R_W0:R_W0 + 1] * buf[slot, 0] + ri[:, R_W1:R_W1 + 1] * buf[slot, 1]
    o_ref[0] = x1_ref[0] + mod_ref[0, 5:6, :] * _rms(moe, gp_ref[...])


def _combine(x1, rinfo, mod, g_post2, ys, pos):
    bsz, l, d = x1.shape
    tm = _tile(l, 256)
    n_l = l // tm
    tok = lambda b, i, p: (b, i, 0)
    return pl.pallas_call(
        functools.partial(_combine_kernel, tm=tm, n_l=n_l, n_tok=bsz * l),
        grid_spec=pltpu.PrefetchScalarGridSpec(
            num_scalar_prefetch=1,
            grid=(bsz, n_l),
            in_specs=[pl.BlockSpec((1, tm, d), tok),
                      pl.BlockSpec((1, tm, LANES), tok),
                      pl.BlockSpec((1, N_MOD, d), lambda b, i, p: (b, 0, 0)),
                      pl.BlockSpec((1, d), lambda b, i, p: (0, 0)),
                      pl.BlockSpec(memory_space=pl.ANY)],
            out_specs=pl.BlockSpec((1, tm, d), tok),
            scratch_shapes=[pltpu.VMEM((2, TOP_K, tm, d), F32),
                            pltpu.SemaphoreType.DMA((2,))]),
        out_shape=jax.ShapeDtypeStruct((bsz, l, d), F32),
        compiler_params=_params("arbitrary", "arbitrary"),
        name="combine",
    )(pos, x1, rinfo, mod, g_post2.reshape(1, d), ys)


def _rope_tables(seq, head_dim):
    axis_dim = head_dim // 2
    t = jnp.arange(seq, dtype=jnp.int32)
    pos = jnp.stack([t // GRID_W, t % GRID_W], axis=-1).astype(F32)
    inv_freq = ROPE_THETA ** (-jnp.arange(0, axis_dim, 2, dtype=F32) / axis_dim)
    ang = pos[:, :, None] * inv_freq
    cos, sin = jnp.cos(ang), jnp.sin(ang)
    cos_t = jnp.concatenate([cos, cos], axis=-1).reshape(seq, head_dim)
    sin_t = jnp.concatenate([-sin, sin], axis=-1).reshape(seq, head_dim)
    return cos_t, sin_t


def _route_plan(rt, cnt, n_groups, n_experts, n_tiles):
    e = rt[T_E0:T_E1 + 1]
    rank = jnp.stack([rt[T_R0_HI] * 256 + rt[T_R0_LO], rt[T_R1_HI] * 256 + rt[T_R1_LO]])
    counts = cnt[0, n_groups:n_groups + n_experts].astype(jnp.int32)
    tiles_e = (counts + EXPERT_TILE - 1) // EXPERT_TILE
    ids = jnp.arange(n_experts, dtype=jnp.int32)
    tile_end = jnp.sum(jnp.where(ids[None, :] <= ids[:, None], tiles_e[None, :], 0), axis=1)
    tile_start = tile_end - tiles_e
    nt = tile_end[-1]
    row0 = tile_start * EXPERT_TILE
    pos = jnp.sum(jnp.where(e[None] == ids[:, None, None], row0[:, None, None], 0), axis=0) + rank
    t = jnp.arange(n_tiles, dtype=jnp.int32)
    owner = lambda q: jnp.sum((tile_end[None, :] <= q[:, None]).astype(jnp.int32), axis=1)
    te = owner(jnp.minimum(t, nt - 1))
    chg = ((t == 0) | (te != owner(jnp.minimum(jnp.maximum(t - 1, 0), nt - 1)))).astype(jnp.int32)
    partial_last = jnp.any((tile_end[None, :] - 1 == t[:, None]) & (counts[None, :] % EXPERT_TILE != 0), axis=1)
    zflag = ((t >= nt) | partial_last).astype(jnp.int32)
    used = tiles_e > 0
    later = used[None, :] & (ids[None, :] > te[:, None])
    nxt = jnp.min(jnp.where(later, ids[None, :], n_experts), axis=1)
    nxt = jnp.where(nxt == n_experts, -1, nxt)
    par = jnp.sum((used[None, :] & (ids[None, :] < te[:, None])).astype(jnp.int32), axis=1) % 2
    return pos.reshape(-1), (te, chg, nxt, par, nt.reshape(1)), zflag


def kernel(x, c, ctx, c_ctx, w_ada, b_ada, g_pre1, g_post1, g_pre2, g_post2, w_in, q_norm, k_norm,
           gm_ln, w_s, b_s, w_ba, w_bg, w_o, w_rg, b_rg, w_re, b_re, w_gate, w_up, w_down):
    bsz, seq, d = x.shape
    depth = w_ada.shape[0]
    head_dim = q_norm.shape[-1]
    q_w, gm_w = w_ba.shape[1], w_bg.shape[1]
    kv_w = (w_in.shape[2] - q_w - 2 * gm_w - 2 * d) // 2
    n_groups, per_group = w_re.shape[2], w_re.shape[3]
    n_experts = n_groups * per_group
    assert head_dim == LANES and w_s.shape[2] == LANES and gm_w // w_s.shape[1] == LANES
    assert n_groups + n_experts <= LANES and seq % GRID_W == 0
    col_q = 2 * kv_w
    col_u, col_v = col_q + q_w, col_q + q_w + gm_w
    col_ga, col_gg = col_v + gm_w, col_v + gm_w + d
    n_tok = bsz * seq
    n_rows = n_tok * TOP_K + n_experts * EXPERT_TILE
    n_tiles = n_rows // EXPERT_TILE

    cos_t, sin_t = _rope_tables(seq, head_dim)
    pad = (-(bsz + 1)) % (2 * SUBLANES)
    cs = jnp.concatenate([c, c_ctx[None, :], jnp.zeros((pad, d), F32)], axis=0)

    for l in range(depth):
        assert l + 1 == depth, "context-stream update for non-final layers is not implemented"
        mod = _ada(cs, w_ada[l], b_ada[l]).reshape(cs.shape[0], N_MOD, d)
        w_in_b = w_in[l].astype(BF16)
        w_ba_b, w_bg_b, w_o_b = w_ba[l].astype(BF16), w_bg[l].astype(BF16), w_o[l].astype(BF16)
        bs_full = jnp.repeat(b_s[l].T, LANES, axis=1)
        w_r = jnp.concatenate([w_rg[l], w_re[l].reshape(d, n_experts),
                               jnp.zeros((d, LANES - n_groups - n_experts), F32)], axis=1)
        b_r = jnp.concatenate([b_rg[l], b_re[l].reshape(n_experts),
                               jnp.zeros((LANES - n_groups - n_experts,), F32)]).reshape(1, LANES)

        hc = _prenorm(ctx, mod, g_pre1[l], lambda b: bsz)
        hx = _prenorm(x, mod, g_pre1[l], lambda b: b)
        kc, vc = _project_kv(hc, w_in_b, k_norm[l], kv_w, None, None)
        kx, vx = _project_kv(hx, w_in_b, k_norm[l], kv_w, cos_t, sin_t)
        qx = _project_q(hx, w_in_b, q_norm[l], col_q, q_w, cos_t, sin_t, head_dim ** -0.5)
        gm = _gmlp(hx, w_in_b, gm_ln[l], w_s[l], bs_full, col_u, col_v, gm_w)
        attn = _attention(qx, kc, vc, kx, vx, kv_w // head_dim)
        merged = _merge(hx, attn, gm, w_in_b, w_ba_b, w_bg_b, col_ga, col_gg)

        x1, h2, rinfo, rt, cnt = _out_route(merged, x, mod, g_post1[l], g_pre2[l], w_o_b, w_r, b_r,
                                            n_groups, per_group)
        pos, plan, zflag = _route_plan(rt, cnt, n_groups, n_experts, n_tiles)

        xs = _dispatch(h2.reshape(n_tok, d), pos, zflag, n_rows)
        hid = _expert_up(xs, w_gate[l], w_up[l], plan)
        ys = _expert_down(hid, w_down[l], plan)
        x = _combine(x1, rinfo, mod, g_post2[l], ys, pos)
    return x
```

```python
import functools

import jax
import jax.numpy as jnp
from jax import lax
from jax.experimental import pallas as pl
from jax.experimental.pallas import tpu as pltpu

GRID_W = 64
ROPE_THETA = 10000.0
EPS = 1e-6
N_MOD = 6
TOP_K = 2
LOG2_E = 1.4426950408889634

LANES = 128
SUBLANES = 8
VMEM_LIMIT_BYTES = 56 * 1024 * 1024

EXPERT_TILE = 256
DMA_UNROLL = SUBLANES

F32 = jnp.float32
BF16 = jnp.bfloat16


def _params(*sem):
    return pltpu.CompilerParams(dimension_semantics=sem, vmem_limit_bytes=VMEM_LIMIT_BYTES)


def _dot(a, b):
    return jnp.dot(a, b, preferred_element_type=F32)


def _split_bf16(a):
    hi = a.astype(BF16)
    lo = (a - hi.astype(F32)).astype(BF16)
    return hi, lo


def _dot3(a, w):
    a_hi, a_lo = _split_bf16(a)
    w_hi, w_lo = _split_bf16(w)
    return _dot(a_hi, w_hi) + _dot(a_lo, w_hi) + _dot(a_hi, w_lo)


def _rms(x, g):
    return x * lax.rsqrt(jnp.mean(x * x, axis=-1, keepdims=True) + EPS) * g


def _tile(n, pref):
    t = min(n, pref)
    while n % t:
        t //= 2
    return t


def _ada_kernel(c_ref, w_ref, b_ref, o_ref):
    c = c_ref[...]
    a = c * jax.nn.sigmoid(c)
    o_ref[...] = _dot3(a, w_ref[...]) + b_ref[...]


def _ada(cs, w, b):
    m, d = cs.shape
    n = w.shape[1]
    tn = _tile(n, 1024)
    return pl.pallas_call(
        _ada_kernel,
        grid=(n // tn,),
        in_specs=[pl.BlockSpec((m, d), lambda j: (0, 0)),
                  pl.BlockSpec((d, tn), lambda j: (0, j)),
                  pl.BlockSpec((1, tn), lambda j: (0, j))],
        out_specs=pl.BlockSpec((m, tn), lambda j: (0, j)),
        out_shape=jax.ShapeDtypeStruct((m, n), F32),
        compiler_params=_params("arbitrary"),
        name="ada",
    )(cs, w, b.reshape(1, n))


def _prenorm_kernel(x_ref, mod_ref, g_ref, o_ref):
    y = _rms(x_ref[0], g_ref[...])
    o_ref[0] = (y * (1.0 + mod_ref[0, 1:2, :]) + mod_ref[0, 0:1, :]).astype(BF16)


def _prenorm(x, mod, g, mod_row):
    bsz, l, d = x.shape
    tm = _tile(l, 512)
    return pl.pallas_call(
        _prenorm_kernel,
        grid=(bsz, l // tm),
        in_specs=[pl.BlockSpec((1, tm, d), lambda b, i: (b, i, 0)),
                  pl.BlockSpec((1, N_MOD, d), lambda b, i: (mod_row(b), 0, 0)),
                  pl.BlockSpec((1, d), lambda b, i: (0, 0))],
        out_specs=pl.BlockSpec((1, tm, d), lambda b, i: (b, i, 0)),
        out_shape=jax.ShapeDtypeStruct((bsz, l, d), BF16),
        compiler_params=_params("arbitrary", "arbitrary"),
        name="prenorm",
    )(x, mod, g.reshape(1, d))


def _swap32(x):
    lane = lax.broadcasted_iota(jnp.int32, x.shape, 1)
    fwd = pltpu.roll(x, LANES - 32, 1)
    bwd = pltpu.roll(x, 32, 1)
    return jnp.where((lane & 32) == 0, fwd, bwd)


def _norm_head(r, gain, cos, sin, scale):
    y = _rms(r, gain)
    if cos is not None:
        y = y * cos + _swap32(y) * sin
    if scale != 1.0:
        y = y * scale
    return y.astype(BF16)


def _kv_kernel(*refs, n_kv, rope):
    if rope:
        h_ref, w_ref, g_ref, cos_ref, sin_ref, k_ref, v_ref = refs
        cos, sin = cos_ref[...], sin_ref[...]
    else:
        h_ref, w_ref, g_ref, k_ref, v_ref = refs
        cos = sin = None
    kv_w = n_kv * LANES
    r = _dot(h_ref[0], w_ref[...])
    for hh in range(n_kv):
        sl = slice(hh * LANES, (hh + 1) * LANES)
        k_ref[0, :, sl] = _norm_head(r[:, sl], g_ref[...], cos, sin, 1.0)
    v_ref[0] = r[:, kv_w:].astype(BF16)


def _project_kv(h, w_in_b, k_norm, kv_w, cos, sin):
    bsz, l, d = h.shape
    tm = _tile(l, 512)
    rope = cos is not None
    in_specs = [pl.BlockSpec((1, tm, d), lambda b, i: (b, i, 0)),
                pl.BlockSpec((d, 2 * kv_w), lambda b, i: (0, 0)),
                pl.BlockSpec((1, LANES), lambda b, i: (0, 0))]
    args = [h, w_in_b, k_norm.reshape(1, LANES)]
    if rope:
        in_specs += [pl.BlockSpec((tm, LANES), lambda b, i: (i, 0))] * 2
        args += [cos, sin]
    out = jax.ShapeDtypeStruct((bsz, l, kv_w), BF16)
    return pl.pallas_call(
        functools.partial(_kv_kernel, n_kv=kv_w // LANES, rope=rope),
        grid=(bsz, l // tm),
        in_specs=in_specs,
        out_specs=[pl.BlockSpec((1, tm, kv_w), lambda b, i: (b, i, 0))] * 2,
        out_shape=[out, out],
        compiler_params=_params("arbitrary", "arbitrary"),
        name="proj_kv",
    )(*args)


def _gelu(x):
    c = 0.7978845608028654
    return x * (0.5 * (1.0 + jnp.tanh(c * (x + 0.044715 * (x * x * x)))))


COL_CHUNK = 512


def _inproj_kernel(x_ref, mod_ref, g_ref, w_ref, kn_ref, qn_ref, cos_ref, sin_ref, ln_ref, ws_ref, bs_ref,
                   h_ref, k_ref, v_ref, q_ref, gm_ref, *, kv_w, q_w, gm_w, scale):
    y = _rms(x_ref[0], g_ref[...])
    h = (y * (1.0 + mod_ref[0, 1:2, :]) + mod_ref[0, 0:1, :]).astype(BF16)
    h_ref[0] = h
    cos, sin = cos_ref[...], sin_ref[...]
    tm = h.shape[0]

    rk = _dot(h, w_ref[:, 0:kv_w])
    for hh in range(kv_w // LANES):
        sl = slice(hh * LANES, (hh + 1) * LANES)
        k_ref[0, :, sl] = _norm_head(rk[:, sl], kn_ref[...], cos, sin, 1.0)
    v_ref[0] = _dot(h, w_ref[:, kv_w:2 * kv_w]).astype(BF16)

    col_q = 2 * kv_w
    cq = min(COL_CHUNK, q_w)
    for j in range(q_w // cq):
        r = _dot(h, w_ref[:, col_q + j * cq:col_q + (j + 1) * cq])
        for hh in range(cq // LANES):
            sl = slice(hh * LANES, (hh + 1) * LANES)
            q_ref[0, :, j * cq + hh * LANES:j * cq + (hh + 1) * LANES] = _norm_head(
                r[:, sl], qn_ref[...], cos, sin, scale)

    col_u, col_v = col_q + q_w, col_q + q_w + gm_w
    cg = min(COL_CHUNK, gm_w)
    for j in range(gm_w // cg):
        gu = _gelu(_dot(h, w_ref[:, col_u + j * cg:col_u + (j + 1) * cg]))
        gv = _gelu(_dot(h, w_ref[:, col_v + j * cg:col_v + (j + 1) * cg]))
        for g in range(cg // LANES):
            cs = slice(g * LANES, (g + 1) * LANES)
            oc = slice(j * cg + g * LANES, j * cg + (g + 1) * LANES)
            v = gv[:, cs]
            vc = v - jnp.mean(v, axis=-1, keepdims=True)
            vn = vc * lax.rsqrt(jnp.mean(vc * vc, axis=-1, keepdims=True) + EPS) * ln_ref[:, oc]
            vn = vn.astype(BF16)
            w = ws_ref[j * (cg // LANES) + g].astype(BF16)
            for c in range(tm // LANES):
                rs = slice(c * LANES, (c + 1) * LANES)
                s = _dot(w, vn[rs, :]) + bs_ref[:, oc]
                gm_ref[0, rs, oc] = (gu[rs, cs] * s).astype(BF16)


def _inproj(x, mod, g_pre, w_a, k_norm, q_norm, cos, sin, gm_ln, w_s, bs_full, kv_w, q_w, gm_w, scale):
    bsz, l, d = x.shape
    tm = _tile(l, 512)
    assert tm % LANES == 0 and w_a.shape[1] == 2 * kv_w + q_w + 2 * gm_w
    tok = lambda b, i: (b, i, 0)
    c2 = lambda b, i: (0, 0)
    return pl.pallas_call(
        functools.partial(_inproj_kernel, kv_w=kv_w, q_w=q_w, gm_w=gm_w, scale=scale),
        grid=(bsz, l // tm),
        in_specs=[pl.BlockSpec((1, tm, d), tok),
                  pl.BlockSpec((1, N_MOD, d), lambda b, i: (b, 0, 0)),
                  pl.BlockSpec((1, d), c2),
                  pl.BlockSpec(w_a.shape, c2),
                  pl.BlockSpec((1, LANES), c2),
                  pl.BlockSpec((1, LANES), c2),
                  pl.BlockSpec((tm, LANES), lambda b, i: (i, 0)),
                  pl.BlockSpec((tm, LANES), lambda b, i: (i, 0)),
                  pl.BlockSpec((1, gm_w), c2),
                  pl.BlockSpec(w_s.shape, lambda b, i: (0, 0, 0)),
                  pl.BlockSpec(bs_full.shape, c2)],
        out_specs=[pl.BlockSpec((1, tm, d), tok),
                   pl.BlockSpec((1, tm, kv_w), tok),
                   pl.BlockSpec((1, tm, kv_w), tok),
                   pl.BlockSpec((1, tm, q_w), tok),
                   pl.BlockSpec((1, tm, gm_w), tok)],
        out_shape=[jax.ShapeDtypeStruct((bsz, l, d), BF16),
                   jax.ShapeDtypeStruct((bsz, l, kv_w), BF16),
                   jax.ShapeDtypeStruct((bsz, l, kv_w), BF16),
                   jax.ShapeDtypeStruct((bsz, l, q_w), BF16),
                   jax.ShapeDtypeStruct((bsz, l, gm_w), BF16)],
        compiler_params=_params("arbitrary", "arbitrary"),
        name="inproj",
    )(x, mod, g_pre.reshape(1, d), w_a, k_norm.reshape(1, LANES), q_norm.reshape(1, LANES), cos, sin,
      gm_ln.reshape(1, gm_w), w_s, bs_full)


def _attn_kernel(q_ref, kc_ref, vc_ref, kx_ref, vx_ref, o_ref, *, n_kv, grp):
    nt = (((1,), (1,)), ((), ()))
    for kv in range(n_kv):
        ks = slice(kv * LANES, (kv + 1) * LANES)
        kc, kx = kc_ref[0, :, ks], kx_ref[0, :, ks]
        vc = jnp.concatenate([vc_ref[0, :, ks], jnp.ones_like(kc)], axis=1)
        vx = jnp.concatenate([vx_ref[0, :, ks], jnp.ones_like(kx)], axis=1)
        for hh in range(grp):
            sl = slice((kv * grp + hh) * LANES, (kv * grp + hh + 1) * LANES)
            q = q_ref[0, :, sl]
            sc = lax.dot_general(q, kc, nt, preferred_element_type=F32)
            sx = lax.dot_general(q, kx, nt, preferred_element_type=F32)
            m = jnp.maximum(jnp.max(sc, axis=-1, keepdims=True), jnp.max(sx, axis=-1, keepdims=True))
            pc = jnp.exp2(sc - m).astype(BF16)
            px = jnp.exp2(sx - m).astype(BF16)
            o = _dot(pc, vc) + _dot(px, vx)
            o_ref[0, :, sl] = (o[:, :LANES] / o[:, LANES:LANES + 1]).astype(BF16)


def _attention(q, kc, vc, kx, vx, n_kv):
    bsz, l, q_w = q.shape
    lc, kv_w = kc.shape[1], kc.shape[2]
    tq = _tile(l, 512)
    tok = lambda b, i: (b, i, 0)
    whole = lambda b, i: (b, 0, 0)
    return pl.pallas_call(
        functools.partial(_attn_kernel, n_kv=n_kv, grp=q_w // kv_w),
        grid=(bsz, l // tq),
        in_specs=[pl.BlockSpec((1, tq, q_w), tok),
                  pl.BlockSpec((1, lc, kv_w), whole),
                  pl.BlockSpec((1, lc, kv_w), whole),
                  pl.BlockSpec((1, l, kv_w), whole),
                  pl.BlockSpec((1, l, kv_w), whole)],
        out_specs=pl.BlockSpec((1, tq, q_w), tok),
        out_shape=jax.ShapeDtypeStruct((bsz, l, q_w), BF16),
        compiler_params=_params("arbitrary", "arbitrary"),
        name="attention",
    )(q, kc, vc, kx, vx)


def _merge_kernel(h_ref, a_ref, g_ref, wga_ref, wgg_ref, wba_ref, wbg_ref, o_ref):
    h = h_ref[0]
    ga = jax.nn.sigmoid(_dot(h, wga_ref[...]))
    gg = jax.nn.sigmoid(_dot(h, wgg_ref[...]))
    pa = _dot(a_ref[0], wba_ref[...])
    pg = _dot(g_ref[0], wbg_ref[...])
    o_ref[0] = (ga * pa + gg * pg).astype(BF16)


def _merge(h, attn, gm, w_g, w_ba_b, w_bg_b, col_ga, col_gg):
    bsz, l, d = h.shape
    q_w, gm_w = attn.shape[2], gm.shape[2]
    tm = _tile(l, 512)
    tn = _tile(d, 512)
    assert col_ga % tn == 0 and col_gg % tn == 0
    ca, cg = col_ga // tn, col_gg // tn
    return pl.pallas_call(
        _merge_kernel,
        grid=(bsz, l // tm, d // tn),
        in_specs=[pl.BlockSpec((1, tm, d), lambda b, i, j: (b, i, 0)),
                  pl.BlockSpec((1, tm, q_w), lambda b, i, j: (b, i, 0)),
                  pl.BlockSpec((1, tm, gm_w), lambda b, i, j: (b, i, 0)),
                  pl.BlockSpec((d, tn), lambda b, i, j: (0, ca + j)),
                  pl.BlockSpec((d, tn), lambda b, i, j: (0, cg + j)),
                  pl.BlockSpec((q_w, tn), lambda b, i, j: (0, j)),
                  pl.BlockSpec((gm_w, tn), lambda b, i, j: (0, j))],
        out_specs=pl.BlockSpec((1, tm, tn), lambda b, i, j: (b, i, j)),
        out_shape=jax.ShapeDtypeStruct((bsz, l, d), BF16),
        compiler_params=_params("arbitrary", "arbitrary", "arbitrary"),
        name="merge",
    )(h, attn, gm, w_g, w_g, w_ba_b, w_bg_b)


R_W0, R_W1 = range(2)
T_E0, T_E1, T_R0_HI, T_R0_LO, T_R1_HI, T_R1_LO = range(6)


def _out_kernel(m_ref, x_ref, mod_ref, gp1_ref, gp2_ref, wo_ref, wr_ref, br_ref,
                x1_ref, h2_ref, ri_ref, rt_ref, cnt_ref, carry_ref, *, n_groups, per_group):
    @pl.when((pl.program_id(0) == 0) & (pl.program_id(1) == 0))
    def _():
        carry_ref[...] = jnp.zeros_like(carry_ref)

    mix = _dot(m_ref[0], wo_ref[...])
    x1 = x_ref[0] + mod_ref[0, 2:3, :] * _rms(mix, gp1_ref[...])
    x1_ref[0] = x1
    h2 = _rms(x1, gp2_ref[...]) * (1.0 + mod_ref[0, 4:5, :]) + mod_ref[0, 3:4, :]
    h2_ref[0] = h2

    logits = _dot3(h2, wr_ref[...]) + br_ref[...]
    tm = logits.shape[0]
    lane = lax.broadcasted_iota(jnp.int32, logits.shape, 1)
    neg = jnp.float32(-jnp.inf)
    lg = jnp.where(lane < n_groups, logits, neg)
    gmax = jnp.max(lg, axis=-1, keepdims=True)
    p_top = 1.0 / jnp.sum(jnp.exp(lg - gmax), axis=-1, keepdims=True)
    gidx = jnp.min(jnp.where(lg == gmax, lane, LANES), axis=-1, keepdims=True)
    lo = n_groups + gidx * per_group
    le = jnp.where((lane >= lo) & (lane < lo + per_group), logits, neg)
    l1 = jnp.max(le, axis=-1, keepdims=True)
    i1 = jnp.min(jnp.where(le == l1, lane, LANES), axis=-1, keepdims=True)
    le2 = jnp.where(lane == i1, neg, le)
    l2 = jnp.max(le2, axis=-1, keepdims=True)
    i2 = jnp.min(jnp.where(le2 == l2, lane, LANES), axis=-1, keepdims=True)
    r = jnp.exp(l2 - l1)
    w0 = p_top / (1.0 + r)
    w1 = p_top * r / (1.0 + r)

    oh1 = lane == i1
    oh2 = lane == i2
    oh = jnp.where(oh1 | oh2, 1.0, 0.0)
    row = lax.broadcasted_iota(jnp.int32, (tm, tm), 0)
    col = lax.broadcasted_iota(jnp.int32, (tm, tm), 1)
    tri = jnp.where(col < row, 1.0, 0.0).astype(BF16)
    base = _dot(tri, oh.astype(BF16)) + carry_ref[...]
    rank0 = jnp.sum(jnp.where(oh1, base, 0.0), axis=-1, keepdims=True)
    rank1 = jnp.sum(jnp.where(oh2, base, 0.0), axis=-1, keepdims=True)
    carry_ref[...] += jnp.sum(oh, axis=0, keepdims=True)
    cnt_ref[...] = carry_ref[...]

    e0 = (i1 - n_groups).astype(F32)
    e1 = (i2 - n_groups).astype(F32)
    rec = jnp.zeros_like(logits)
    for idx, val in ((R_W0, w0), (R_W1, w1)):
        rec = jnp.where(lane == idx, val, rec)
    ri_ref[0] = rec

    r0_hi = jnp.floor(rank0 * (1.0 / 256.0))
    r1_hi = jnp.floor(rank1 * (1.0 / 256.0))
    ints = jnp.zeros_like(logits)
    for idx, val in ((T_E0, e0), (T_E1, e1), (T_R0_HI, r0_hi), (T_R0_LO, rank0 - 256.0 * r0_hi),
                     (T_R1_HI, r1_hi), (T_R1_LO, rank1 - 256.0 * r1_hi)):
        ints = jnp.where(lane == idx, val, ints)
    sel = jnp.where(lax.broadcasted_iota(jnp.int32, (SUBLANES, LANES), 0)
                    == lax.broadcasted_iota(jnp.int32, (SUBLANES, LANES), 1), 1.0, 0.0).astype(BF16)
    rt = lax.dot_general(sel, ints.astype(BF16), (((1,), (1,)), ((), ())), preferred_element_type=F32)
    rt_ref[...] = rt.astype(jnp.int32)


def _out_route(merged, x, mod, g_post1, g_pre2, w_o_b, w_r, b_r, n_groups, per_group):
    bsz, l, d = x.shape
    tm = _tile(l, 512)
    tok = lambda b, i: (b, i, 0)
    const2 = lambda b, i: (0, 0)
    return pl.pallas_call(
        functools.partial(_out_kernel, n_groups=n_groups, per_group=per_group),
        grid=(bsz, l // tm),
        in_specs=[pl.BlockSpec((1, tm, d), tok),
                  pl.BlockSpec((1, tm, d), tok),
                  pl.BlockSpec((1, N_MOD, d), lambda b, i: (b, 0, 0)),
                  pl.BlockSpec((1, d), const2),
                  pl.BlockSpec((1, d), const2),
                  pl.BlockSpec((d, d), const2),
                  pl.BlockSpec((d, LANES), const2),
                  pl.BlockSpec((1, LANES), const2)],
        out_specs=[pl.BlockSpec((1, tm, d), tok),
                   pl.BlockSpec((1, tm, d), tok),
                   pl.BlockSpec((1, tm, LANES), tok),
                   pl.BlockSpec((SUBLANES, tm), lambda b, i: (0, b * (l // tm) + i)),
                   pl.BlockSpec((1, LANES), const2)],
        out_shape=[jax.ShapeDtypeStruct((bsz, l, d), F32),
                   jax.ShapeDtypeStruct((bsz, l, d), F32),
                   jax.ShapeDtypeStruct((bsz, l, LANES), F32),
                   jax.ShapeDtypeStruct((SUBLANES, bsz * l), jnp.int32),
                   jax.ShapeDtypeStruct((1, LANES), F32)],
        scratch_shapes=[pltpu.VMEM((1, LANES), F32)],
        compiler_params=_params("arbitrary", "arbitrary"),
        name="out_route",
    )(merged, x, mod, g_post1.reshape(1, d), g_pre2.reshape(1, d), w_o_b, w_r, b_r)


def _dispatch_kernel(pos_ref, zf_ref, h_ref, xs_ref, zbuf, sem, zsem, *, tm, n_tiles, n_tok):
    i = pl.program_id(0)

    def zero_copy(t):
        return pltpu.make_async_copy(zbuf, xs_ref.at[pl.ds(t * EXPERT_TILE, EXPERT_TILE)], zsem)

    @pl.when(i == 0)
    def _():
        zbuf[...] = jnp.zeros_like(zbuf)

        def issue(t, c):
            @pl.when(zf_ref[t] != 0)
            def _():
                zero_copy(t).start()
            return c

        def drain(t, c):
            @pl.when(zf_ref[t] != 0)
            def _():
                zero_copy(t).wait()
            return c

        lax.fori_loop(0, n_tiles, issue, 0)
        lax.fori_loop(0, n_tiles, drain, 0)

    def issue_rows(rb, c):
        for u in range(DMA_UNROLL):
            r = rb * DMA_UNROLL + u
            for k in range(TOP_K):
                p = pos_ref[k * n_tok + i * tm + r]
                pltpu.make_async_copy(h_ref.at[pl.ds(r, 1)], xs_ref.at[pl.ds(p, 1)], sem).start()
        return c

    lax.fori_loop(0, tm // DMA_UNROLL, issue_rows, 0)
    for k in range(TOP_K):
        pltpu.make_async_copy(h_ref, xs_ref.at[pl.ds(0, tm)], sem).wait()


def _dispatch(h2, pos, zflag, n_rows):
    n, d = h2.shape
    tm = _tile(n, 256)
    n_tiles = n_rows // EXPERT_TILE
    return pl.pallas_call(
        functools.partial(_dispatch_kernel, tm=tm, n_tiles=n_tiles, n_tok=n),
        grid_spec=pltpu.PrefetchScalarGridSpec(
            num_scalar_prefetch=2,
            grid=(n // tm,),
            in_specs=[pl.BlockSpec((tm, d), lambda i, p, z: (i, 0))],
            out_specs=pl.BlockSpec(memory_space=pl.ANY),
            scratch_shapes=[pltpu.VMEM((EXPERT_TILE, d), F32),
                            pltpu.SemaphoreType.DMA(()),
                            pltpu.SemaphoreType.DMA(())]),
        out_shape=jax.ShapeDtypeStruct((n_rows, d), F32),
        compiler_params=_params("arbitrary"),
        name="dispatch",
    )(pos, zflag, h2)


def _cast_rows(src_ref, dst_ref, chunk=256):
    rows = dst_ref.shape[0]
    chunk = min(chunk, rows)

    def body(c, carry):
        r0 = pl.multiple_of(c * chunk, chunk)
        dst_ref[pl.ds(r0, chunk), :] = src_ref[pl.ds(r0, chunk), :].astype(BF16)
        return carry

    lax.fori_loop(0, rows // chunk, body, 0)


def _stage_expert_weights(t, te_ref, nxt_ref, par_ref, w_hbm, wf, wb, sem):
    n_w = len(w_hbm)
    slot = par_ref[t]

    def copy(e, sl, w):
        return pltpu.make_async_copy(w_hbm[w].at[e], wf.at[sl, w], sem.at[sl])

    @pl.when(t == 0)
    def _():
        for w in range(n_w):
            copy(te_ref[0], slot, w).start()

    for w in range(n_w):
        copy(te_ref[t], slot, w).wait()

    @pl.when(nxt_ref[t] >= 0)
    def _():
        for w in range(n_w):
            copy(nxt_ref[t], 1 - slot, w).start()

    for w in range(n_w):
        _cast_rows(wf.at[slot, w], wb.at[w])


def _m1_kernel(te_ref, chg_ref, nxt_ref, par_ref, nt_ref, xs_ref, wg_hbm, wu_hbm, o_ref, wf, wb, sem):
    t = pl.program_id(0)

    @pl.when(t < nt_ref[0])
    def _():
        @pl.when(chg_ref[t] != 0)
        def _():
            _stage_expert_weights(t, te_ref, nxt_ref, par_ref, (wg_hbm, wu_hbm), wf, wb, sem)

        x = xs_ref[...].astype(BF16)
        g = _dot(x, wb[0])
        u = _dot(x, wb[1])
        o_ref[...] = (g * jax.nn.sigmoid(g) * u).astype(BF16)

    @pl.when(t >= nt_ref[0])
    def _():
        o_ref[...] = jnp.zeros_like(o_ref)


def _expert_up(xs, w_gate, w_up, plan):
    n_rows, d = xs.shape
    de = w_gate.shape[2]
    tile = lambda t, *_: (t, 0)
    return pl.pallas_call(
        _m1_kernel,
        grid_spec=pltpu.PrefetchScalarGridSpec(
            num_scalar_prefetch=len(plan),
            grid=(n_rows // EXPERT_TILE,),
            in_specs=[pl.BlockSpec((EXPERT_TILE, d), tile),
                      pl.BlockSpec(memory_space=pl.ANY), pl.BlockSpec(memory_space=pl.ANY)],
            out_specs=pl.BlockSpec((EXPERT_TILE, de), tile),
            scratch_shapes=[pltpu.VMEM((2, 2, d, de), F32), pltpu.VMEM((2, d, de), BF16),
                            pltpu.SemaphoreType.DMA((2,))]),
        out_shape=jax.ShapeDtypeStruct((n_rows, de), BF16),
        compiler_params=_params("arbitrary"),
        name="expert_up",
    )(*plan, xs, w_gate, w_up)


def _m2_kernel(te_ref, chg_ref, nxt_ref, par_ref, nt_ref, h_ref, wd_hbm, o_ref, wf, wb, sem):
    t = pl.program_id(0)

    @pl.when(t < nt_ref[0])
    def _():
        @pl.when(chg_ref[t] != 0)
        def _():
            _stage_expert_weights(t, te_ref, nxt_ref, par_ref, (wd_hbm,), wf, wb, sem)

        o_ref[...] = _dot(h_ref[...], wb[0])

    @pl.when(t >= nt_ref[0])
    def _():
        o_ref[...] = jnp.zeros_like(o_ref)


def _expert_down(hid, w_down, plan):
    n_rows, de = hid.shape
    d = w_down.shape[2]
    tile = lambda t, *_: (t, 0)
    return pl.pallas_call(
        _m2_kernel,
        grid_spec=pltpu.PrefetchScalarGridSpec(
            num_scalar_prefetch=len(plan),
            grid=(n_rows // EXPERT_TILE,),
            in_specs=[pl.BlockSpec((EXPERT_TILE, de), tile), pl.BlockSpec(memory_space=pl.ANY)],
            out_specs=pl.BlockSpec((EXPERT_TILE, d), tile),
            scratch_shapes=[pltpu.VMEM((2, 1, de, d), F32), pltpu.VMEM((1, de, d), BF16),
                            pltpu.SemaphoreType.DMA((2,))]),
        out_shape=jax.ShapeDtypeStruct((n_rows, d), F32),
        compiler_params=_params("arbitrary"),
        name="expert_down",
    )(*plan, hid, w_down)


def _combine_kernel(pos_ref, x1_ref, ri_ref, mod_ref, gp_ref, ys_ref, o_ref, buf, sem, *, tm, n_l, n_tok):
    step = pl.program_id(0) * n_l + pl.program_id(1)
    n_steps = pl.num_programs(0) * n_l
    slot = step % 2

    def issue(s, sl):
        def body(rb, c):
            for u in range(DMA_UNROLL):
                r = rb * DMA_UNROLL + u
                for k in range(TOP_K):
                    p = pos_ref[k * n_tok + s * tm + r]
                    pltpu.make_async_copy(ys_ref.at[pl.ds(p, 1)], buf.at[sl, k, pl.ds(r, 1)], sem.at[sl]).start()
            return c

        lax.fori_loop(0, tm // DMA_UNROLL, body, 0)

    @pl.when(step == 0)
    def _():
        issue(0, 0)

    @pl.when(step + 1 < n_steps)
    def _():
        issue(step + 1, 1 - slot)

    for k in range(TOP_K):
        pltpu.make_async_copy(ys_ref.at[pl.ds(0, tm)], buf.at[slot, k], sem.at[slot]).wait()

    ri = ri_ref[0]
    moe = ri[:, R_W0:R_W0 + 1] * buf[slot, 0] + ri[:, R_W1:R_W1 + 1] * buf[slot, 1]
    o_ref[0] = x1_ref[0] + mod_ref[0, 5:6, :] * _rms(moe, gp_ref[...])


def _combine(x1, rinfo, mod, g_post2, ys, pos):
    bsz, l, d = x1.shape
    tm = _tile(l, 256)
    n_l = l // tm
    tok = lambda b, i, p: (b, i, 0)
    return pl.pallas_call(
        functools.partial(_combine_kernel, tm=tm, n_l=n_l, n_tok=bsz * l),
        grid_spec=pltpu.PrefetchScalarGridSpec(
            num_scalar_prefetch=1,
            grid=(bsz, n_l),
            in_specs=[pl.BlockSpec((1, tm, d), tok),
                      pl.BlockSpec((1, tm, LANES), tok),
                      pl.BlockSpec((1, N_MOD, d), lambda b, i, p: (b, 0, 0)),
                      pl.BlockSpec((1, d), lambda b, i, p: (0, 0)),
                      pl.BlockSpec(memory_space=pl.ANY)],
            out_specs=pl.BlockSpec((1, tm, d), tok),
            scratch_shapes=[pltpu.VMEM((2, TOP_K, tm, d), F32),
                            pltpu.SemaphoreType.DMA((2,))]),
        out_shape=jax.ShapeDtypeStruct((bsz, l, d), F32),
        compiler_params=_params("arbitrary", "arbitrary"),
        name="combine",
    )(pos, x1, rinfo, mod, g_post2.reshape(1, d), ys)


def _rope_tables(seq, head_dim):
    axis_dim = head_dim // 2
    t = jnp.arange(seq, dtype=jnp.int32)
    pos = jnp.stack([t // GRID_W, t % GRID_W], axis=-1).astype(F32)
    inv_freq = ROPE_THETA ** (-jnp.arange(0, axis_dim, 2, dtype=F32) / axis_dim)
    ang = pos[:, :, None] * inv_freq
    cos, sin = jnp.cos(ang), jnp.sin(ang)
    cos_t = jnp.concatenate([cos, cos], axis=-1).reshape(seq, head_dim)
    sin_t = jnp.concatenate([-sin, sin], axis=-1).reshape(seq, head_dim)
    return cos_t, sin_t


def _route_plan(rt, cnt, n_groups, n_experts, n_tiles):
    e = rt[T_E0:T_E1 + 1]
    rank = jnp.stack([rt[T_R0_HI] * 256 + rt[T_R0_LO], rt[T_R1_HI] * 256 + rt[T_R1_LO]])
    counts = cnt[0, n_groups:n_groups + n_experts].astype(jnp.int32)
    tiles_e = (counts + EXPERT_TILE - 1) // EXPERT_TILE
    ids = jnp.arange(n_experts, dtype=jnp.int32)
    tile_end = jnp.sum(jnp.where(ids[None, :] <= ids[:, None], tiles_e[None, :], 0), axis=1)
    tile_start = tile_end - tiles_e
    nt = tile_end[-1]
    row0 = tile_start * EXPERT_TILE
    pos = jnp.sum(jnp.where(e[None] == ids[:, None, None], row0[:, None, None], 0), axis=0) + rank
    t = jnp.arange(n_tiles, dtype=jnp.int32)
    owner = lambda q: jnp.sum((tile_end[None, :] <= q[:, None]).astype(jnp.int32), axis=1)
    te = owner(jnp.minimum(t, nt - 1))
    chg = ((t == 0) | (te != owner(jnp.minimum(jnp.maximum(t - 1, 0), nt - 1)))).astype(jnp.int32)
    partial_last = jnp.any((tile_end[None, :] - 1 == t[:, None]) & (counts[None, :] % EXPERT_TILE != 0), axis=1)
    zflag = ((t >= nt) | partial_last).astype(jnp.int32)
    used = tiles_e > 0
    later = used[None, :] & (ids[None, :] > te[:, None])
    nxt = jnp.min(jnp.where(later, ids[None, :], n_experts), axis=1)
    nxt = jnp.where(nxt == n_experts, -1, nxt)
    par = jnp.sum((used[None, :] & (ids[None, :] < te[:, None])).astype(jnp.int32), axis=1) % 2
    return pos.reshape(-1), (te, chg, nxt, par, nt.reshape(1)), zflag


def kernel(x, c, ctx, c_ctx, w_ada, b_ada, g_pre1, g_post1, g_pre2, g_post2, w_in, q_norm, k_norm,
           gm_ln, w_s, b_s, w_ba, w_bg, w_o, w_rg, b_rg, w_re, b_re, w_gate, w_up, w_down):
    bsz, seq, d = x.shape
    depth = w_ada.shape[0]
    head_dim = q_norm.shape[-1]
    q_w, gm_w = w_ba.shape[1], w_bg.shape[1]
    kv_w = (w_in.shape[2] - q_w - 2 * gm_w - 2 * d) // 2
    n_groups, per_group = w_re.shape[2], w_re.shape[3]
    n_experts = n_groups * per_group
    assert head_dim == LANES and w_s.shape[2] == LANES and gm_w // w_s.shape[1] == LANES
    assert n_groups + n_experts <= LANES and seq % GRID_W == 0
    col_q = 2 * kv_w
    col_u, col_v = col_q + q_w, col_q + q_w + gm_w
    col_ga, col_gg = col_v + gm_w, col_v + gm_w + d
    n_tok = bsz * seq
    n_rows = n_tok * TOP_K + n_experts * EXPERT_TILE
    n_tiles = n_rows // EXPERT_TILE

    cos_t, sin_t = _rope_tables(seq, head_dim)
    pad = (-(bsz + 1)) % (2 * SUBLANES)
    cs = jnp.concatenate([c, c_ctx[None, :], jnp.zeros((pad, d), F32)], axis=0)

    for l in range(depth):
        assert l + 1 == depth, "context-stream update for non-final layers is not implemented"
        mod = _ada(cs, w_ada[l], b_ada[l]).reshape(cs.shape[0], N_MOD, d)
        w_a = w_in[l, :, :col_ga].astype(BF16)
        w_g = w_in[l, :, col_ga:].astype(BF16)
        w_ba_b, w_bg_b, w_o_b = w_ba[l].astype(BF16), w_bg[l].astype(BF16), w_o[l].astype(BF16)
        bs_full = jnp.repeat(b_s[l].T, LANES, axis=1)
        w_r = jnp.concatenate([w_rg[l], w_re[l].reshape(d, n_experts),
                               jnp.zeros((d, LANES - n_groups - n_experts), F32)], axis=1)
        b_r = jnp.concatenate([b_rg[l], b_re[l].reshape(n_experts),
                               jnp.zeros((LANES - n_groups - n_experts,), F32)]).reshape(1, LANES)

        hc = _prenorm(ctx, mod, g_pre1[l], lambda b: bsz)
        kc, vc = _project_kv(hc, w_a, k_norm[l], kv_w, None, None)
        hx, kx, vx, qx, gm = _inproj(x, mod, g_pre1[l], w_a, k_norm[l], q_norm[l], cos_t, sin_t, gm_ln[l],
                                     w_s[l], bs_full, kv_w, q_w, gm_w, head_dim ** -0.5 * LOG2_E)
        attn = _attention(qx, kc, vc, kx, vx, kv_w // head_dim)
        merged = _merge(hx, attn, gm, w_g, w_ba_b, w_bg_b, 0, d)

        x1, h2, rinfo, rt, cnt = _out_route(merged, x, mod, g_post1[l], g_pre2[l], w_o_b, w_r, b_r,
                                            n_groups, per_group)
        pos, plan, zflag = _route_plan(rt, cnt, n_groups, n_experts, n_tiles)

        xs = _dispatch(h2.reshape(n_tok, d), pos, zflag, n_rows)
        hid = _expert_up(xs, w_gate[l], w_up[l], plan)
        ys = _expert_down(hid, w_down[l], plan)
        x = _combine(x1, rinfo, mod, g_post2[l], ys, pos)
    return x
```

```python
import functools

import jax
import jax.numpy as jnp
from jax import lax
from jax.experimental import pallas as pl
from jax.experimental.pallas import tpu as pltpu

GRID_W = 64
ROPE_THETA = 10000.0
EPS = 1e-6
N_MOD = 6
TOP_K = 2
LOG2_E = 1.4426950408889634

LANES = 128
SUBLANES = 8
VMEM_LIMIT_BYTES = 56 * 1024 * 1024

EXPERT_TILE = 256
DMA_UNROLL = SUBLANES

F32 = jnp.float32
BF16 = jnp.bfloat16


def _params(*sem):
    return pltpu.CompilerParams(dimension_semantics=sem, vmem_limit_bytes=VMEM_LIMIT_BYTES)


def _dot(a, b):
    return jnp.dot(a, b, preferred_element_type=F32)


def _split_bf16(a):
    hi = a.astype(BF16)
    lo = (a - hi.astype(F32)).astype(BF16)
    return hi, lo


def _dot3(a, w):
    a_hi, a_lo = _split_bf16(a)
    w_hi, w_lo = _split_bf16(w)
    return _dot(a_hi, w_hi) + _dot(a_lo, w_hi) + _dot(a_hi, w_lo)


def _rms(x, g):
    return x * lax.rsqrt(jnp.mean(x * x, axis=-1, keepdims=True) + EPS) * g


def _store_row_tiled(ref, t0, val):
    rows, d = val.shape
    n_c = d // LANES
    for c in range(n_c):
        ref[pl.ds(t0 * n_c + c, rows, stride=n_c), :] = val[:, c * LANES:(c + 1) * LANES]


def _load_row_tiled(ref, rows, n_c):
    return jnp.concatenate([ref[pl.ds(c, rows, stride=n_c), :] for c in range(n_c)], axis=1)


def _tile(n, pref):
    t = min(n, pref)
    while n % t:
        t //= 2
    return t


def _ada_kernel(c_ref, w_ref, b_ref, o_ref):
    c = c_ref[...]
    a = c * jax.nn.sigmoid(c)
    o_ref[...] = _dot3(a, w_ref[...]) + b_ref[...]


def _ada(cs, w, b):
    m, d = cs.shape
    n = w.shape[1]
    tn = _tile(n, 1024)
    return pl.pallas_call(
        _ada_kernel,
        grid=(n // tn,),
        in_specs=[pl.BlockSpec((m, d), lambda j: (0, 0)),
                  pl.BlockSpec((d, tn), lambda j: (0, j)),
                  pl.BlockSpec((1, tn), lambda j: (0, j))],
        out_specs=pl.BlockSpec((m, tn), lambda j: (0, j)),
        out_shape=jax.ShapeDtypeStruct((m, n), F32),
        compiler_params=_params("arbitrary"),
        name="ada",
    )(cs, w, b.reshape(1, n))


def _prenorm_kernel(x_ref, mod_ref, g_ref, o_ref):
    y = _rms(x_ref[0], g_ref[...])
    o_ref[0] = (y * (1.0 + mod_ref[0, 1:2, :]) + mod_ref[0, 0:1, :]).astype(BF16)


def _prenorm(x, mod, g, mod_row):
    bsz, l, d = x.shape
    tm = _tile(l, 512)
    return pl.pallas_call(
        _prenorm_kernel,
        grid=(bsz, l // tm),
        in_specs=[pl.BlockSpec((1, tm, d), lambda b, i: (b, i, 0)),
                  pl.BlockSpec((1, N_MOD, d), lambda b, i: (mod_row(b), 0, 0)),
                  pl.BlockSpec((1, d), lambda b, i: (0, 0))],
        out_specs=pl.BlockSpec((1, tm, d), lambda b, i: (b, i, 0)),
        out_shape=jax.ShapeDtypeStruct((bsz, l, d), BF16),
        compiler_params=_params("arbitrary", "arbitrary"),
        name="prenorm",
    )(x, mod, g.reshape(1, d))


def _swap32(x):
    lane = lax.broadcasted_iota(jnp.int32, x.shape, 1)
    fwd = pltpu.roll(x, LANES - 32, 1)
    bwd = pltpu.roll(x, 32, 1)
    return jnp.where((lane & 32) == 0, fwd, bwd)


def _norm_head(r, gain, cos, sin, scale):
    y = _rms(r, gain)
    if cos is not None:
        y = y * cos + _swap32(y) * sin
    if scale != 1.0:
        y = y * scale
    return y.astype(BF16)


def _kv_kernel(*refs, n_kv, rope):
    if rope:
        h_ref, w_ref, g_ref, cos_ref, sin_ref, k_ref, v_ref = refs
        cos, sin = cos_ref[...], sin_ref[...]
    else:
        h_ref, w_ref, g_ref, k_ref, v_ref = refs
        cos = sin = None
    kv_w = n_kv * LANES
    r = _dot(h_ref[0], w_ref[...])
    for hh in range(n_kv):
        sl = slice(hh * LANES, (hh + 1) * LANES)
        k_ref[0, :, sl] = _norm_head(r[:, sl], g_ref[...], cos, sin, 1.0)
    v_ref[0] = r[:, kv_w:].astype(BF16)


def _project_kv(h, w_in_b, k_norm, kv_w, cos, sin):
    bsz, l, d = h.shape
    tm = _tile(l, 512)
    rope = cos is not None
    in_specs = [pl.BlockSpec((1, tm, d), lambda b, i: (b, i, 0)),
                pl.BlockSpec((d, 2 * kv_w), lambda b, i: (0, 0)),
                pl.BlockSpec((1, LANES), lambda b, i: (0, 0))]
    args = [h, w_in_b, k_norm.reshape(1, LANES)]
    if rope:
        in_specs += [pl.BlockSpec((tm, LANES), lambda b, i: (i, 0))] * 2
        args += [cos, sin]
    out = jax.ShapeDtypeStruct((bsz, l, kv_w), BF16)
    return pl.pallas_call(
        functools.partial(_kv_kernel, n_kv=kv_w // LANES, rope=rope),
        grid=(bsz, l // tm),
        in_specs=in_specs,
        out_specs=[pl.BlockSpec((1, tm, kv_w), lambda b, i: (b, i, 0))] * 2,
        out_shape=[out, out],
        compiler_params=_params("arbitrary", "arbitrary"),
        name="proj_kv",
    )(*args)


def _gelu(x):
    c = 0.7978845608028654
    return x * (0.5 * (1.0 + jnp.tanh(c * (x + 0.044715 * (x * x * x)))))


COL_CHUNK = 512


def _inproj_kernel(x_ref, mod_ref, g_ref, w_ref, kn_ref, qn_ref, cos_ref, sin_ref, ln_ref, ws_ref, bs_ref,
                   h_ref, k_ref, v_ref, q_ref, gm_ref, *, kv_w, q_w, gm_w, scale):
    y = _rms(x_ref[0], g_ref[...])
    h = (y * (1.0 + mod_ref[0, 1:2, :]) + mod_ref[0, 0:1, :]).astype(BF16)
    h_ref[0] = h
    cos, sin = cos_ref[...], sin_ref[...]
    tm = h.shape[0]

    rk = _dot(h, w_ref[:, 0:kv_w])
    for hh in range(kv_w // LANES):
        sl = slice(hh * LANES, (hh + 1) * LANES)
        k_ref[0, :, sl] = _norm_head(rk[:, sl], kn_ref[...], cos, sin, 1.0)
    v_ref[0] = _dot(h, w_ref[:, kv_w:2 * kv_w]).astype(BF16)

    col_q = 2 * kv_w
    cq = min(COL_CHUNK, q_w)
    for j in range(q_w // cq):
        r = _dot(h, w_ref[:, col_q + j * cq:col_q + (j + 1) * cq])
        for hh in range(cq // LANES):
            sl = slice(hh * LANES, (hh + 1) * LANES)
            q_ref[0, :, j * cq + hh * LANES:j * cq + (hh + 1) * LANES] = _norm_head(
                r[:, sl], qn_ref[...], cos, sin, scale)

    col_u, col_v = col_q + q_w, col_q + q_w + gm_w
    cg = min(COL_CHUNK, gm_w)
    for j in range(gm_w // cg):
        gu = _gelu(_dot(h, w_ref[:, col_u + j * cg:col_u + (j + 1) * cg]))
        gv = _gelu(_dot(h, w_ref[:, col_v + j * cg:col_v + (j + 1) * cg]))
        for g in range(cg // LANES):
            cs = slice(g * LANES, (g + 1) * LANES)
            oc = slice(j * cg + g * LANES, j * cg + (g + 1) * LANES)
            v = gv[:, cs]
            vc = v - jnp.mean(v, axis=-1, keepdims=True)
            vn = vc * lax.rsqrt(jnp.mean(vc * vc, axis=-1, keepdims=True) + EPS) * ln_ref[:, oc]
            vn = vn.astype(BF16)
            w = ws_ref[j * (cg // LANES) + g].astype(BF16)
            for c in range(tm // LANES):
                rs = slice(c * LANES, (c + 1) * LANES)
                s = _dot(w, vn[rs, :]) + bs_ref[:, oc]
                gm_ref[0, rs, oc] = (gu[rs, cs] * s).astype(BF16)


def _inproj(x, mod, g_pre, w_in_b, k_norm, q_norm, cos, sin, gm_ln, w_s, bs_full, kv_w, q_w, gm_w, scale):
    bsz, l, d = x.shape
    tm = _tile(l, 512)
    n_cols = 2 * kv_w + q_w + 2 * gm_w
    assert tm % LANES == 0
    tok = lambda b, i: (b, i, 0)
    c2 = lambda b, i: (0, 0)
    return pl.pallas_call(
        functools.partial(_inproj_kernel, kv_w=kv_w, q_w=q_w, gm_w=gm_w, scale=scale),
        grid=(bsz, l // tm),
        in_specs=[pl.BlockSpec((1, tm, d), tok),
                  pl.BlockSpec((1, N_MOD, d), lambda b, i: (b, 0, 0)),
                  pl.BlockSpec((1, d), c2),
                  pl.BlockSpec((d, n_cols), c2, pipeline_mode=pl.Buffered(1)),
                  pl.BlockSpec((1, LANES), c2),
                  pl.BlockSpec((1, LANES), c2),
                  pl.BlockSpec((tm, LANES), lambda b, i: (i, 0)),
                  pl.BlockSpec((tm, LANES), lambda b, i: (i, 0)),
                  pl.BlockSpec((1, gm_w), c2),
                  pl.BlockSpec(w_s.shape, lambda b, i: (0, 0, 0)),
                  pl.BlockSpec(bs_full.shape, c2)],
        out_specs=[pl.BlockSpec((1, tm, d), tok),
                   pl.BlockSpec((1, tm, kv_w), tok),
                   pl.BlockSpec((1, tm, kv_w), tok),
                   pl.BlockSpec((1, tm, q_w), tok),
                   pl.BlockSpec((1, tm, gm_w), tok)],
        out_shape=[jax.ShapeDtypeStruct((bsz, l, d), BF16),
                   jax.ShapeDtypeStruct((bsz, l, kv_w), BF16),
                   jax.ShapeDtypeStruct((bsz, l, kv_w), BF16),
                   jax.ShapeDtypeStruct((bsz, l, q_w), BF16),
                   jax.ShapeDtypeStruct((bsz, l, gm_w), BF16)],
        compiler_params=_params("arbitrary", "arbitrary"),
        name="inproj",
    )(x, mod, g_pre.reshape(1, d), w_in_b, k_norm.reshape(1, LANES), q_norm.reshape(1, LANES), cos, sin,
      gm_ln.reshape(1, gm_w), w_s, bs_full)


def _attn_kernel(q_ref, kc_ref, vc_ref, kx_ref, vx_ref, o_ref, *, n_kv, grp):
    nt = (((1,), (1,)), ((), ()))
    for kv in range(n_kv):
        ks = slice(kv * LANES, (kv + 1) * LANES)
        kc, kx = kc_ref[0, :, ks], kx_ref[0, :, ks]
        vc = jnp.concatenate([vc_ref[0, :, ks], jnp.ones_like(kc)], axis=1)
        vx = jnp.concatenate([vx_ref[0, :, ks], jnp.ones_like(kx)], axis=1)
        for hh in range(grp):
            sl = slice((kv * grp + hh) * LANES, (kv * grp + hh + 1) * LANES)
            q = q_ref[0, :, sl]
            sc = lax.dot_general(q, kc, nt, preferred_element_type=F32)
            sx = lax.dot_general(q, kx, nt, preferred_element_type=F32)
            m = jnp.maximum(jnp.max(sc, axis=-1, keepdims=True), jnp.max(sx, axis=-1, keepdims=True))
            pc = jnp.exp2(sc - m).astype(BF16)
            px = jnp.exp2(sx - m).astype(BF16)
            o = _dot(pc, vc) + _dot(px, vx)
            o_ref[0, :, sl] = (o[:, :LANES] / o[:, LANES:LANES + 1]).astype(BF16)


def _attention(q, kc, vc, kx, vx, n_kv):
    bsz, l, q_w = q.shape
    lc, kv_w = kc.shape[1], kc.shape[2]
    tq = _tile(l, 512)
    tok = lambda b, i: (b, i, 0)
    whole = lambda b, i: (b, 0, 0)
    return pl.pallas_call(
        functools.partial(_attn_kernel, n_kv=n_kv, grp=q_w // kv_w),
        grid=(bsz, l // tq),
        in_specs=[pl.BlockSpec((1, tq, q_w), tok),
                  pl.BlockSpec((1, lc, kv_w), whole),
                  pl.BlockSpec((1, lc, kv_w), whole),
                  pl.BlockSpec((1, l, kv_w), whole),
                  pl.BlockSpec((1, l, kv_w), whole)],
        out_specs=pl.BlockSpec((1, tq, q_w), tok),
        out_shape=jax.ShapeDtypeStruct((bsz, l, q_w), BF16),
        compiler_params=_params("arbitrary", "arbitrary"),
        name="attention",
    )(q, kc, vc, kx, vx)


def _merge_kernel(h_ref, a_ref, g_ref, wga_ref, wgg_ref, wba_ref, wbg_ref, o_ref):
    h = h_ref[0]
    ga = jax.nn.sigmoid(_dot(h, wga_ref[...]))
    gg = jax.nn.sigmoid(_dot(h, wgg_ref[...]))
    pa = _dot(a_ref[0], wba_ref[...])
    pg = _dot(g_ref[0], wbg_ref[...])
    o_ref[0] = (ga * pa + gg * pg).astype(BF16)


def _merge(h, attn, gm, w_in_b, w_ba_b, w_bg_b, col_ga, col_gg):
    bsz, l, d = h.shape
    q_w, gm_w = attn.shape[2], gm.shape[2]
    tm = _tile(l, 512)
    tn = _tile(d, 512)
    assert col_ga % tn == 0 and col_gg % tn == 0
    ca, cg = col_ga // tn, col_gg // tn
    return pl.pallas_call(
        _merge_kernel,
        grid=(bsz, l // tm, d // tn),
        in_specs=[pl.BlockSpec((1, tm, d), lambda b, i, j: (b, i, 0)),
                  pl.BlockSpec((1, tm, q_w), lambda b, i, j: (b, i, 0)),
                  pl.BlockSpec((1, tm, gm_w), lambda b, i, j: (b, i, 0)),
                  pl.BlockSpec((d, tn), lambda b, i, j: (0, ca + j)),
                  pl.BlockSpec((d, tn), lambda b, i, j: (0, cg + j)),
                  pl.BlockSpec((q_w, tn), lambda b, i, j: (0, j)),
                  pl.BlockSpec((gm_w, tn), lambda b, i, j: (0, j))],
        out_specs=pl.BlockSpec((1, tm, tn), lambda b, i, j: (b, i, j)),
        out_shape=jax.ShapeDtypeStruct((bsz, l, d), BF16),
        compiler_params=_params("arbitrary", "arbitrary", "arbitrary"),
        name="merge",
    )(h, attn, gm, w_in_b, w_in_b, w_ba_b, w_bg_b)


R_W0, R_W1 = range(2)
T_E0, T_E1, T_R0_HI, T_R0_LO, T_R1_HI, T_R1_LO = range(6)


ROUTE_SUB = 512


def _out_kernel(m_ref, x_ref, mod_ref, gp1_ref, gp2_ref, wo_ref, wr_ref, br_ref,
                x1_ref, h2_ref, ri_ref, rt_ref, cnt_ref, carry_ref, *, n_groups, per_group):
    @pl.when((pl.program_id(0) == 0) & (pl.program_id(1) == 0))
    def _():
        carry_ref[...] = jnp.zeros_like(carry_ref)

    tm = m_ref.shape[1]
    sub = min(ROUTE_SUB, tm)
    wr_hi, wr_lo = _split_bf16(wr_ref[...])
    lane = lax.broadcasted_iota(jnp.int32, (sub, LANES), 1)
    row = lax.broadcasted_iota(jnp.int32, (sub, sub), 0)
    col = lax.broadcasted_iota(jnp.int32, (sub, sub), 1)
    tri = jnp.where(col < row, 1.0, 0.0).astype(BF16)
    sel = jnp.where(lax.broadcasted_iota(jnp.int32, (SUBLANES, LANES), 0)
                    == lax.broadcasted_iota(jnp.int32, (SUBLANES, LANES), 1), 1.0, 0.0).astype(BF16)
    neg = jnp.float32(-jnp.inf)
    carry = carry_ref[...]

    for sb in range(tm // sub):
        rs = slice(sb * sub, (sb + 1) * sub)
        mix = _dot(m_ref[0, rs, :], wo_ref[...])
        x1 = x_ref[0, rs, :] + mod_ref[0, 2:3, :] * _rms(mix, gp1_ref[...])
        x1_ref[0, rs, :] = x1
        h2 = _rms(x1, gp2_ref[...]) * (1.0 + mod_ref[0, 4:5, :]) + mod_ref[0, 3:4, :]
        _store_row_tiled(h2_ref, sb * sub, h2)

        h_hi, h_lo = _split_bf16(h2)
        logits = _dot(h_hi, wr_hi) + _dot(h_lo, wr_hi) + _dot(h_hi, wr_lo) + br_ref[...]
        lg = jnp.where(lane < n_groups, logits, neg)
        gmax = jnp.max(lg, axis=-1, keepdims=True)
        p_top = 1.0 / jnp.sum(jnp.exp(lg - gmax), axis=-1, keepdims=True)
        gidx = jnp.min(jnp.where(lg == gmax, lane, LANES), axis=-1, keepdims=True)
        lo = n_groups + gidx * per_group
        le = jnp.where((lane >= lo) & (lane < lo + per_group), logits, neg)
        l1 = jnp.max(le, axis=-1, keepdims=True)
        i1 = jnp.min(jnp.where(le == l1, lane, LANES), axis=-1, keepdims=True)
        le2 = jnp.where(lane == i1, neg, le)
        l2 = jnp.max(le2, axis=-1, keepdims=True)
        i2 = jnp.min(jnp.where(le2 == l2, lane, LANES), axis=-1, keepdims=True)
        r = jnp.exp(l2 - l1)
        w0 = p_top / (1.0 + r)
        w1 = p_top * r / (1.0 + r)

        oh1 = lane == i1
        oh2 = lane == i2
        oh = jnp.where(oh1 | oh2, 1.0, 0.0)
        base = _dot(tri, oh.astype(BF16)) + carry
        rank0 = jnp.sum(jnp.where(oh1, base, 0.0), axis=-1, keepdims=True)
        rank1 = jnp.sum(jnp.where(oh2, base, 0.0), axis=-1, keepdims=True)
        carry = carry + jnp.sum(oh, axis=0, keepdims=True)

        e0 = (i1 - n_groups).astype(F32)
        e1 = (i2 - n_groups).astype(F32)
        rec = jnp.zeros_like(logits)
        for idx, val in ((R_W0, w0), (R_W1, w1)):
            rec = jnp.where(lane == idx, val, rec)
        ri_ref[0, rs, :] = rec

        r0_hi = jnp.floor(rank0 * (1.0 / 256.0))
        r1_hi = jnp.floor(rank1 * (1.0 / 256.0))
        ints = jnp.zeros_like(logits)
        for idx, val in ((T_E0, e0), (T_E1, e1), (T_R0_HI, r0_hi), (T_R0_LO, rank0 - 256.0 * r0_hi),
                         (T_R1_HI, r1_hi), (T_R1_LO, rank1 - 256.0 * r1_hi)):
            ints = jnp.where(lane == idx, val, ints)
        rt = lax.dot_general(sel, ints.astype(BF16), (((1,), (1,)), ((), ())), preferred_element_type=F32)
        rt_ref[:, rs] = rt.astype(jnp.int32)

    carry_ref[...] = carry
    cnt_ref[...] = carry


def _out_route(merged, x, mod, g_post1, g_pre2, w_o_b, w_r, b_r, n_groups, per_group):
    bsz, l, d = x.shape
    tm = _tile(l, 512)
    n_c = d // LANES
    tok = lambda b, i: (b, i, 0)
    const2 = lambda b, i: (0, 0)
    return pl.pallas_call(
        functools.partial(_out_kernel, n_groups=n_groups, per_group=per_group),
        grid=(bsz, l // tm),
        in_specs=[pl.BlockSpec((1, tm, d), tok),
                  pl.BlockSpec((1, tm, d), tok),
                  pl.BlockSpec((1, N_MOD, d), lambda b, i: (b, 0, 0)),
                  pl.BlockSpec((1, d), const2),
                  pl.BlockSpec((1, d), const2),
                  pl.BlockSpec((d, d), const2),
                  pl.BlockSpec((d, LANES), const2),
                  pl.BlockSpec((1, LANES), const2)],
        out_specs=[pl.BlockSpec((1, tm, d), tok),
                   pl.BlockSpec((tm * n_c, LANES), lambda b, i: (b * (l // tm) + i, 0)),
                   pl.BlockSpec((1, tm, LANES), tok),
                   pl.BlockSpec((SUBLANES, tm), lambda b, i: (0, b * (l // tm) + i)),
                   pl.BlockSpec((1, LANES), const2)],
        out_shape=[jax.ShapeDtypeStruct((bsz, l, d), F32),
                   jax.ShapeDtypeStruct((bsz * l * n_c, LANES), F32),
                   jax.ShapeDtypeStruct((bsz, l, LANES), F32),
                   jax.ShapeDtypeStruct((SUBLANES, bsz * l), jnp.int32),
                   jax.ShapeDtypeStruct((1, LANES), F32)],
        scratch_shapes=[pltpu.VMEM((1, LANES), F32)],
        compiler_params=_params("arbitrary", "arbitrary"),
        name="out_route",
    )(merged, x, mod, g_post1.reshape(1, d), g_pre2.reshape(1, d), w_o_b, w_r, b_r)


def _dispatch_kernel(pos_ref, zf_ref, h_ref, xs_ref, zbuf, sem, zsem, *, tm, n_c, n_tiles, n_tok):
    i = pl.program_id(0)
    tile_rows = EXPERT_TILE * n_c

    def zero_copy(t):
        return pltpu.make_async_copy(zbuf, xs_ref.at[pl.ds(pl.multiple_of(t * tile_rows, tile_rows), tile_rows)], zsem)

    @pl.when(i == 0)
    def _():
        zbuf[...] = jnp.zeros_like(zbuf)

        def issue(t, c):
            @pl.when(zf_ref[t] != 0)
            def _():
                zero_copy(t).start()
            return c

        def drain(t, c):
            @pl.when(zf_ref[t] != 0)
            def _():
                zero_copy(t).wait()
            return c

        lax.fori_loop(0, n_tiles, issue, 0)
        lax.fori_loop(0, n_tiles, drain, 0)

    def issue_rows(rb, c):
        for u in range(DMA_UNROLL):
            r = rb * DMA_UNROLL + u
            src = h_ref.at[pl.ds(pl.multiple_of(r * n_c, n_c), n_c)]
            for k in range(TOP_K):
                p = pos_ref[k * n_tok + i * tm + r]
                pltpu.make_async_copy(src, xs_ref.at[pl.ds(pl.multiple_of(p * n_c, n_c), n_c)],
                                      sem).start(priority=k % 2)
        return c

    lax.fori_loop(0, tm // DMA_UNROLL, issue_rows, 0)
    for k in range(TOP_K):
        pltpu.make_async_copy(h_ref, xs_ref.at[pl.ds(0, tm * n_c)], sem).wait()


def _dispatch(h2, pos, zflag, n_rows, d):
    n_c = d // LANES
    n = h2.shape[0] // n_c
    tm = _tile(n, 256)
    n_tiles = n_rows // EXPERT_TILE
    return pl.pallas_call(
        functools.partial(_dispatch_kernel, tm=tm, n_c=n_c, n_tiles=n_tiles, n_tok=n),
        grid_spec=pltpu.PrefetchScalarGridSpec(
            num_scalar_prefetch=2,
            grid=(n // tm,),
            in_specs=[pl.BlockSpec((tm * n_c, LANES), lambda i, p, z: (i, 0))],
            out_specs=pl.BlockSpec(memory_space=pl.ANY),
            scratch_shapes=[pltpu.VMEM((EXPERT_TILE * n_c, LANES), F32),
                            pltpu.SemaphoreType.DMA(()),
                            pltpu.SemaphoreType.DMA(())]),
        out_shape=jax.ShapeDtypeStruct((n_rows * n_c, LANES), F32),
        compiler_params=_params("arbitrary"),
        name="dispatch",
    )(pos, zflag, h2)


def _cast_rows(src_ref, dst_ref, chunk=256):
    rows = dst_ref.shape[0]
    chunk = min(chunk, rows)

    def body(c, carry):
        r0 = pl.multiple_of(c * chunk, chunk)
        dst_ref[pl.ds(r0, chunk), :] = src_ref[pl.ds(r0, chunk), :].astype(BF16)
        return carry

    lax.fori_loop(0, rows // chunk, body, 0)


def _stage_expert_weights(t, te_ref, nxt_ref, par_ref, w_hbm, wf, wb, sem):
    n_w = len(w_hbm)
    slot = par_ref[t]

    def copy(e, sl, w):
        return pltpu.make_async_copy(w_hbm[w].at[e], wf.at[sl, w], sem.at[sl])

    @pl.when(t == 0)
    def _():
        for w in range(n_w):
            copy(te_ref[0], slot, w).start(priority=1)

    for w in range(n_w):
        copy(te_ref[t], slot, w).wait()

    @pl.when(nxt_ref[t] >= 0)
    def _():
        for w in range(n_w):
            copy(nxt_ref[t], 1 - slot, w).start(priority=1)

    for w in range(n_w):
        _cast_rows(wf.at[slot, w], wb.at[w])


def _m1_kernel(te_ref, chg_ref, nxt_ref, par_ref, nt_ref, xs_ref, wg_hbm, wu_hbm, o_ref, wf, wb, sem):
    t = pl.program_id(0)

    @pl.when(t < nt_ref[0])
    def _():
        @pl.when(chg_ref[t] != 0)
        def _():
            _stage_expert_weights(t, te_ref, nxt_ref, par_ref, (wg_hbm, wu_hbm), wf, wb, sem)

        x = _load_row_tiled(xs_ref, EXPERT_TILE, wb.shape[1] // LANES).astype(BF16)
        g = _dot(x, wb[0])
        u = _dot(x, wb[1])
        o_ref[...] = (g * jax.nn.sigmoid(g) * u).astype(BF16)

    @pl.when(t >= nt_ref[0])
    def _():
        o_ref[...] = jnp.zeros_like(o_ref)


def _expert_up(xs, w_gate, w_up, plan):
    d, de = w_gate.shape[1], w_gate.shape[2]
    n_c = d // LANES
    n_rows = xs.shape[0] // n_c
    tile = lambda t, *_: (t, 0)
    return pl.pallas_call(
        _m1_kernel,
        grid_spec=pltpu.PrefetchScalarGridSpec(
            num_scalar_prefetch=len(plan),
            grid=(n_rows // EXPERT_TILE,),
            in_specs=[pl.BlockSpec((EXPERT_TILE * n_c, LANES), tile),
                      pl.BlockSpec(memory_space=pl.ANY), pl.BlockSpec(memory_space=pl.ANY)],
            out_specs=pl.BlockSpec((EXPERT_TILE, de), tile),
            scratch_shapes=[pltpu.VMEM((2, 2, d, de), F32), pltpu.VMEM((2, d, de), BF16),
                            pltpu.SemaphoreType.DMA((2,))]),
        out_shape=jax.ShapeDtypeStruct((n_rows, de), BF16),
        compiler_params=_params("arbitrary"),
        name="expert_up",
    )(*plan, xs, w_gate, w_up)


def _m2_kernel(te_ref, chg_ref, nxt_ref, par_ref, nt_ref, h_ref, wd_hbm, o_ref, wf, wb, sem):
    t = pl.program_id(0)

    @pl.when(t < nt_ref[0])
    def _():
        @pl.when(chg_ref[t] != 0)
        def _():
            _stage_expert_weights(t, te_ref, nxt_ref, par_ref, (wd_hbm,), wf, wb, sem)

        o_ref[...] = _dot(h_ref[...], wb[0])

    @pl.when(t >= nt_ref[0])
    def _():
        o_ref[...] = jnp.zeros_like(o_ref)


def _expert_down(hid, w_down, plan):
    n_rows, de = hid.shape
    d = w_down.shape[2]
    tile = lambda t, *_: (t, 0)
    return pl.pallas_call(
        _m2_kernel,
        grid_spec=pltpu.PrefetchScalarGridSpec(
            num_scalar_prefetch=len(plan),
            grid=(n_rows // EXPERT_TILE,),
            in_specs=[pl.BlockSpec((EXPERT_TILE, de), tile), pl.BlockSpec(memory_space=pl.ANY)],
            out_specs=pl.BlockSpec((EXPERT_TILE, d), tile),
            scratch_shapes=[pltpu.VMEM((2, 1, de, d), F32), pltpu.VMEM((1, de, d), BF16),
                            pltpu.SemaphoreType.DMA((2,))]),
        out_shape=jax.ShapeDtypeStruct((n_rows, d), F32),
        compiler_params=_params("arbitrary"),
        name="expert_down",
    )(*plan, hid, w_down)


def _combine_kernel(pos_ref, x1_ref, ri_ref, mod_ref, gp_ref, ys_ref, o_ref, buf, sem, *, tm, n_l, n_tok):
    step = pl.program_id(0) * n_l + pl.program_id(1)
    n_steps = pl.num_programs(0) * n_l
    slot = step % 2

    def issue(s, sl):
        def body(rb, c):
            for u in range(DMA_UNROLL):
                r = rb * DMA_UNROLL + u
                for k in range(TOP_K):
                    p = pos_ref[k * n_tok + s * tm + r]
                    pltpu.make_async_copy(ys_ref.at[pl.ds(p, 1)], buf.at[sl, k, pl.ds(r, 1)],
                                          sem.at[sl]).start(priority=k % 2)
            return c

        lax.fori_loop(0, tm // DMA_UNROLL, body, 0)

    @pl.when(step == 0)
    def _():
        issue(0, 0)

    @pl.when(step + 1 < n_steps)
    def _():
        issue(step + 1, 1 - slot)

    for k in range(TOP_K):
        pltpu.make_async_copy(ys_ref.at[pl.ds(0, tm)], buf.at[slot, k], sem.at[slot]).wait()

    ri = ri_ref[0]
    moe = ri[:, R_W0:R_W0 + 1] * buf[slot, 0] + ri[:, R_W1:R_W1 + 1] * buf[slot, 1]
    o_ref[0] = x1_ref[0] + mod_ref[0, 5:6, :] * _rms(moe, gp_ref[...])


def _combine(x1, rinfo, mod, g_post2, ys, pos):
    bsz, l, d = x1.shape
    tm = _tile(l, 256)
    n_l = l // tm
    tok = lambda b, i, p: (b, i, 0)
    return pl.pallas_call(
        functools.partial(_combine_kernel, tm=tm, n_l=n_l, n_tok=bsz * l),
        grid_spec=pltpu.PrefetchScalarGridSpec(
            num_scalar_prefetch=1,
            grid=(bsz, n_l),
            in_specs=[pl.BlockSpec((1, tm, d), tok),
                      pl.BlockSpec((1, tm, LANES), tok),
                      pl.BlockSpec((1, N_MOD, d), lambda b, i, p: (b, 0, 0)),
                      pl.BlockSpec((1, d), lambda b, i, p: (0, 0)),
                      pl.BlockSpec(memory_space=pl.ANY)],
            out_specs=pl.BlockSpec((1, tm, d), tok),
            scratch_shapes=[pltpu.VMEM((2, TOP_K, tm, d), F32),
                            pltpu.SemaphoreType.DMA((2,))]),
        out_shape=jax.ShapeDtypeStruct((bsz, l, d), F32),
        compiler_params=_params("arbitrary", "arbitrary"),
        name="combine",
    )(pos, x1, rinfo, mod, g_post2.reshape(1, d), ys)


def _rope_tables(seq, head_dim):
    axis_dim = head_dim // 2
    t = jnp.arange(seq, dtype=jnp.int32)
    pos = jnp.stack([t // GRID_W, t % GRID_W], axis=-1).astype(F32)
    inv_freq = ROPE_THETA ** (-jnp.arange(0, axis_dim, 2, dtype=F32) / axis_dim)
    ang = pos[:, :, None] * inv_freq
    cos, sin = jnp.cos(ang), jnp.sin(ang)
    cos_t = jnp.concatenate([cos, cos], axis=-1).reshape(seq, head_dim)
    sin_t = jnp.concatenate([-sin, sin], axis=-1).reshape(seq, head_dim)
    return cos_t, sin_t


def _route_plan(rt, cnt, n_groups, n_experts, n_tiles):
    e = rt[T_E0:T_E1 + 1]
    rank = jnp.stack([rt[T_R0_HI] * 256 + rt[T_R0_LO], rt[T_R1_HI] * 256 + rt[T_R1_LO]])
    counts = cnt[0, n_groups:n_groups + n_experts].astype(jnp.int32)
    tiles_e = (counts + EXPERT_TILE - 1) // EXPERT_TILE
    ids = jnp.arange(n_experts, dtype=jnp.int32)
    tile_end = jnp.sum(jnp.where(ids[None, :] <= ids[:, None], tiles_e[None, :], 0), axis=1)
    tile_start = tile_end - tiles_e
    nt = tile_end[-1]
    row0 = tile_start * EXPERT_TILE
    pos = jnp.sum(jnp.where(e[None] == ids[:, None, None], row0[:, None, None], 0), axis=0) + rank
    t = jnp.arange(n_tiles, dtype=jnp.int32)
    owner = lambda q: jnp.sum((tile_end[None, :] <= q[:, None]).astype(jnp.int32), axis=1)
    te = owner(jnp.minimum(t, nt - 1))
    chg = ((t == 0) | (te != owner(jnp.minimum(jnp.maximum(t - 1, 0), nt - 1)))).astype(jnp.int32)
    partial_last = jnp.any((tile_end[None, :] - 1 == t[:, None]) & (counts[None, :] % EXPERT_TILE != 0), axis=1)
    zflag = ((t >= nt) | partial_last).astype(jnp.int32)
    used = tiles_e > 0
    later = used[None, :] & (ids[None, :] > te[:, None])
    nxt = jnp.min(jnp.where(later, ids[None, :], n_experts), axis=1)
    nxt = jnp.where(nxt == n_experts, -1, nxt)
    par = jnp.sum((used[None, :] & (ids[None, :] < te[:, None])).astype(jnp.int32), axis=1) % 2
    return pos.reshape(-1), (te, chg, nxt, par, nt.reshape(1)), zflag


def kernel(x, c, ctx, c_ctx, w_ada, b_ada, g_pre1, g_post1, g_pre2, g_post2, w_in, q_norm, k_norm,
           gm_ln, w_s, b_s, w_ba, w_bg, w_o, w_rg, b_rg, w_re, b_re, w_gate, w_up, w_down):
    bsz, seq, d = x.shape
    depth = w_ada.shape[0]
    head_dim = q_norm.shape[-1]
    q_w, gm_w = w_ba.shape[1], w_bg.shape[1]
    kv_w = (w_in.shape[2] - q_w - 2 * gm_w - 2 * d) // 2
    n_groups, per_group = w_re.shape[2], w_re.shape[3]
    n_experts = n_groups * per_group
    assert head_dim == LANES and w_s.shape[2] == LANES and gm_w // w_s.shape[1] == LANES
    assert n_groups + n_experts <= LANES and seq % GRID_W == 0
    col_q = 2 * kv_w
    col_u, col_v = col_q + q_w, col_q + q_w + gm_w
    col_ga, col_gg = col_v + gm_w, col_v + gm_w + d
    n_tok = bsz * seq
    n_rows = n_tok * TOP_K + n_experts * EXPERT_TILE
    n_tiles = n_rows // EXPERT_TILE

    cos_t, sin_t = _rope_tables(seq, head_dim)
    pad = (-(bsz + 1)) % (2 * SUBLANES)
    cs = jnp.concatenate([c, c_ctx[None, :], jnp.zeros((pad, d), F32)], axis=0)

    for l in range(depth):
        assert l + 1 == depth, "context-stream update for non-final layers is not implemented"
        mod = _ada(cs, w_ada[l], b_ada[l]).reshape(cs.shape[0], N_MOD, d)
        w_in_b = w_in[l].astype(BF16)
        w_ba_b, w_bg_b, w_o_b = w_ba[l].astype(BF16), w_bg[l].astype(BF16), w_o[l].astype(BF16)
        bs_full = jnp.repeat(b_s[l].T, LANES, axis=1)
        w_r = jnp.concatenate([w_rg[l], w_re[l].reshape(d, n_experts),
                               jnp.zeros((d, LANES - n_groups - n_experts), F32)], axis=1)
        b_r = jnp.concatenate([b_rg[l], b_re[l].reshape(n_experts),
                               jnp.zeros((LANES - n_groups - n_experts,), F32)]).reshape(1, LANES)

        hc = _prenorm(ctx, mod, g_pre1[l], lambda b: bsz)
        kc, vc = _project_kv(hc, w_in_b, k_norm[l], kv_w, None, None)
        hx, kx, vx, qx, gm = _inproj(x, mod, g_pre1[l], w_in_b, k_norm[l], q_norm[l], cos_t, sin_t, gm_ln[l],
                                     w_s[l], bs_full, kv_w, q_w, gm_w, head_dim ** -0.5 * LOG2_E)
        attn = _attention(qx, kc, vc, kx, vx, kv_w // head_dim)
        merged = _merge(hx, attn, gm, w_in_b, w_ba_b, w_bg_b, col_ga, col_gg)

        x1, h2, rinfo, rt, cnt = _out_route(merged, x, mod, g_post1[l], g_pre2[l], w_o_b, w_r, b_r,
                                            n_groups, per_group)
        pos, plan, zflag = _route_plan(rt, cnt, n_groups, n_experts, n_tiles)

        xs = _dispatch(h2, pos, zflag, n_rows, d)
        hid = _expert_up(xs, w_gate[l], w_up[l], plan)
        ys = _expert_down(hid, w_down[l], plan)
        x = _combine(x1, rinfo, mod, g_post2[l], ys, pos)
    return x
```

```python
import functools

import jax
import jax.numpy as jnp
from jax import lax
from jax.experimental import pallas as pl
from jax.experimental.pallas import tpu as pltpu

GRID_W = 64
ROPE_THETA = 10000.0
EPS = 1e-6
N_MOD = 6
TOP_K = 2
LOG2_E = 1.4426950408889634

LANES = 128
SUBLANES = 8
VMEM_LIMIT_BYTES = 56 * 1024 * 1024

EXPERT_TILE = 256
DMA_UNROLL = SUBLANES

F32 = jnp.float32
BF16 = jnp.bfloat16


def _params(*sem):
    return pltpu.CompilerParams(dimension_semantics=sem, vmem_limit_bytes=VMEM_LIMIT_BYTES)


def _dot(a, b):
    return jnp.dot(a, b, preferred_element_type=F32)


def _split_bf16(a):
    hi = a.astype(BF16)
    lo = (a - hi.astype(F32)).astype(BF16)
    return hi, lo


def _dot3(a, w):
    a_hi, a_lo = _split_bf16(a)
    w_hi, w_lo = _split_bf16(w)
    return _dot(a_hi, w_hi) + _dot(a_lo, w_hi) + _dot(a_hi, w_lo)


def _rms(x, g):
    return x * lax.rsqrt(jnp.mean(x * x, axis=-1, keepdims=True) + EPS) * g


def _store_row_tiled(ref, t0, val):
    rows, d = val.shape
    n_c = d // LANES
    for c in range(n_c):
        ref[pl.ds(t0 * n_c + c, rows, stride=n_c), :] = val[:, c * LANES:(c + 1) * LANES]


def _load_row_tiled(ref, rows, n_c):
    return jnp.concatenate([ref[pl.ds(c, rows, stride=n_c), :] for c in range(n_c)], axis=1)


def _tile(n, pref):
    t = min(n, pref)
    while n % t:
        t //= 2
    return t


def _ada_kernel(c_ref, w_ref, b_ref, o_ref):
    c = c_ref[...]
    a = c * jax.nn.sigmoid(c)
    o_ref[...] = _dot3(a, w_ref[...]) + b_ref[...]


def _ada(cs, w, b):
    m, d = cs.shape
    n = w.shape[1]
    tn = _tile(n, 1024)
    return pl.pallas_call(
        _ada_kernel,
        grid=(n // tn,),
        in_specs=[pl.BlockSpec((m, d), lambda j: (0, 0)),
                  pl.BlockSpec((d, tn), lambda j: (0, j)),
                  pl.BlockSpec((1, tn), lambda j: (0, j))],
        out_specs=pl.BlockSpec((m, tn), lambda j: (0, j)),
        out_shape=jax.ShapeDtypeStruct((m, n), F32),
        compiler_params=_params("arbitrary"),
        name="ada",
    )(cs, w, b.reshape(1, n))


def _prenorm_kernel(x_ref, mod_ref, g_ref, o_ref):
    y = _rms(x_ref[0], g_ref[...])
    o_ref[0] = (y * (1.0 + mod_ref[0, 1:2, :]) + mod_ref[0, 0:1, :]).astype(BF16)


def _prenorm(x, mod, g, mod_row):
    bsz, l, d = x.shape
    tm = _tile(l, 512)
    return pl.pallas_call(
        _prenorm_kernel,
        grid=(bsz, l // tm),
        in_specs=[pl.BlockSpec((1, tm, d), lambda b, i: (b, i, 0)),
                  pl.BlockSpec((1, N_MOD, d), lambda b, i: (mod_row(b), 0, 0)),
                  pl.BlockSpec((1, d), lambda b, i: (0, 0))],
        out_specs=pl.BlockSpec((1, tm, d), lambda b, i: (b, i, 0)),
        out_shape=jax.ShapeDtypeStruct((bsz, l, d), BF16),
        compiler_params=_params("arbitrary", "arbitrary"),
        name="prenorm",
    )(x, mod, g.reshape(1, d))


def _swap32(x):
    lane = lax.broadcasted_iota(jnp.int32, x.shape, 1)
    fwd = pltpu.roll(x, LANES - 32, 1)
    bwd = pltpu.roll(x, 32, 1)
    return jnp.where((lane & 32) == 0, fwd, bwd)


def _norm_head(r, gain, cos, sin, scale):
    y = _rms(r, gain)
    if cos is not None:
        y = y * cos + _swap32(y) * sin
    if scale != 1.0:
        y = y * scale
    return y.astype(BF16)


def _kv_kernel(*refs, n_kv, rope):
    if rope:
        h_ref, w_ref, g_ref, cos_ref, sin_ref, k_ref, v_ref = refs
        cos, sin = cos_ref[...], sin_ref[...]
    else:
        h_ref, w_ref, g_ref, k_ref, v_ref = refs
        cos = sin = None
    kv_w = n_kv * LANES
    r = _dot(h_ref[0], w_ref[...])
    for hh in range(n_kv):
        sl = slice(hh * LANES, (hh + 1) * LANES)
        k_ref[0, :, sl] = _norm_head(r[:, sl], g_ref[...], cos, sin, 1.0)
    v_ref[0] = r[:, kv_w:].astype(BF16)


def _project_kv(h, w_in_b, k_norm, kv_w, cos, sin):
    bsz, l, d = h.shape
    tm = _tile(l, 512)
    rope = cos is not None
    in_specs = [pl.BlockSpec((1, tm, d), lambda b, i: (b, i, 0)),
                pl.BlockSpec((d, 2 * kv_w), lambda b, i: (0, 0)),
                pl.BlockSpec((1, LANES), lambda b, i: (0, 0))]
    args = [h, w_in_b, k_norm.reshape(1, LANES)]
    if rope:
        in_specs += [pl.BlockSpec((tm, LANES), lambda b, i: (i, 0))] * 2
        args += [cos, sin]
    out = jax.ShapeDtypeStruct((bsz, l, kv_w), BF16)
    return pl.pallas_call(
        functools.partial(_kv_kernel, n_kv=kv_w // LANES, rope=rope),
        grid=(bsz, l // tm),
        in_specs=in_specs,
        out_specs=[pl.BlockSpec((1, tm, kv_w), lambda b, i: (b, i, 0))] * 2,
        out_shape=[out, out],
        compiler_params=_params("arbitrary", "arbitrary"),
        name="proj_kv",
    )(*args)


def _gelu(x):
    c = 0.7978845608028654
    return x * (0.5 * (1.0 + jnp.tanh(c * (x + 0.044715 * (x * x * x)))))


COL_CHUNK = 512


def _inproj_kernel(x_ref, mod_ref, g_ref, w_ref, kn_ref, qn_ref, cos_ref, sin_ref, ln_ref, ws_ref, bs_ref,
                   h_ref, k_ref, v_ref, q_ref, gm_ref, *, kv_w, q_w, gm_w, scale):
    y = _rms(x_ref[0], g_ref[...])
    h = (y * (1.0 + mod_ref[0, 1:2, :]) + mod_ref[0, 0:1, :]).astype(BF16)
    h_ref[0] = h
    cos, sin = cos_ref[...], sin_ref[...]
    tm = h.shape[0]

    rk = _dot(h, w_ref[:, 0:kv_w])
    for hh in range(kv_w // LANES):
        sl = slice(hh * LANES, (hh + 1) * LANES)
        k_ref[0, :, sl] = _norm_head(rk[:, sl], kn_ref[...], cos, sin, 1.0)
    v_ref[0] = _dot(h, w_ref[:, kv_w:2 * kv_w]).astype(BF16)

    col_q = 2 * kv_w
    cq = min(COL_CHUNK, q_w)
    for j in range(q_w // cq):
        r = _dot(h, w_ref[:, col_q + j * cq:col_q + (j + 1) * cq])
        for hh in range(cq // LANES):
            sl = slice(hh * LANES, (hh + 1) * LANES)
            q_ref[0, :, j * cq + hh * LANES:j * cq + (hh + 1) * LANES] = _norm_head(
                r[:, sl], qn_ref[...], cos, sin, scale)

    col_u, col_v = col_q + q_w, col_q + q_w + gm_w
    cg = min(COL_CHUNK, gm_w)
    for j in range(gm_w // cg):
        gu = _gelu(_dot(h, w_ref[:, col_u + j * cg:col_u + (j + 1) * cg]))
        gv = _gelu(_dot(h, w_ref[:, col_v + j * cg:col_v + (j + 1) * cg]))
        for g in range(cg // LANES):
            cs = slice(g * LANES, (g + 1) * LANES)
            oc = slice(j * cg + g * LANES, j * cg + (g + 1) * LANES)
            v = gv[:, cs]
            vc = v - jnp.mean(v, axis=-1, keepdims=True)
            vn = vc * lax.rsqrt(jnp.mean(vc * vc, axis=-1, keepdims=True) + EPS) * ln_ref[:, oc]
            vn = vn.astype(BF16)
            w = ws_ref[j * (cg // LANES) + g].astype(BF16)
            for c in range(tm // LANES):
                rs = slice(c * LANES, (c + 1) * LANES)
                s = _dot(w, vn[rs, :]) + bs_ref[:, oc]
                gm_ref[0, rs, oc] = (gu[rs, cs] * s).astype(BF16)


def _inproj(x, mod, g_pre, w_in_b, k_norm, q_norm, cos, sin, gm_ln, w_s, bs_full, kv_w, q_w, gm_w, scale):
    bsz, l, d = x.shape
    tm = _tile(l, 512)
    n_cols = 2 * kv_w + q_w + 2 * gm_w
    assert tm % LANES == 0
    tok = lambda b, i: (b, i, 0)
    c2 = lambda b, i: (0, 0)
    return pl.pallas_call(
        functools.partial(_inproj_kernel, kv_w=kv_w, q_w=q_w, gm_w=gm_w, scale=scale),
        grid=(bsz, l // tm),
        in_specs=[pl.BlockSpec((1, tm, d), tok),
                  pl.BlockSpec((1, N_MOD, d), lambda b, i: (b, 0, 0)),
                  pl.BlockSpec((1, d), c2),
                  pl.BlockSpec((d, n_cols), c2, pipeline_mode=pl.Buffered(1)),
                  pl.BlockSpec((1, LANES), c2),
                  pl.BlockSpec((1, LANES), c2),
                  pl.BlockSpec((tm, LANES), lambda b, i: (i, 0)),
                  pl.BlockSpec((tm, LANES), lambda b, i: (i, 0)),
                  pl.BlockSpec((1, gm_w), c2),
                  pl.BlockSpec(w_s.shape, lambda b, i: (0, 0, 0)),
                  pl.BlockSpec(bs_full.shape, c2)],
        out_specs=[pl.BlockSpec((1, tm, d), tok),
                   pl.BlockSpec((1, tm, kv_w), tok),
                   pl.BlockSpec((1, tm, kv_w), tok),
                   pl.BlockSpec((1, tm, q_w), tok),
                   pl.BlockSpec((1, tm, gm_w), tok)],
        out_shape=[jax.ShapeDtypeStruct((bsz, l, d), BF16),
                   jax.ShapeDtypeStruct((bsz, l, kv_w), BF16),
                   jax.ShapeDtypeStruct((bsz, l, kv_w), BF16),
                   jax.ShapeDtypeStruct((bsz, l, q_w), BF16),
                   jax.ShapeDtypeStruct((bsz, l, gm_w), BF16)],
        compiler_params=_params("arbitrary", "arbitrary"),
        name="inproj",
    )(x, mod, g_pre.reshape(1, d), w_in_b, k_norm.reshape(1, LANES), q_norm.reshape(1, LANES), cos, sin,
      gm_ln.reshape(1, gm_w), w_s, bs_full)


def _attn_kernel(q_ref, kc_ref, vc_ref, kx_ref, vx_ref, o_ref, *, n_kv, grp):
    nt = (((1,), (1,)), ((), ()))
    for kv in range(n_kv):
        ks = slice(kv * LANES, (kv + 1) * LANES)
        kc, kx = kc_ref[0, :, ks], kx_ref[0, :, ks]
        vc = jnp.concatenate([vc_ref[0, :, ks], jnp.ones_like(kc)], axis=1)
        vx = jnp.concatenate([vx_ref[0, :, ks], jnp.ones_like(kx)], axis=1)
        for hh in range(grp):
            sl = slice((kv * grp + hh) * LANES, (kv * grp + hh + 1) * LANES)
            q = q_ref[0, :, sl]
            sc = lax.dot_general(q, kc, nt, preferred_element_type=F32)
            sx = lax.dot_general(q, kx, nt, preferred_element_type=F32)
            m = jnp.maximum(jnp.max(sc, axis=-1, keepdims=True), jnp.max(sx, axis=-1, keepdims=True))
            pc = jnp.exp2(sc - m).astype(BF16)
            px = jnp.exp2(sx - m).astype(BF16)
            o = _dot(pc, vc) + _dot(px, vx)
            o_ref[0, :, sl] = (o[:, :LANES] / o[:, LANES:LANES + 1]).astype(BF16)


def _attention(q, kc, vc, kx, vx, n_kv):
    bsz, l, q_w = q.shape
    lc, kv_w = kc.shape[1], kc.shape[2]
    tq = _tile(l, 512)
    tok = lambda b, i: (b, i, 0)
    whole = lambda b, i: (b, 0, 0)
    return pl.pallas_call(
        functools.partial(_attn_kernel, n_kv=n_kv, grp=q_w // kv_w),
        grid=(bsz, l // tq),
        in_specs=[pl.BlockSpec((1, tq, q_w), tok),
                  pl.BlockSpec((1, lc, kv_w), whole),
                  pl.BlockSpec((1, lc, kv_w), whole),
                  pl.BlockSpec((1, l, kv_w), whole),
                  pl.BlockSpec((1, l, kv_w), whole)],
        out_specs=pl.BlockSpec((1, tq, q_w), tok),
        out_shape=jax.ShapeDtypeStruct((bsz, l, q_w), BF16),
        compiler_params=_params("arbitrary", "arbitrary"),
        name="attention",
    )(q, kc, vc, kx, vx)


def _merge_kernel(h_ref, a_ref, g_ref, *refs, n_chunk):
    wga, wgg = refs[:n_chunk], refs[n_chunk:2 * n_chunk]
    wba_ref, wbg_ref, o_ref = refs[2 * n_chunk:]
    h, a, g = h_ref[0], a_ref[0], g_ref[0]
    tn = wga[0].shape[1]
    for j in range(n_chunk):
        cs = slice(j * tn, (j + 1) * tn)
        ga = jax.nn.sigmoid(_dot(h, wga[j][...]))
        gg = jax.nn.sigmoid(_dot(h, wgg[j][...]))
        o_ref[0, :, cs] = (ga * _dot(a, wba_ref[:, cs]) + gg * _dot(g, wbg_ref[:, cs])).astype(BF16)


def _merge(h, attn, gm, w_in_b, w_ba_b, w_bg_b, col_ga, col_gg):
    bsz, l, d = h.shape
    q_w, gm_w = attn.shape[2], gm.shape[2]
    tm = _tile(l, 512)
    tn = _tile(d, COL_CHUNK)
    assert col_ga % tn == 0 and col_gg % tn == 0
    n_chunk = d // tn
    tok = lambda b, i: (b, i, 0)
    gate_specs = [pl.BlockSpec((d, tn), functools.partial(lambda b, i, c: (0, c), c=(c0 // tn) + j),
                               pipeline_mode=pl.Buffered(1))
                  for c0 in (col_ga, col_gg) for j in range(n_chunk)]
    return pl.pallas_call(
        functools.partial(_merge_kernel, n_chunk=n_chunk),
        grid=(bsz, l // tm),
        in_specs=[pl.BlockSpec((1, tm, d), tok),
                  pl.BlockSpec((1, tm, q_w), tok),
                  pl.BlockSpec((1, tm, gm_w), tok),
                  *gate_specs,
                  pl.BlockSpec((q_w, d), lambda b, i: (0, 0)),
                  pl.BlockSpec((gm_w, d), lambda b, i: (0, 0))],
        out_specs=pl.BlockSpec((1, tm, d), tok),
        out_shape=jax.ShapeDtypeStruct((bsz, l, d), BF16),
        compiler_params=_params("arbitrary", "arbitrary"),
        name="merge",
    )(h, attn, gm, *([w_in_b] * (2 * n_chunk)), w_ba_b, w_bg_b)


R_W0, R_W1 = range(2)
T_E0, T_E1, T_R0_HI, T_R0_LO, T_R1_HI, T_R1_LO = range(6)


ROUTE_SUB = 512


def _out_kernel(m_ref, x_ref, mod_ref, gp1_ref, gp2_ref, wo_ref, wr_ref, br_ref,
                x1_ref, h2_ref, ri_ref, rt_ref, cnt_ref, carry_ref, *, n_groups, per_group):
    @pl.when((pl.program_id(0) == 0) & (pl.program_id(1) == 0))
    def _():
        carry_ref[...] = jnp.zeros_like(carry_ref)

    tm = m_ref.shape[1]
    sub = min(ROUTE_SUB, tm)
    wr_hi, wr_lo = _split_bf16(wr_ref[...])
    lane = lax.broadcasted_iota(jnp.int32, (sub, LANES), 1)
    row = lax.broadcasted_iota(jnp.int32, (sub, sub), 0)
    col = lax.broadcasted_iota(jnp.int32, (sub, sub), 1)
    tri = jnp.where(col < row, 1.0, 0.0).astype(BF16)
    sel = jnp.where(lax.broadcasted_iota(jnp.int32, (SUBLANES, LANES), 0)
                    == lax.broadcasted_iota(jnp.int32, (SUBLANES, LANES), 1), 1.0, 0.0).astype(BF16)
    neg = jnp.float32(-jnp.inf)
    carry = carry_ref[...]

    for sb in range(tm // sub):
        rs = slice(sb * sub, (sb + 1) * sub)
        mix = _dot(m_ref[0, rs, :], wo_ref[...])
        x1 = x_ref[0, rs, :] + mod_ref[0, 2:3, :] * _rms(mix, gp1_ref[...])
        x1_ref[0, rs, :] = x1
        h2 = _rms(x1, gp2_ref[...]) * (1.0 + mod_ref[0, 4:5, :]) + mod_ref[0, 3:4, :]
        _store_row_tiled(h2_ref, sb * sub, h2)

        h_hi, h_lo = _split_bf16(h2)
        logits = _dot(h_hi, wr_hi) + _dot(h_lo, wr_hi) + _dot(h_hi, wr_lo) + br_ref[...]
        lg = jnp.where(lane < n_groups, logits, neg)
        gmax = jnp.max(lg, axis=-1, keepdims=True)
        p_top = 1.0 / jnp.sum(jnp.exp(lg - gmax), axis=-1, keepdims=True)
        gidx = jnp.min(jnp.where(lg == gmax, lane, LANES), axis=-1, keepdims=True)
        lo = n_groups + gidx * per_group
        le = jnp.where((lane >= lo) & (lane < lo + per_group), logits, neg)
        l1 = jnp.max(le, axis=-1, keepdims=True)
        i1 = jnp.min(jnp.where(le == l1, lane, LANES), axis=-1, keepdims=True)
        le2 = jnp.where(lane == i1, neg, le)
        l2 = jnp.max(le2, axis=-1, keepdims=True)
        i2 = jnp.min(jnp.where(le2 == l2, lane, LANES), axis=-1, keepdims=True)
        r = jnp.exp(l2 - l1)
        w0 = p_top / (1.0 + r)
        w1 = p_top * r / (1.0 + r)

        oh1 = lane == i1
        oh2 = lane == i2
        oh = jnp.where(oh1 | oh2, 1.0, 0.0)
        base = _dot(tri, oh.astype(BF16)) + carry
        rank0 = jnp.sum(jnp.where(oh1, base, 0.0), axis=-1, keepdims=True)
        rank1 = jnp.sum(jnp.where(oh2, base, 0.0), axis=-1, keepdims=True)
        carry = carry + jnp.sum(oh, axis=0, keepdims=True)

        e0 = (i1 - n_groups).astype(F32)
        e1 = (i2 - n_groups).astype(F32)
        rec = jnp.zeros_like(logits)
        for idx, val in ((R_W0, w0), (R_W1, w1)):
            rec = jnp.where(lane == idx, val, rec)
        ri_ref[0, rs, :] = rec

        r0_hi = jnp.floor(rank0 * (1.0 / 256.0))
        r1_hi = jnp.floor(rank1 * (1.0 / 256.0))
        ints = jnp.zeros_like(logits)
        for idx, val in ((T_E0, e0), (T_E1, e1), (T_R0_HI, r0_hi), (T_R0_LO, rank0 - 256.0 * r0_hi),
                         (T_R1_HI, r1_hi), (T_R1_LO, rank1 - 256.0 * r1_hi)):
            ints = jnp.where(lane == idx, val, ints)
        rt = lax.dot_general(sel, ints.astype(BF16), (((1,), (1,)), ((), ())), preferred_element_type=F32)
        rt_ref[:, rs] = rt.astype(jnp.int32)

    carry_ref[...] = carry
    cnt_ref[...] = carry


def _out_route(merged, x, mod, g_post1, g_pre2, w_o_b, w_r, b_r, n_groups, per_group):
    bsz, l, d = x.shape
    tm = _tile(l, 512)
    n_c = d // LANES
    tok = lambda b, i: (b, i, 0)
    const2 = lambda b, i: (0, 0)
    return pl.pallas_call(
        functools.partial(_out_kernel, n_groups=n_groups, per_group=per_group),
        grid=(bsz, l // tm),
        in_specs=[pl.BlockSpec((1, tm, d), tok),
                  pl.BlockSpec((1, tm, d), tok),
                  pl.BlockSpec((1, N_MOD, d), lambda b, i: (b, 0, 0)),
                  pl.BlockSpec((1, d), const2),
                  pl.BlockSpec((1, d), const2),
                  pl.BlockSpec((d, d), const2),
                  pl.BlockSpec((d, LANES), const2),
                  pl.BlockSpec((1, LANES), const2)],
        out_specs=[pl.BlockSpec((1, tm, d), tok),
                   pl.BlockSpec((tm * n_c, LANES), lambda b, i: (b * (l // tm) + i, 0)),
                   pl.BlockSpec((1, tm, LANES), tok),
                   pl.BlockSpec((SUBLANES, tm), lambda b, i: (0, b * (l // tm) + i)),
                   pl.BlockSpec((1, LANES), const2)],
        out_shape=[jax.ShapeDtypeStruct((bsz, l, d), F32),
                   jax.ShapeDtypeStruct((bsz * l * n_c, LANES), F32),
                   jax.ShapeDtypeStruct((bsz, l, LANES), F32),
                   jax.ShapeDtypeStruct((SUBLANES, bsz * l), jnp.int32),
                   jax.ShapeDtypeStruct((1, LANES), F32)],
        scratch_shapes=[pltpu.VMEM((1, LANES), F32)],
        compiler_params=_params("arbitrary", "arbitrary"),
        name="out_route",
    )(merged, x, mod, g_post1.reshape(1, d), g_pre2.reshape(1, d), w_o_b, w_r, b_r)


def _dispatch_kernel(pos_ref, zf_ref, h_ref, xs_ref, zbuf, sem, zsem, *, tm, n_c, n_tiles, n_tok):
    i = pl.program_id(0)
    tile_rows = EXPERT_TILE * n_c

    def zero_copy(t):
        return pltpu.make_async_copy(zbuf, xs_ref.at[pl.ds(pl.multiple_of(t * tile_rows, tile_rows), tile_rows)], zsem)

    @pl.when(i == 0)
    def _():
        zbuf[...] = jnp.zeros_like(zbuf)

        def issue(t, c):
            @pl.when(zf_ref[t] != 0)
            def _():
                zero_copy(t).start()
            return c

        def drain(t, c):
            @pl.when(zf_ref[t] != 0)
            def _():
                zero_copy(t).wait()
            return c

        lax.fori_loop(0, n_tiles, issue, 0)
        lax.fori_loop(0, n_tiles, drain, 0)

    def issue_rows(rb, c):
        for u in range(DMA_UNROLL):
            r = rb * DMA_UNROLL + u
            src = h_ref.at[pl.ds(pl.multiple_of(r * n_c, n_c), n_c)]
            for k in range(TOP_K):
                p = pos_ref[k * n_tok + i * tm + r]
                pltpu.make_async_copy(src, xs_ref.at[pl.ds(pl.multiple_of(p * n_c, n_c), n_c)],
                                      sem).start(priority=k % 2)
        return c

    lax.fori_loop(0, tm // DMA_UNROLL, issue_rows, 0)
    for k in range(TOP_K):
        pltpu.make_async_copy(h_ref, xs_ref.at[pl.ds(0, tm * n_c)], sem).wait()


def _dispatch(h2, pos, zflag, n_rows, d):
    n_c = d // LANES
    n = h2.shape[0] // n_c
    tm = _tile(n, 256)
    n_tiles = n_rows // EXPERT_TILE
    return pl.pallas_call(
        functools.partial(_dispatch_kernel, tm=tm, n_c=n_c, n_tiles=n_tiles, n_tok=n),
        grid_spec=pltpu.PrefetchScalarGridSpec(
            num_scalar_prefetch=2,
            grid=(n // tm,),
            in_specs=[pl.BlockSpec((tm * n_c, LANES), lambda i, p, z: (i, 0))],
            out_specs=pl.BlockSpec(memory_space=pl.ANY),
            scratch_shapes=[pltpu.VMEM((EXPERT_TILE * n_c, LANES), F32),
                            pltpu.SemaphoreType.DMA(()),
                            pltpu.SemaphoreType.DMA(())]),
        out_shape=jax.ShapeDtypeStruct((n_rows * n_c, LANES), F32),
        compiler_params=_params("arbitrary"),
        name="dispatch",
    )(pos, zflag, h2)


def _cast_rows(src_ref, dst_ref, chunk=256):
    rows = dst_ref.shape[0]
    chunk = min(chunk, rows)

    def body(c, carry):
        r0 = pl.multiple_of(c * chunk, chunk)
        dst_ref[pl.ds(r0, chunk), :] = src_ref[pl.ds(r0, chunk), :].astype(BF16)
        return carry

    lax.fori_loop(0, rows // chunk, body, 0)


def _experts_kernel(te_ref, chg_ref, nxt_ref, nt_ref, xs_ref, wg_hbm, wu_hbm, wd_hbm, o_ref,
                    wf_g, wf_u, wf_d, wb_g, wb_u, wb_d, sem):
    t = pl.program_id(0)
    stages = ((wg_hbm, wf_g, wb_g), (wu_hbm, wf_u, wb_u), (wd_hbm, wf_d, wb_d))

    def fetch(e):
        return [pltpu.make_async_copy(src.at[e], dst, sem) for src, dst, _ in stages]

    @pl.when(t < nt_ref[0])
    def _():
        @pl.when(chg_ref[t] != 0)
        def _():
            @pl.when(t == 0)
            def _():
                for cp in fetch(te_ref[0]):
                    cp.start(priority=1)

            for cp in fetch(te_ref[t]):
                cp.wait()
            for _, wf, wb in stages:
                _cast_rows(wf, wb)

            @pl.when(nxt_ref[t] >= 0)
            def _():
                for cp in fetch(nxt_ref[t]):
                    cp.start(priority=1)

        x = _load_row_tiled(xs_ref, EXPERT_TILE, wb_g.shape[0] // LANES).astype(BF16)
        g = _dot(x, wb_g[...])
        u = _dot(x, wb_u[...])
        hid = (g * jax.nn.sigmoid(g) * u).astype(BF16)
        o_ref[...] = _dot(hid, wb_d[...])

    @pl.when(t >= nt_ref[0])
    def _():
        o_ref[...] = jnp.zeros_like(o_ref)


def _experts(xs, w_gate, w_up, w_down, plan):
    d, de = w_gate.shape[1], w_gate.shape[2]
    n_c = d // LANES
    n_rows = xs.shape[0] // n_c
    tile = lambda t, *_: (t, 0)
    hbm = pl.BlockSpec(memory_space=pl.ANY)
    return pl.pallas_call(
        _experts_kernel,
        grid_spec=pltpu.PrefetchScalarGridSpec(
            num_scalar_prefetch=len(plan),
            grid=(n_rows // EXPERT_TILE,),
            in_specs=[pl.BlockSpec((EXPERT_TILE * n_c, LANES), tile), hbm, hbm, hbm],
            out_specs=pl.BlockSpec((EXPERT_TILE, d), tile),
            scratch_shapes=[pltpu.VMEM((d, de), F32), pltpu.VMEM((d, de), F32), pltpu.VMEM((de, d), F32),
                            pltpu.VMEM((d, de), BF16), pltpu.VMEM((d, de), BF16), pltpu.VMEM((de, d), BF16),
                            pltpu.SemaphoreType.DMA(())]),
        out_shape=jax.ShapeDtypeStruct((n_rows, d), F32),
        compiler_params=_params("arbitrary"),
        name="experts",
    )(*plan, xs, w_gate, w_up, w_down)


def _combine_kernel(pos_ref, x1_ref, ri_ref, mod_ref, gp_ref, ys_ref, o_ref, buf, sem, *, tm, n_l, n_tok):
    step = pl.program_id(0) * n_l + pl.program_id(1)
    n_steps = pl.num_programs(0) * n_l
    slot = step % 2

    def issue(s, sl):
        def body(rb, c):
            for u in range(DMA_UNROLL):
                r = rb * DMA_UNROLL + u
                for k in range(TOP_K):
                    p = pos_ref[k * n_tok + s * tm + r]
                    pltpu.make_async_copy(ys_ref.at[pl.ds(p, 1)], buf.at[sl, k, pl.ds(r, 1)],
                                          sem.at[sl]).start(priority=k % 2)
            return c

        lax.fori_loop(0, tm // DMA_UNROLL, body, 0)

    @pl.when(step == 0)
    def _():
        issue(0, 0)

    @pl.when(step + 1 < n_steps)
    def _():
        issue(step + 1, 1 - slot)

    for k in range(TOP_K):
        pltpu.make_async_copy(ys_ref.at[pl.ds(0, tm)], buf.at[slot, k], sem.at[slot]).wait()

    ri = ri_ref[0]
    moe = ri[:, R_W0:R_W0 + 1] * buf[slot, 0] + ri[:, R_W1:R_W1 + 1] * buf[slot, 1]
    o_ref[0] = x1_ref[0] + mod_ref[0, 5:6, :] * _rms(moe, gp_ref[...])


def _combine(x1, rinfo, mod, g_post2, ys, pos):
    bsz, l, d = x1.shape
    tm = _tile(l, 256)
    n_l = l // tm
    tok = lambda b, i, p: (b, i, 0)
    return pl.pallas_call(
        functools.partial(_combine_kernel, tm=tm, n_l=n_l, n_tok=bsz * l),
        grid_spec=pltpu.PrefetchScalarGridSpec(
            num_scalar_prefetch=1,
            grid=(bsz, n_l),
            in_specs=[pl.BlockSpec((1, tm, d), tok),
                      pl.BlockSpec((1, tm, LANES), tok),
                      pl.BlockSpec((1, N_MOD, d), lambda b, i, p: (b, 0, 0)),
                      pl.BlockSpec((1, d), lambda b, i, p: (0, 0)),
                      pl.BlockSpec(memory_space=pl.ANY)],
            out_specs=pl.BlockSpec((1, tm, d), tok),
            scratch_shapes=[pltpu.VMEM((2, TOP_K, tm, d), F32),
                            pltpu.SemaphoreType.DMA((2,))]),
        out_shape=jax.ShapeDtypeStruct((bsz, l, d), F32),
        compiler_params=_params("arbitrary", "arbitrary"),
        name="combine",
    )(pos, x1, rinfo, mod, g_post2.reshape(1, d), ys)


def _rope_tables(seq, head_dim):
    axis_dim = head_dim // 2
    t = jnp.arange(seq, dtype=jnp.int32)
    pos = jnp.stack([t // GRID_W, t % GRID_W], axis=-1).astype(F32)
    inv_freq = ROPE_THETA ** (-jnp.arange(0, axis_dim, 2, dtype=F32) / axis_dim)
    ang = pos[:, :, None] * inv_freq
    cos, sin = jnp.cos(ang), jnp.sin(ang)
    cos_t = jnp.concatenate([cos, cos], axis=-1).reshape(seq, head_dim)
    sin_t = jnp.concatenate([-sin, sin], axis=-1).reshape(seq, head_dim)
    return cos_t, sin_t


def _route_plan(rt, cnt, n_groups, n_experts, n_tiles):
    e = rt[T_E0:T_E1 + 1]
    rank = jnp.stack([rt[T_R0_HI] * 256 + rt[T_R0_LO], rt[T_R1_HI] * 256 + rt[T_R1_LO]])
    counts = cnt[0, n_groups:n_groups + n_experts].astype(jnp.int32)
    tiles_e = (counts + EXPERT_TILE - 1) // EXPERT_TILE
    ids = jnp.arange(n_experts, dtype=jnp.int32)
    tile_end = jnp.sum(jnp.where(ids[None, :] <= ids[:, None], tiles_e[None, :], 0), axis=1)
    tile_start = tile_end - tiles_e
    nt = tile_end[-1]
    row0 = tile_start * EXPERT_TILE
    pos = jnp.sum(jnp.where(e[None] == ids[:, None, None], row0[:, None, None], 0), axis=0) + rank
    t = jnp.arange(n_tiles, dtype=jnp.int32)
    owner = lambda q: jnp.sum((tile_end[None, :] <= q[:, None]).astype(jnp.int32), axis=1)
    te = owner(jnp.minimum(t, nt - 1))
    chg = ((t == 0) | (te != owner(jnp.minimum(jnp.maximum(t - 1, 0), nt - 1)))).astype(jnp.int32)
    partial_last = jnp.any((tile_end[None, :] - 1 == t[:, None]) & (counts[None, :] % EXPERT_TILE != 0), axis=1)
    zflag = ((t >= nt) | partial_last).astype(jnp.int32)
    used = tiles_e > 0
    later = used[None, :] & (ids[None, :] > te[:, None])
    nxt = jnp.min(jnp.where(later, ids[None, :], n_experts), axis=1)
    nxt = jnp.where(nxt == n_experts, -1, nxt)
    return pos.reshape(-1), (te, chg, nxt, nt.reshape(1)), zflag


def kernel(x, c, ctx, c_ctx, w_ada, b_ada, g_pre1, g_post1, g_pre2, g_post2, w_in, q_norm, k_norm,
           gm_ln, w_s, b_s, w_ba, w_bg, w_o, w_rg, b_rg, w_re, b_re, w_gate, w_up, w_down):
    bsz, seq, d = x.shape
    depth = w_ada.shape[0]
    head_dim = q_norm.shape[-1]
    q_w, gm_w = w_ba.shape[1], w_bg.shape[1]
    kv_w = (w_in.shape[2] - q_w - 2 * gm_w - 2 * d) // 2
    n_groups, per_group = w_re.shape[2], w_re.shape[3]
    n_experts = n_groups * per_group
    assert head_dim == LANES and w_s.shape[2] == LANES and gm_w // w_s.shape[1] == LANES
    assert n_groups + n_experts <= LANES and seq % GRID_W == 0
    col_q = 2 * kv_w
    col_u, col_v = col_q + q_w, col_q + q_w + gm_w
    col_ga, col_gg = col_v + gm_w, col_v + gm_w + d
    n_tok = bsz * seq
    n_rows = n_tok * TOP_K + n_experts * EXPERT_TILE
    n_tiles = n_rows // EXPERT_TILE

    cos_t, sin_t = _rope_tables(seq, head_dim)
    pad = (-(bsz + 1)) % (2 * SUBLANES)
    cs = jnp.concatenate([c, c_ctx[None, :], jnp.zeros((pad, d), F32)], axis=0)

    for l in range(depth):
        assert l + 1 == depth, "context-stream update for non-final layers is not implemented"
        mod = _ada(cs, w_ada[l], b_ada[l]).reshape(cs.shape[0], N_MOD, d)
        w_in_b = w_in[l].astype(BF16)
        w_ba_b, w_bg_b, w_o_b = w_ba[l].astype(BF16), w_bg[l].astype(BF16), w_o[l].astype(BF16)
        bs_full = jnp.repeat(b_s[l].T, LANES, axis=1)
        w_r = jnp.concatenate([w_rg[l], w_re[l].reshape(d, n_experts),
                               jnp.zeros((d, LANES - n_groups - n_experts), F32)], axis=1)
        b_r = jnp.concatenate([b_rg[l], b_re[l].reshape(n_experts),
                               jnp.zeros((LANES - n_groups - n_experts,), F32)]).reshape(1, LANES)

        hc = _prenorm(ctx, mod, g_pre1[l], lambda b: bsz)
        kc, vc = _project_kv(hc, w_in_b, k_norm[l], kv_w, None, None)
        hx, kx, vx, qx, gm = _inproj(x, mod, g_pre1[l], w_in_b, k_norm[l], q_norm[l], cos_t, sin_t, gm_ln[l],
                                     w_s[l], bs_full, kv_w, q_w, gm_w, head_dim ** -0.5 * LOG2_E)
        attn = _attention(qx, kc, vc, kx, vx, kv_w // head_dim)
        merged = _merge(hx, attn, gm, w_in_b, w_ba_b, w_bg_b, col_ga, col_gg)

        x1, h2, rinfo, rt, cnt = _out_route(merged, x, mod, g_post1[l], g_pre2[l], w_o_b, w_r, b_r,
                                            n_groups, per_group)
        pos, plan, zflag = _route_plan(rt, cnt, n_groups, n_experts, n_tiles)

        xs = _dispatch(h2, pos, zflag, n_rows, d)
        ys = _experts(xs, w_gate[l], w_up[l], w_down[l], plan)
        x = _combine(x1, rinfo, mod, g_post2[l], ys, pos)
    return x
```

```python
import functools

import jax
import jax.numpy as jnp
from jax import lax
from jax.experimental import pallas as pl
from jax.experimental.pallas import tpu as pltpu

GRID_W = 64
ROPE_THETA = 10000.0
EPS = 1e-6
N_MOD = 6
TOP_K = 2
LOG2_E = 1.4426950408889634

LANES = 128
SUBLANES = 8
VMEM_LIMIT_BYTES = 56 * 1024 * 1024

EXPERT_TILE = 256
DMA_UNROLL = SUBLANES

F32 = jnp.float32
BF16 = jnp.bfloat16


def _params(*sem):
    return pltpu.CompilerParams(dimension_semantics=sem, vmem_limit_bytes=VMEM_LIMIT_BYTES)


def _dot(a, b):
    return jnp.dot(a, b, preferred_element_type=F32)


def _split_bf16(a):
    hi = a.astype(BF16)
    lo = (a - hi.astype(F32)).astype(BF16)
    return hi, lo


def _dot3(a, w):
    a_hi, a_lo = _split_bf16(a)
    w_hi, w_lo = _split_bf16(w)
    return _dot(a_hi, w_hi) + _dot(a_lo, w_hi) + _dot(a_hi, w_lo)


def _rms(x, g):
    return x * lax.rsqrt(jnp.mean(x * x, axis=-1, keepdims=True) + EPS) * g


def _store_row_tiled(ref, t0, val):
    rows, d = val.shape
    n_c = d // LANES
    for c in range(n_c):
        ref[pl.ds(t0 * n_c + c, rows, stride=n_c), :] = val[:, c * LANES:(c + 1) * LANES]


def _load_row_tiled(ref, rows, n_c):
    return jnp.concatenate([ref[pl.ds(c, rows, stride=n_c), :] for c in range(n_c)], axis=1)


def _tile(n, pref):
    t = min(n, pref)
    while n % t:
        t //= 2
    return t


def _ada_kernel(c_ref, w_ref, b_ref, o_ref):
    c = c_ref[...]
    a = c * jax.nn.sigmoid(c)
    o_ref[...] = _dot3(a, w_ref[...]) + b_ref[...]


def _ada(cs, w, b):
    m, d = cs.shape
    n = w.shape[1]
    tn = _tile(n, 1024)
    return pl.pallas_call(
        _ada_kernel,
        grid=(n // tn,),
        in_specs=[pl.BlockSpec((m, d), lambda j: (0, 0)),
                  pl.BlockSpec((d, tn), lambda j: (0, j)),
                  pl.BlockSpec((1, tn), lambda j: (0, j))],
        out_specs=pl.BlockSpec((m, tn), lambda j: (0, j)),
        out_shape=jax.ShapeDtypeStruct((m, n), F32),
        compiler_params=_params("arbitrary"),
        name="ada",
    )(cs, w, b.reshape(1, n))


def _prenorm_kernel(x_ref, mod_ref, g_ref, o_ref):
    y = _rms(x_ref[0], g_ref[...])
    o_ref[0] = (y * (1.0 + mod_ref[0, 1:2, :]) + mod_ref[0, 0:1, :]).astype(BF16)


def _prenorm(x, mod, g, mod_row):
    bsz, l, d = x.shape
    tm = _tile(l, 512)
    return pl.pallas_call(
        _prenorm_kernel,
        grid=(bsz, l // tm),
        in_specs=[pl.BlockSpec((1, tm, d), lambda b, i: (b, i, 0)),
                  pl.BlockSpec((1, N_MOD, d), lambda b, i: (mod_row(b), 0, 0)),
                  pl.BlockSpec((1, d), lambda b, i: (0, 0))],
        out_specs=pl.BlockSpec((1, tm, d), lambda b, i: (b, i, 0)),
        out_shape=jax.ShapeDtypeStruct((bsz, l, d), BF16),
        compiler_params=_params("arbitrary", "arbitrary"),
        name="prenorm",
    )(x, mod, g.reshape(1, d))


def _swap32(x):
    lane = lax.broadcasted_iota(jnp.int32, x.shape, 1)
    fwd = pltpu.roll(x, LANES - 32, 1)
    bwd = pltpu.roll(x, 32, 1)
    return jnp.where((lane & 32) == 0, fwd, bwd)


def _norm_head(r, gain, cos, sin, scale):
    y = _rms(r, gain)
    if cos is not None:
        y = y * cos + _swap32(y) * sin
    if scale != 1.0:
        y = y * scale
    return y.astype(BF16)


def _kv_kernel(*refs, n_kv, rope):
    if rope:
        h_ref, w_ref, g_ref, cos_ref, sin_ref, k_ref, v_ref = refs
        cos, sin = cos_ref[...], sin_ref[...]
    else:
        h_ref, w_ref, g_ref, k_ref, v_ref = refs
        cos = sin = None
    kv_w = n_kv * LANES
    r = _dot(h_ref[0], w_ref[...])
    for hh in range(n_kv):
        sl = slice(hh * LANES, (hh + 1) * LANES)
        k_ref[0, :, sl] = _norm_head(r[:, sl], g_ref[...], cos, sin, 1.0)
    v_ref[0] = r[:, kv_w:].astype(BF16)


def _project_kv(h, w_in_b, k_norm, kv_w, cos, sin):
    bsz, l, d = h.shape
    tm = _tile(l, 512)
    rope = cos is not None
    in_specs = [pl.BlockSpec((1, tm, d), lambda b, i: (b, i, 0)),
                pl.BlockSpec((d, 2 * kv_w), lambda b, i: (0, 0)),
                pl.BlockSpec((1, LANES), lambda b, i: (0, 0))]
    args = [h, w_in_b, k_norm.reshape(1, LANES)]
    if rope:
        in_specs += [pl.BlockSpec((tm, LANES), lambda b, i: (i, 0))] * 2
        args += [cos, sin]
    out = jax.ShapeDtypeStruct((bsz, l, kv_w), BF16)
    return pl.pallas_call(
        functools.partial(_kv_kernel, n_kv=kv_w // LANES, rope=rope),
        grid=(bsz, l // tm),
        in_specs=in_specs,
        out_specs=[pl.BlockSpec((1, tm, kv_w), lambda b, i: (b, i, 0))] * 2,
        out_shape=[out, out],
        compiler_params=_params("arbitrary", "arbitrary"),
        name="proj_kv",
    )(*args)


def _gelu(x):
    c = 0.7978845608028654
    return x * (0.5 + 0.5 * jnp.tanh(x * (c + (c * 0.044715) * (x * x))))


COL_CHUNK = 512


def _inproj_kernel(x_ref, mod_ref, g_ref, w_ref, kn_ref, qn_ref, cos_ref, sin_ref, ln_ref, ws_ref, bs_ref,
                   h_ref, k_ref, v_ref, q_ref, gm_ref, *, kv_w, q_w, gm_w, scale):
    x = x_ref[0]
    inv = lax.rsqrt(jnp.mean(x * x, axis=-1, keepdims=True) + EPS)
    h = ((x * inv) * (g_ref[...] * (1.0 + mod_ref[0, 1:2, :])) + mod_ref[0, 0:1, :]).astype(BF16)
    h_ref[0] = h
    cos, sin = cos_ref[...], sin_ref[...]
    tm = h.shape[0]

    rk = _dot(h, w_ref[:, 0:kv_w])
    for hh in range(kv_w // LANES):
        sl = slice(hh * LANES, (hh + 1) * LANES)
        k_ref[0, :, sl] = _norm_head(rk[:, sl], kn_ref[...], cos, sin, 1.0)
    v_ref[0] = _dot(h, w_ref[:, kv_w:2 * kv_w]).astype(BF16)

    col_q = 2 * kv_w
    cq = min(COL_CHUNK, q_w)
    for j in range(q_w // cq):
        r = _dot(h, w_ref[:, col_q + j * cq:col_q + (j + 1) * cq])
        for hh in range(cq // LANES):
            sl = slice(hh * LANES, (hh + 1) * LANES)
            q_ref[0, :, j * cq + hh * LANES:j * cq + (hh + 1) * LANES] = _norm_head(
                r[:, sl], qn_ref[...], cos, sin, scale)

    col_u, col_v = col_q + q_w, col_q + q_w + gm_w
    cg = min(COL_CHUNK, gm_w)
    for j in range(gm_w // cg):
        gu = _gelu(_dot(h, w_ref[:, col_u + j * cg:col_u + (j + 1) * cg]))
        gv = _gelu(_dot(h, w_ref[:, col_v + j * cg:col_v + (j + 1) * cg]))
        for g in range(cg // LANES):
            cs = slice(g * LANES, (g + 1) * LANES)
            oc = slice(j * cg + g * LANES, j * cg + (g + 1) * LANES)
            v = gv[:, cs]
            vc = v - jnp.mean(v, axis=-1, keepdims=True)
            vn = vc * lax.rsqrt(jnp.mean(vc * vc, axis=-1, keepdims=True) + EPS) * ln_ref[:, oc]
            vn = vn.astype(BF16)
            w = ws_ref[j * (cg // LANES) + g].astype(BF16)
            for c in range(tm // LANES):
                rs = slice(c * LANES, (c + 1) * LANES)
                s = _dot(w, vn[rs, :]) + bs_ref[:, oc]
                gm_ref[0, rs, oc] = (gu[rs, cs] * s).astype(BF16)


def _inproj(x, mod, g_pre, w_in_b, k_norm, q_norm, cos, sin, gm_ln, w_s, bs_full, kv_w, q_w, gm_w, scale):
    bsz, l, d = x.shape
    tm = _tile(l, 512)
    n_cols = 2 * kv_w + q_w + 2 * gm_w
    assert tm % LANES == 0
    tok = lambda b, i: (b, i, 0)
    c2 = lambda b, i: (0, 0)
    return pl.pallas_call(
        functools.partial(_inproj_kernel, kv_w=kv_w, q_w=q_w, gm_w=gm_w, scale=scale),
        grid=(bsz, l // tm),
        in_specs=[pl.BlockSpec((1, tm, d), tok),
                  pl.BlockSpec((1, N_MOD, d), lambda b, i: (b, 0, 0)),
                  pl.BlockSpec((1, d), c2),
                  pl.BlockSpec((d, n_cols), c2, pipeline_mode=pl.Buffered(1)),
                  pl.BlockSpec((1, LANES), c2),
                  pl.BlockSpec((1, LANES), c2),
                  pl.BlockSpec((tm, LANES), lambda b, i: (i, 0)),
                  pl.BlockSpec((tm, LANES), lambda b, i: (i, 0)),
                  pl.BlockSpec((1, gm_w), c2),
                  pl.BlockSpec(w_s.shape, lambda b, i: (0, 0, 0)),
                  pl.BlockSpec(bs_full.shape, c2)],
        out_specs=[pl.BlockSpec((1, tm, d), tok),
                   pl.BlockSpec((1, tm, kv_w), tok),
                   pl.BlockSpec((1, tm, kv_w), tok),
                   pl.BlockSpec((1, tm, q_w), tok),
                   pl.BlockSpec((1, tm, gm_w), tok)],
        out_shape=[jax.ShapeDtypeStruct((bsz, l, d), BF16),
                   jax.ShapeDtypeStruct((bsz, l, kv_w), BF16),
                   jax.ShapeDtypeStruct((bsz, l, kv_w), BF16),
                   jax.ShapeDtypeStruct((bsz, l, q_w), BF16),
                   jax.ShapeDtypeStruct((bsz, l, gm_w), BF16)],
        compiler_params=_params("arbitrary", "arbitrary"),
        name="inproj",
    )(x, mod, g_pre.reshape(1, d), w_in_b, k_norm.reshape(1, LANES), q_norm.reshape(1, LANES), cos, sin,
      gm_ln.reshape(1, gm_w), w_s, bs_full)


def _attn_kernel(q_ref, kc_ref, vc_ref, kx_ref, vx_ref, o_ref, *, n_kv, grp):
    nt = (((1,), (1,)), ((), ()))
    for kv in range(n_kv):
        ks = slice(kv * LANES, (kv + 1) * LANES)
        kc, kx = kc_ref[0, :, ks], kx_ref[0, :, ks]
        vc = jnp.concatenate([vc_ref[0, :, ks], jnp.ones_like(kc)], axis=1)
        vx = jnp.concatenate([vx_ref[0, :, ks], jnp.ones_like(kx)], axis=1)
        for hh in range(grp):
            sl = slice((kv * grp + hh) * LANES, (kv * grp + hh + 1) * LANES)
            q = q_ref[0, :, sl]
            sc = lax.dot_general(q, kc, nt, preferred_element_type=F32)
            sx = lax.dot_general(q, kx, nt, preferred_element_type=F32)
            m = jnp.maximum(jnp.max(sc, axis=-1, keepdims=True), jnp.max(sx, axis=-1, keepdims=True))
            pc = jnp.exp2(sc - m).astype(BF16)
            px = jnp.exp2(sx - m).astype(BF16)
            o = _dot(pc, vc) + _dot(px, vx)
            o_ref[0, :, sl] = (o[:, :LANES] / o[:, LANES:LANES + 1]).astype(BF16)


def _attention(q, kc, vc, kx, vx, n_kv):
    bsz, l, q_w = q.shape
    lc, kv_w = kc.shape[1], kc.shape[2]
    tq = _tile(l, 512)
    tok = lambda b, i: (b, i, 0)
    whole = lambda b, i: (b, 0, 0)
    return pl.pallas_call(
        functools.partial(_attn_kernel, n_kv=n_kv, grp=q_w // kv_w),
        grid=(bsz, l // tq),
        in_specs=[pl.BlockSpec((1, tq, q_w), tok),
                  pl.BlockSpec((1, lc, kv_w), whole),
                  pl.BlockSpec((1, lc, kv_w), whole),
                  pl.BlockSpec((1, l, kv_w), whole),
                  pl.BlockSpec((1, l, kv_w), whole)],
        out_specs=pl.BlockSpec((1, tq, q_w), tok),
        out_shape=jax.ShapeDtypeStruct((bsz, l, q_w), BF16),
        compiler_params=_params("arbitrary", "arbitrary"),
        name="attention",
    )(q, kc, vc, kx, vx)


def _merge_kernel(h_ref, a_ref, g_ref, *refs, n_chunk):
    wga, wgg = refs[:n_chunk], refs[n_chunk:2 * n_chunk]
    wba_ref, wbg_ref, o_ref = refs[2 * n_chunk:]
    h, a, g = h_ref[0], a_ref[0], g_ref[0]
    tn = wga[0].shape[1]
    for j in range(n_chunk):
        cs = slice(j * tn, (j + 1) * tn)
        ga = jax.nn.sigmoid(_dot(h, wga[j][...]))
        gg = jax.nn.sigmoid(_dot(h, wgg[j][...]))
        o_ref[0, :, cs] = (ga * _dot(a, wba_ref[:, cs]) + gg * _dot(g, wbg_ref[:, cs])).astype(BF16)


def _merge(h, attn, gm, w_in_b, w_ba_b, w_bg_b, col_ga, col_gg):
    bsz, l, d = h.shape
    q_w, gm_w = attn.shape[2], gm.shape[2]
    tm = _tile(l, 512)
    tn = _tile(d, COL_CHUNK)
    assert col_ga % tn == 0 and col_gg % tn == 0
    n_chunk = d // tn
    tok = lambda b, i: (b, i, 0)
    gate_specs = [pl.BlockSpec((d, tn), functools.partial(lambda b, i, c: (0, c), c=(c0 // tn) + j),
                               pipeline_mode=pl.Buffered(1))
                  for c0 in (col_ga, col_gg) for j in range(n_chunk)]
    return pl.pallas_call(
        functools.partial(_merge_kernel, n_chunk=n_chunk),
        grid=(bsz, l // tm),
        in_specs=[pl.BlockSpec((1, tm, d), tok),
                  pl.BlockSpec((1, tm, q_w), tok),
                  pl.BlockSpec((1, tm, gm_w), tok),
                  *gate_specs,
                  pl.BlockSpec((q_w, d), lambda b, i: (0, 0)),
                  pl.BlockSpec((gm_w, d), lambda b, i: (0, 0))],
        out_specs=pl.BlockSpec((1, tm, d), tok),
        out_shape=jax.ShapeDtypeStruct((bsz, l, d), BF16),
        compiler_params=_params("arbitrary", "arbitrary"),
        name="merge",
    )(h, attn, gm, *([w_in_b] * (2 * n_chunk)), w_ba_b, w_bg_b)


R_W0, R_W1 = range(2)
T_E0, T_E1, T_R0_HI, T_R0_LO, T_R1_HI, T_R1_LO = range(6)


ROUTE_SUB = 512


def _out_kernel(m_ref, x_ref, mod_ref, gp1_ref, gp2_ref, wo_ref, wr_ref, br_ref,
                x1_ref, h2_ref, ri_ref, rt_ref, cnt_ref, carry_ref, *, n_groups, per_group):
    @pl.when((pl.program_id(0) == 0) & (pl.program_id(1) == 0))
    def _():
        carry_ref[...] = jnp.zeros_like(carry_ref)

    tm = m_ref.shape[1]
    sub = min(ROUTE_SUB, tm)
    wr_hi, wr_lo = _split_bf16(wr_ref[...])
    wr_both = jnp.concatenate([wr_hi, wr_lo], axis=1)
    lane = lax.broadcasted_iota(jnp.int32, (sub, LANES), 1)
    row = lax.broadcasted_iota(jnp.int32, (sub, sub), 0)
    col = lax.broadcasted_iota(jnp.int32, (sub, sub), 1)
    tri = jnp.where(col < row, 1.0, 0.0).astype(BF16)
    sel = jnp.where(lax.broadcasted_iota(jnp.int32, (SUBLANES, LANES), 0)
                    == lax.broadcasted_iota(jnp.int32, (SUBLANES, LANES), 1), 1.0, 0.0).astype(BF16)
    neg = jnp.float32(-jnp.inf)
    carry = carry_ref[...]

    for sb in range(tm // sub):
        rs = slice(sb * sub, (sb + 1) * sub)
        mix = _dot(m_ref[0, rs, :], wo_ref[...])
        inv1 = lax.rsqrt(jnp.mean(mix * mix, axis=-1, keepdims=True) + EPS)
        x1 = x_ref[0, rs, :] + (mix * inv1) * (mod_ref[0, 2:3, :] * gp1_ref[...])
        x1_ref[0, rs, :] = x1
        inv2 = lax.rsqrt(jnp.mean(x1 * x1, axis=-1, keepdims=True) + EPS)
        h2 = (x1 * inv2) * (gp2_ref[...] * (1.0 + mod_ref[0, 4:5, :])) + mod_ref[0, 3:4, :]
        _store_row_tiled(h2_ref, sb * sub, h2)

        h_hi, h_lo = _split_bf16(h2)
        both = _dot(h_hi, wr_both)
        logits = both[:, :LANES] + both[:, LANES:] + _dot(h_lo, wr_hi) + br_ref[...]
        lg = jnp.where(lane < n_groups, logits, neg)
        gmax = jnp.max(lg, axis=-1, keepdims=True)
        p_top = 1.0 / jnp.sum(jnp.exp(lg - gmax), axis=-1, keepdims=True)
        gidx = jnp.min(jnp.where(lg == gmax, lane, LANES), axis=-1, keepdims=True)
        lo = n_groups + gidx * per_group
        le = jnp.where((lane >= lo) & (lane < lo + per_group), logits, neg)
        l1 = jnp.max(le, axis=-1, keepdims=True)
        i1 = jnp.min(jnp.where(le == l1, lane, LANES), axis=-1, keepdims=True)
        le2 = jnp.where(lane == i1, neg, le)
        l2 = jnp.max(le2, axis=-1, keepdims=True)
        i2 = jnp.min(jnp.where(le2 == l2, lane, LANES), axis=-1, keepdims=True)
        r = jnp.exp(l2 - l1)
        w0 = p_top / (1.0 + r)
        w1 = p_top * r / (1.0 + r)

        oh1 = lane == i1
        oh2 = lane == i2
        oh = jnp.where(oh1 | oh2, 1.0, 0.0)
        base = _dot(tri, oh.astype(BF16)) + carry
        rank0 = jnp.sum(jnp.where(oh1, base, 0.0), axis=-1, keepdims=True)
        rank1 = jnp.sum(jnp.where(oh2, base, 0.0), axis=-1, keepdims=True)
        carry = carry + jnp.sum(oh, axis=0, keepdims=True)

        e0 = (i1 - n_groups).astype(F32)
        e1 = (i2 - n_groups).astype(F32)
        rec = jnp.zeros_like(logits)
        for idx, val in ((R_W0, w0), (R_W1, w1)):
            rec = jnp.where(lane == idx, val, rec)
        ri_ref[0, rs, :] = rec

        r0_hi = jnp.floor(rank0 * (1.0 / 256.0))
        r1_hi = jnp.floor(rank1 * (1.0 / 256.0))
        ints = jnp.zeros_like(logits)
        for idx, val in ((T_E0, e0), (T_E1, e1), (T_R0_HI, r0_hi), (T_R0_LO, rank0 - 256.0 * r0_hi),
                         (T_R1_HI, r1_hi), (T_R1_LO, rank1 - 256.0 * r1_hi)):
            ints = jnp.where(lane == idx, val, ints)
        rt = lax.dot_general(sel, ints.astype(BF16), (((1,), (1,)), ((), ())), preferred_element_type=F32)
        rt_ref[:, rs] = rt.astype(jnp.int32)

    carry_ref[...] = carry
    cnt_ref[...] = carry


def _out_route(merged, x, mod, g_post1, g_pre2, w_o_b, w_r, b_r, n_groups, per_group):
    bsz, l, d = x.shape
    tm = _tile(l, 512)
    n_c = d // LANES
    tok = lambda b, i: (b, i, 0)
    const2 = lambda b, i: (0, 0)
    return pl.pallas_call(
        functools.partial(_out_kernel, n_groups=n_groups, per_group=per_group),
        grid=(bsz, l // tm),
        in_specs=[pl.BlockSpec((1, tm, d), tok),
                  pl.BlockSpec((1, tm, d), tok),
                  pl.BlockSpec((1, N_MOD, d), lambda b, i: (b, 0, 0)),
                  pl.BlockSpec((1, d), const2),
                  pl.BlockSpec((1, d), const2),
                  pl.BlockSpec((d, d), const2),
                  pl.BlockSpec((d, LANES), const2),
                  pl.BlockSpec((1, LANES), const2)],
        out_specs=[pl.BlockSpec((1, tm, d), tok),
                   pl.BlockSpec((tm * n_c, LANES), lambda b, i: (b * (l // tm) + i, 0)),
                   pl.BlockSpec((1, tm, LANES), tok),
                   pl.BlockSpec((SUBLANES, tm), lambda b, i: (0, b * (l // tm) + i)),
                   pl.BlockSpec((1, LANES), const2)],
        out_shape=[jax.ShapeDtypeStruct((bsz, l, d), F32),
                   jax.ShapeDtypeStruct((bsz * l * n_c, LANES), F32),
                   jax.ShapeDtypeStruct((bsz, l, LANES), F32),
                   jax.ShapeDtypeStruct((SUBLANES, bsz * l), jnp.int32),
                   jax.ShapeDtypeStruct((1, LANES), F32)],
        scratch_shapes=[pltpu.VMEM((1, LANES), F32)],
        compiler_params=_params("arbitrary", "arbitrary"),
        name="out_route",
    )(merged, x, mod, g_post1.reshape(1, d), g_pre2.reshape(1, d), w_o_b, w_r, b_r)


def _dispatch_kernel(pos_ref, zf_ref, h_ref, xs_ref, zbuf, sem, zsem, *, tm, n_c, n_tiles, n_tok):
    i = pl.program_id(0)
    tile_rows = EXPERT_TILE * n_c

    def zero_copy(t):
        return pltpu.make_async_copy(zbuf, xs_ref.at[pl.ds(pl.multiple_of(t * tile_rows, tile_rows), tile_rows)], zsem)

    @pl.when(i == 0)
    def _():
        zbuf[...] = jnp.zeros_like(zbuf)

        def issue(t, c):
            @pl.when(zf_ref[t] != 0)
            def _():
                zero_copy(t).start()
            return c

        def drain(t, c):
            @pl.when(zf_ref[t] != 0)
            def _():
                zero_copy(t).wait()
            return c

        lax.fori_loop(0, n_tiles, issue, 0)
        lax.fori_loop(0, n_tiles, drain, 0)

    def issue_rows(rb, c):
        for u in range(DMA_UNROLL):
            r = rb * DMA_UNROLL + u
            src = h_ref.at[pl.ds(pl.multiple_of(r * n_c, n_c), n_c)]
            for k in range(TOP_K):
                p = pos_ref[k * n_tok + i * tm + r]
                pltpu.make_async_copy(src, xs_ref.at[pl.ds(pl.multiple_of(p * n_c, n_c), n_c)],
                                      sem).start(priority=k % 2)
        return c

    lax.fori_loop(0, tm // DMA_UNROLL, issue_rows, 0)
    for k in range(TOP_K):
        pltpu.make_async_copy(h_ref, xs_ref.at[pl.ds(0, tm * n_c)], sem).wait()


def _dispatch(h2, pos, zflag, n_rows, d):
    n_c = d // LANES
    n = h2.shape[0] // n_c
    tm = _tile(n, 256)
    n_tiles = n_rows // EXPERT_TILE
    return pl.pallas_call(
        functools.partial(_dispatch_kernel, tm=tm, n_c=n_c, n_tiles=n_tiles, n_tok=n),
        grid_spec=pltpu.PrefetchScalarGridSpec(
            num_scalar_prefetch=2,
            grid=(n // tm,),
            in_specs=[pl.BlockSpec((tm * n_c, LANES), lambda i, p, z: (i, 0))],
            out_specs=pl.BlockSpec(memory_space=pl.ANY),
            scratch_shapes=[pltpu.VMEM((EXPERT_TILE * n_c, LANES), F32),
                            pltpu.SemaphoreType.DMA(()),
                            pltpu.SemaphoreType.DMA(())]),
        out_shape=jax.ShapeDtypeStruct((n_rows * n_c, LANES), F32),
        compiler_params=_params("arbitrary"),
        name="dispatch",
    )(pos, zflag, h2)


def _cast_rows(src_ref, dst_ref, chunk=256):
    rows = dst_ref.shape[0]
    chunk = min(chunk, rows)

    def body(c, carry):
        r0 = pl.multiple_of(c * chunk, chunk)
        dst_ref[pl.ds(r0, chunk), :] = src_ref[pl.ds(r0, chunk), :].astype(BF16)
        return carry

    lax.fori_loop(0, rows // chunk, body, 0)


def _experts_kernel(te_ref, chg_ref, nxt_ref, nt_ref, xs_ref, wg_hbm, wu_hbm, wd_hbm, o_ref,
                    wf_g, wf_u, wf_d, wb_g, wb_u, wb_d, sem):
    t = pl.program_id(0)
    stages = ((wg_hbm, wf_g, wb_g), (wu_hbm, wf_u, wb_u), (wd_hbm, wf_d, wb_d))

    def fetch(e):
        return [pltpu.make_async_copy(src.at[e], dst, sem) for src, dst, _ in stages]

    @pl.when(t < nt_ref[0])
    def _():
        @pl.when(chg_ref[t] != 0)
        def _():
            @pl.when(t == 0)
            def _():
                for cp in fetch(te_ref[0]):
                    cp.start(priority=1)

            for cp in fetch(te_ref[t]):
                cp.wait()
            for _, wf, wb in stages:
                _cast_rows(wf, wb)

            @pl.when(nxt_ref[t] >= 0)
            def _():
                for cp in fetch(nxt_ref[t]):
                    cp.start(priority=1)

        x = _load_row_tiled(xs_ref, EXPERT_TILE, wb_g.shape[0] // LANES).astype(BF16)
        g = _dot(x, wb_g[...])
        u = _dot(x, wb_u[...])
        hid = (g * jax.nn.sigmoid(g) * u).astype(BF16)
        o_ref[...] = _dot(hid, wb_d[...])

    @pl.when(t >= nt_ref[0])
    def _():
        o_ref[...] = jnp.zeros_like(o_ref)


def _experts(xs, w_gate, w_up, w_down, plan):
    d, de = w_gate.shape[1], w_gate.shape[2]
    n_c = d // LANES
    n_rows = xs.shape[0] // n_c
    tile = lambda t, *_: (t, 0)
    hbm = pl.BlockSpec(memory_space=pl.ANY)
    return pl.pallas_call(
        _experts_kernel,
        grid_spec=pltpu.PrefetchScalarGridSpec(
            num_scalar_prefetch=len(plan),
            grid=(n_rows // EXPERT_TILE,),
            in_specs=[pl.BlockSpec((EXPERT_TILE * n_c, LANES), tile), hbm, hbm, hbm],
            out_specs=pl.BlockSpec((EXPERT_TILE, d), tile),
            scratch_shapes=[pltpu.VMEM((d, de), F32), pltpu.VMEM((d, de), F32), pltpu.VMEM((de, d), F32),
                            pltpu.VMEM((d, de), BF16), pltpu.VMEM((d, de), BF16), pltpu.VMEM((de, d), BF16),
                            pltpu.SemaphoreType.DMA(())]),
        out_shape=jax.ShapeDtypeStruct((n_rows, d), F32),
        compiler_params=_params("arbitrary"),
        name="experts",
    )(*plan, xs, w_gate, w_up, w_down)


def _combine_kernel(pos_ref, x1_ref, ri_ref, mod_ref, gp_ref, ys_ref, o_ref, buf, sem, *, tm, n_l, n_tok):
    step = pl.program_id(0) * n_l + pl.program_id(1)
    n_steps = pl.num_programs(0) * n_l
    slot = step % 2

    def issue(s, sl):
        def body(rb, c):
            for u in range(DMA_UNROLL):
                r = rb * DMA_UNROLL + u
                for k in range(TOP_K):
                    p = pos_ref[k * n_tok + s * tm + r]
                    pltpu.make_async_copy(ys_ref.at[pl.ds(p, 1)], buf.at[sl, k, rb, pl.ds(u, 1)],
                                          sem.at[sl]).start(priority=k % 2)
            return c

        lax.fori_loop(0, tm // DMA_UNROLL, body, 0)

    @pl.when(step == 0)
    def _():
        issue(0, 0)

    @pl.when(step + 1 < n_steps)
    def _():
        issue(step + 1, 1 - slot)

    for k in range(TOP_K):
        pltpu.make_async_copy(buf.at[slot, k], buf.at[slot, k], sem.at[slot]).wait()

    ri = ri_ref[0]
    d = o_ref.shape[-1]
    moe = (ri[:, R_W0:R_W0 + 1] * buf[slot, 0].reshape(tm, d)
           + ri[:, R_W1:R_W1 + 1] * buf[slot, 1].reshape(tm, d))
    o_ref[0] = x1_ref[0] + mod_ref[0, 5:6, :] * _rms(moe, gp_ref[...])


def _combine(x1, rinfo, mod, g_post2, ys, pos):
    bsz, l, d = x1.shape
    tm = _tile(l, 256)
    n_l = l // tm
    tok = lambda b, i, p: (b, i, 0)
    return pl.pallas_call(
        functools.partial(_combine_kernel, tm=tm, n_l=n_l, n_tok=bsz * l),
        grid_spec=pltpu.PrefetchScalarGridSpec(
            num_scalar_prefetch=1,
            grid=(bsz, n_l),
            in_specs=[pl.BlockSpec((1, tm, d), tok),
                      pl.BlockSpec((1, tm, LANES), tok),
                      pl.BlockSpec((1, N_MOD, d), lambda b, i, p: (b, 0, 0)),
                      pl.BlockSpec((1, d), lambda b, i, p: (0, 0)),
                      pl.BlockSpec(memory_space=pl.ANY)],
            out_specs=pl.BlockSpec((1, tm, d), tok),
            scratch_shapes=[pltpu.VMEM((2, TOP_K, tm // DMA_UNROLL, DMA_UNROLL, d), F32),
                            pltpu.SemaphoreType.DMA((2,))]),
        out_shape=jax.ShapeDtypeStruct((bsz, l, d), F32),
        compiler_params=_params("arbitrary", "arbitrary"),
        name="combine",
    )(pos, x1, rinfo, mod, g_post2.reshape(1, d), ys)


def _rope_tables(seq, head_dim):
    axis_dim = head_dim // 2
    t = jnp.arange(seq, dtype=jnp.int32)
    pos = jnp.stack([t // GRID_W, t % GRID_W], axis=-1).astype(F32)
    inv_freq = ROPE_THETA ** (-jnp.arange(0, axis_dim, 2, dtype=F32) / axis_dim)
    ang = pos[:, :, None] * inv_freq
    cos, sin = jnp.cos(ang), jnp.sin(ang)
    cos_t = jnp.concatenate([cos, cos], axis=-1).reshape(seq, head_dim)
    sin_t = jnp.concatenate([-sin, sin], axis=-1).reshape(seq, head_dim)
    return cos_t, sin_t


def _route_plan(rt, cnt, n_groups, n_experts, n_tiles):
    e = rt[T_E0:T_E1 + 1]
    rank = jnp.stack([rt[T_R0_HI] * 256 + rt[T_R0_LO], rt[T_R1_HI] * 256 + rt[T_R1_LO]])
    counts = cnt[0, n_groups:n_groups + n_experts].astype(jnp.int32)
    tiles_e = (counts + EXPERT_TILE - 1) // EXPERT_TILE
    ids = jnp.arange(n_experts, dtype=jnp.int32)
    tile_end = jnp.sum(jnp.where(ids[None, :] <= ids[:, None], tiles_e[None, :], 0), axis=1)
    tile_start = tile_end - tiles_e
    nt = tile_end[-1]
    row0 = tile_start * EXPERT_TILE
    pos = jnp.sum(jnp.where(e[None] == ids[:, None, None], row0[:, None, None], 0), axis=0) + rank
    t = jnp.arange(n_tiles, dtype=jnp.int32)
    owner = lambda q: jnp.sum((tile_end[None, :] <= q[:, None]).astype(jnp.int32), axis=1)
    te = owner(jnp.minimum(t, nt - 1))
    chg = ((t == 0) | (te != owner(jnp.minimum(jnp.maximum(t - 1, 0), nt - 1)))).astype(jnp.int32)
    partial_last = jnp.any((tile_end[None, :] - 1 == t[:, None]) & (counts[None, :] % EXPERT_TILE != 0), axis=1)
    zflag = ((t >= nt) | partial_last).astype(jnp.int32)
    used = tiles_e > 0
    later = used[None, :] & (ids[None, :] > te[:, None])
    nxt = jnp.min(jnp.where(later, ids[None, :], n_experts), axis=1)
    nxt = jnp.where(nxt == n_experts, -1, nxt)
    return pos.reshape(-1), (te, chg, nxt, nt.reshape(1)), zflag


def kernel(x, c, ctx, c_ctx, w_ada, b_ada, g_pre1, g_post1, g_pre2, g_post2, w_in, q_norm, k_norm,
           gm_ln, w_s, b_s, w_ba, w_bg, w_o, w_rg, b_rg, w_re, b_re, w_gate, w_up, w_down):
    bsz, seq, d = x.shape
    depth = w_ada.shape[0]
    head_dim = q_norm.shape[-1]
    q_w, gm_w = w_ba.shape[1], w_bg.shape[1]
    kv_w = (w_in.shape[2] - q_w - 2 * gm_w - 2 * d) // 2
    n_groups, per_group = w_re.shape[2], w_re.shape[3]
    n_experts = n_groups * per_group
    assert head_dim == LANES and w_s.shape[2] == LANES and gm_w // w_s.shape[1] == LANES
    assert n_groups + n_experts <= LANES and seq % GRID_W == 0
    col_q = 2 * kv_w
    col_u, col_v = col_q + q_w, col_q + q_w + gm_w
    col_ga, col_gg = col_v + gm_w, col_v + gm_w + d
    n_tok = bsz * seq
    n_rows = n_tok * TOP_K + n_experts * EXPERT_TILE
    n_tiles = n_rows // EXPERT_TILE

    cos_t, sin_t = _rope_tables(seq, head_dim)
    pad = (-(bsz + 1)) % (2 * SUBLANES)
    cs = jnp.concatenate([c, c_ctx[None, :], jnp.zeros((pad, d), F32)], axis=0)

    for l in range(depth):
        assert l + 1 == depth, "context-stream update for non-final layers is not implemented"
        mod = _ada(cs, w_ada[l], b_ada[l]).reshape(cs.shape[0], N_MOD, d)
        w_in_b = w_in[l].astype(BF16)
        w_ba_b, w_bg_b, w_o_b = w_ba[l].astype(BF16), w_bg[l].astype(BF16), w_o[l].astype(BF16)
        bs_full = jnp.repeat(b_s[l].T, LANES, axis=1)
        w_r = jnp.concatenate([w_rg[l], w_re[l].reshape(d, n_experts),
                               jnp.zeros((d, LANES - n_groups - n_experts), F32)], axis=1)
        b_r = jnp.concatenate([b_rg[l], b_re[l].reshape(n_experts),
                               jnp.zeros((LANES - n_groups - n_experts,), F32)]).reshape(1, LANES)

        hc = _prenorm(ctx, mod, g_pre1[l], lambda b: bsz)
        kc, vc = _project_kv(hc, w_in_b, k_norm[l], kv_w, None, None)
        hx, kx, vx, qx, gm = _inproj(x, mod, g_pre1[l], w_in_b, k_norm[l], q_norm[l], cos_t, sin_t, gm_ln[l],
                                     w_s[l], bs_full, kv_w, q_w, gm_w, head_dim ** -0.5 * LOG2_E)
        attn = _attention(qx, kc, vc, kx, vx, kv_w // head_dim)
        merged = _merge(hx, attn, gm, w_in_b, w_ba_b, w_bg_b, col_ga, col_gg)

        x1, h2, rinfo, rt, cnt = _out_route(merged, x, mod, g_post1[l], g_pre2[l], w_o_b, w_r, b_r,
                                            n_groups, per_group)
        pos, plan, zflag = _route_plan(rt, cnt, n_groups, n_experts, n_tiles)

        xs = _dispatch(h2, pos, zflag, n_rows, d)
        ys = _experts(xs, w_gate[l], w_up[l], w_down[l], plan)
        x = _combine(x1, rinfo, mod, g_post2[l], ys, pos)
    return x
```

```python
import functools

import jax
import jax.numpy as jnp
from jax import lax
from jax.experimental import pallas as pl
from jax.experimental.pallas import tpu as pltpu

GRID_W = 64
ROPE_THETA = 10000.0
EPS = 1e-6
N_MOD = 6
TOP_K = 2
LOG2_E = 1.4426950408889634

LANES = 128
SUBLANES = 8
VMEM_LIMIT_BYTES = 56 * 1024 * 1024

EXPERT_TILE = 256
DMA_UNROLL = SUBLANES

F32 = jnp.float32
BF16 = jnp.bfloat16


def _params(*sem):
    return pltpu.CompilerParams(dimension_semantics=sem, vmem_limit_bytes=VMEM_LIMIT_BYTES)


def _dot(a, b):
    return jnp.dot(a, b, preferred_element_type=F32)


def _split_bf16(a):
    hi = a.astype(BF16)
    lo = (a - hi.astype(F32)).astype(BF16)
    return hi, lo


def _dot3(a, w):
    a_hi, a_lo = _split_bf16(a)
    w_hi, w_lo = _split_bf16(w)
    return _dot(a_hi, w_hi) + _dot(a_lo, w_hi) + _dot(a_hi, w_lo)


def _rms(x, g):
    return x * lax.rsqrt(jnp.mean(x * x, axis=-1, keepdims=True) + EPS) * g


def _store_row_tiled(ref, t0, val):
    rows, d = val.shape
    n_c = d // LANES
    for c in range(n_c):
        ref[pl.ds(t0 * n_c + c, rows, stride=n_c), :] = val[:, c * LANES:(c + 1) * LANES]


def _load_row_tiled(ref, rows, n_c):
    return jnp.concatenate([ref[pl.ds(c, rows, stride=n_c), :] for c in range(n_c)], axis=1)


def _tile(n, pref):
    t = min(n, pref)
    while n % t:
        t //= 2
    return t


def _ada_kernel(c_ref, w_ref, b_ref, o_ref):
    c = c_ref[...]
    a = c * jax.nn.sigmoid(c)
    o_ref[...] = _dot3(a, w_ref[...]) + b_ref[...]


def _ada(cs, w, b):
    m, d = cs.shape
    n = w.shape[1]
    tn = _tile(n, 2048)
    return pl.pallas_call(
        _ada_kernel,
        grid=(n // tn,),
        in_specs=[pl.BlockSpec((m, d), lambda j: (0, 0)),
                  pl.BlockSpec((d, tn), lambda j: (0, j)),
                  pl.BlockSpec((1, tn), lambda j: (0, j))],
        out_specs=pl.BlockSpec((m, tn), lambda j: (0, j)),
        out_shape=jax.ShapeDtypeStruct((m, n), F32),
        compiler_params=_params("arbitrary"),
        name="ada",
    )(cs, w, b.reshape(1, n))


def _prenorm_kernel(x_ref, mod_ref, g_ref, o_ref):
    y = _rms(x_ref[0], g_ref[...])
    o_ref[0] = (y * (1.0 + mod_ref[0, 1:2, :]) + mod_ref[0, 0:1, :]).astype(BF16)


def _prenorm(x, mod, g, mod_row):
    bsz, l, d = x.shape
    tm = _tile(l, 512)
    return pl.pallas_call(
        _prenorm_kernel,
        grid=(bsz, l // tm),
        in_specs=[pl.BlockSpec((1, tm, d), lambda b, i: (b, i, 0)),
                  pl.BlockSpec((1, N_MOD, d), lambda b, i: (mod_row(b), 0, 0)),
                  pl.BlockSpec((1, d), lambda b, i: (0, 0))],
        out_specs=pl.BlockSpec((1, tm, d), lambda b, i: (b, i, 0)),
        out_shape=jax.ShapeDtypeStruct((bsz, l, d), BF16),
        compiler_params=_params("arbitrary", "arbitrary"),
        name="prenorm",
    )(x, mod, g.reshape(1, d))


def _swap32(x):
    lane = lax.broadcasted_iota(jnp.int32, x.shape, 1)
    fwd = pltpu.roll(x, LANES - 32, 1)
    bwd = pltpu.roll(x, 32, 1)
    return jnp.where((lane & 32) == 0, fwd, bwd)


def _norm_head(r, gain, cos, sin, scale):
    y = _rms(r, gain)
    if cos is not None:
        y = y * cos + _swap32(y) * sin
    if scale != 1.0:
        y = y * scale
    return y.astype(BF16)


def _kv_kernel(*refs, n_kv, rope):
    if rope:
        h_ref, w_ref, g_ref, cos_ref, sin_ref, k_ref, v_ref = refs
        cos, sin = cos_ref[...], sin_ref[...]
    else:
        h_ref, w_ref, g_ref, k_ref, v_ref = refs
        cos = sin = None
    kv_w = n_kv * LANES
    r = _dot(h_ref[0], w_ref[...])
    for hh in range(n_kv):
        sl = slice(hh * LANES, (hh + 1) * LANES)
        k_ref[0, :, sl] = _norm_head(r[:, sl], g_ref[...], cos, sin, 1.0)
    v_ref[0] = r[:, kv_w:].astype(BF16)


def _project_kv(h, w_in_b, k_norm, kv_w, cos, sin):
    bsz, l, d = h.shape
    tm = _tile(l, 512)
    rope = cos is not None
    in_specs = [pl.BlockSpec((1, tm, d), lambda b, i: (b, i, 0)),
                pl.BlockSpec((d, 2 * kv_w), lambda b, i: (0, 0)),
                pl.BlockSpec((1, LANES), lambda b, i: (0, 0))]
    args = [h, w_in_b, k_norm.reshape(1, LANES)]
    if rope:
        in_specs += [pl.BlockSpec((tm, LANES), lambda b, i: (i, 0))] * 2
        args += [cos, sin]
    out = jax.ShapeDtypeStruct((bsz, l, kv_w), BF16)
    return pl.pallas_call(
        functools.partial(_kv_kernel, n_kv=kv_w // LANES, rope=rope),
        grid=(bsz, l // tm),
        in_specs=in_specs,
        out_specs=[pl.BlockSpec((1, tm, kv_w), lambda b, i: (b, i, 0))] * 2,
        out_shape=[out, out],
        compiler_params=_params("arbitrary", "arbitrary"),
        name="proj_kv",
    )(*args)


def _gelu(x):
    c = 0.7978845608028654
    return x * (0.5 + 0.5 * jnp.tanh(x * (c + (c * 0.044715) * (x * x))))


COL_CHUNK = 512


def _inproj_kernel(x_ref, mod_ref, g_ref, w_ref, kn_ref, qn_ref, cos_ref, sin_ref, ln_ref, ws_ref, bs_ref,
                   h_ref, k_ref, v_ref, q_ref, gm_ref, *, kv_w, q_w, gm_w, scale):
    x = x_ref[0]
    inv = lax.rsqrt(jnp.mean(x * x, axis=-1, keepdims=True) + EPS)
    h = ((x * inv) * (g_ref[...] * (1.0 + mod_ref[0, 1:2, :])) + mod_ref[0, 0:1, :]).astype(BF16)
    h_ref[0] = h
    cos, sin = cos_ref[...], sin_ref[...]
    tm = h.shape[0]

    rk = _dot(h, w_ref[:, 0:kv_w])
    for hh in range(kv_w // LANES):
        sl = slice(hh * LANES, (hh + 1) * LANES)
        k_ref[0, :, sl] = _norm_head(rk[:, sl], kn_ref[...], cos, sin, 1.0)
    v_ref[0] = _dot(h, w_ref[:, kv_w:2 * kv_w]).astype(BF16)

    col_q = 2 * kv_w
    cq = min(COL_CHUNK, q_w)
    for j in range(q_w // cq):
        r = _dot(h, w_ref[:, col_q + j * cq:col_q + (j + 1) * cq])
        for hh in range(cq // LANES):
            sl = slice(hh * LANES, (hh + 1) * LANES)
            q_ref[0, :, j * cq + hh * LANES:j * cq + (hh + 1) * LANES] = _norm_head(
                r[:, sl], qn_ref[...], cos, sin, scale)

    col_u, col_v = col_q + q_w, col_q + q_w + gm_w
    cg = min(COL_CHUNK, gm_w)
    for j in range(gm_w // cg):
        gu = _gelu(_dot(h, w_ref[:, col_u + j * cg:col_u + (j + 1) * cg]))
        gv = _gelu(_dot(h, w_ref[:, col_v + j * cg:col_v + (j + 1) * cg]))
        for g in range(cg // LANES):
            cs = slice(g * LANES, (g + 1) * LANES)
            oc = slice(j * cg + g * LANES, j * cg + (g + 1) * LANES)
            v = gv[:, cs]
            vc = v - jnp.mean(v, axis=-1, keepdims=True)
            vn = vc * lax.rsqrt(jnp.mean(vc * vc, axis=-1, keepdims=True) + EPS) * ln_ref[:, oc]
            vn = vn.astype(BF16)
            w = ws_ref[j * (cg // LANES) + g].astype(BF16)
            for c in range(tm // LANES):
                rs = slice(c * LANES, (c + 1) * LANES)
                s = _dot(w, vn[rs, :]) + bs_ref[:, oc]
                gm_ref[0, rs, oc] = (gu[rs, cs] * s).astype(BF16)


def _inproj(x, mod, g_pre, w_in_b, k_norm, q_norm, cos, sin, gm_ln, w_s, bs_full, kv_w, q_w, gm_w, scale):
    bsz, l, d = x.shape
    tm = _tile(l, 512)
    n_cols = 2 * kv_w + q_w + 2 * gm_w
    assert tm % LANES == 0
    tok = lambda b, i: (b, i, 0)
    c2 = lambda b, i: (0, 0)
    return pl.pallas_call(
        functools.partial(_inproj_kernel, kv_w=kv_w, q_w=q_w, gm_w=gm_w, scale=scale),
        grid=(bsz, l // tm),
        in_specs=[pl.BlockSpec((1, tm, d), tok),
                  pl.BlockSpec((1, N_MOD, d), lambda b, i: (b, 0, 0)),
                  pl.BlockSpec((1, d), c2),
                  pl.BlockSpec((d, n_cols), c2, pipeline_mode=pl.Buffered(1)),
                  pl.BlockSpec((1, LANES), c2),
                  pl.BlockSpec((1, LANES), c2),
                  pl.BlockSpec((tm, LANES), lambda b, i: (i, 0)),
                  pl.BlockSpec((tm, LANES), lambda b, i: (i, 0)),
                  pl.BlockSpec((1, gm_w), c2),
                  pl.BlockSpec(w_s.shape, lambda b, i: (0, 0, 0)),
                  pl.BlockSpec(bs_full.shape, c2)],
        out_specs=[pl.BlockSpec((1, tm, d), tok),
                   pl.BlockSpec((1, tm, kv_w), tok),
                   pl.BlockSpec((1, tm, kv_w), tok),
                   pl.BlockSpec((1, tm, q_w), tok),
                   pl.BlockSpec((1, tm, gm_w), tok)],
        out_shape=[jax.ShapeDtypeStruct((bsz, l, d), BF16),
                   jax.ShapeDtypeStruct((bsz, l, kv_w), BF16),
                   jax.ShapeDtypeStruct((bsz, l, kv_w), BF16),
                   jax.ShapeDtypeStruct((bsz, l, q_w), BF16),
                   jax.ShapeDtypeStruct((bsz, l, gm_w), BF16)],
        compiler_params=_params("arbitrary", "arbitrary"),
        name="inproj",
    )(x, mod, g_pre.reshape(1, d), w_in_b, k_norm.reshape(1, LANES), q_norm.reshape(1, LANES), cos, sin,
      gm_ln.reshape(1, gm_w), w_s, bs_full)


def _attn_kernel(q_ref, kc_ref, vc_ref, kx_ref, vx_ref, o_ref, *, n_kv, grp):
    nt = (((1,), (1,)), ((), ()))
    for kv in range(n_kv):
        ks = slice(kv * LANES, (kv + 1) * LANES)
        kc, kx = kc_ref[0, :, ks], kx_ref[0, :, ks]
        vc = jnp.concatenate([vc_ref[0, :, ks], jnp.ones_like(kc)], axis=1)
        vx = jnp.concatenate([vx_ref[0, :, ks], jnp.ones_like(kx)], axis=1)
        for hh in range(grp):
            sl = slice((kv * grp + hh) * LANES, (kv * grp + hh + 1) * LANES)
            q = q_ref[0, :, sl]
            sc = lax.dot_general(q, kc, nt, preferred_element_type=F32)
            sx = lax.dot_general(q, kx, nt, preferred_element_type=F32)
            m = jnp.maximum(jnp.max(sc, axis=-1, keepdims=True), jnp.max(sx, axis=-1, keepdims=True))
            pc = jnp.exp2(sc - m).astype(BF16)
            px = jnp.exp2(sx - m).astype(BF16)
            o = _dot(pc, vc) + _dot(px, vx)
            o_ref[0, :, sl] = (o[:, :LANES] / o[:, LANES:LANES + 1]).astype(BF16)


def _attention(q, kc, vc, kx, vx, n_kv):
    bsz, l, q_w = q.shape
    lc, kv_w = kc.shape[1], kc.shape[2]
    tq = _tile(l, 1024)
    tok = lambda b, i: (b, i, 0)
    whole = lambda b, i: (b, 0, 0)
    return pl.pallas_call(
        functools.partial(_attn_kernel, n_kv=n_kv, grp=q_w // kv_w),
        grid=(bsz, l // tq),
        in_specs=[pl.BlockSpec((1, tq, q_w), tok),
                  pl.BlockSpec((1, lc, kv_w), whole),
                  pl.BlockSpec((1, lc, kv_w), whole),
                  pl.BlockSpec((1, l, kv_w), whole),
                  pl.BlockSpec((1, l, kv_w), whole)],
        out_specs=pl.BlockSpec((1, tq, q_w), tok),
        out_shape=jax.ShapeDtypeStruct((bsz, l, q_w), BF16),
        compiler_params=_params("arbitrary", "arbitrary"),
        name="attention",
    )(q, kc, vc, kx, vx)


def _merge_kernel(h_ref, a_ref, g_ref, *refs, n_chunk):
    wga, wgg = refs[:n_chunk], refs[n_chunk:2 * n_chunk]
    wba_ref, wbg_ref, o_ref = refs[2 * n_chunk:]
    h, a, g = h_ref[0], a_ref[0], g_ref[0]
    tn = wga[0].shape[1]
    for j in range(n_chunk):
        cs = slice(j * tn, (j + 1) * tn)
        ga = jax.nn.sigmoid(_dot(h, wga[j][...]))
        gg = jax.nn.sigmoid(_dot(h, wgg[j][...]))
        o_ref[0, :, cs] = (ga * _dot(a, wba_ref[:, cs]) + gg * _dot(g, wbg_ref[:, cs])).astype(BF16)


def _merge(h, attn, gm, w_in_b, w_ba_b, w_bg_b, col_ga, col_gg):
    bsz, l, d = h.shape
    q_w, gm_w = attn.shape[2], gm.shape[2]
    tm = _tile(l, 512)
    tn = _tile(d, COL_CHUNK)
    assert col_ga % tn == 0 and col_gg % tn == 0
    n_chunk = d // tn
    tok = lambda b, i: (b, i, 0)
    gate_specs = [pl.BlockSpec((d, tn), functools.partial(lambda b, i, c: (0, c), c=(c0 // tn) + j),
                               pipeline_mode=pl.Buffered(1))
                  for c0 in (col_ga, col_gg) for j in range(n_chunk)]
    return pl.pallas_call(
        functools.partial(_merge_kernel, n_chunk=n_chunk),
        grid=(bsz, l // tm),
        in_specs=[pl.BlockSpec((1, tm, d), tok),
                  pl.BlockSpec((1, tm, q_w), tok),
                  pl.BlockSpec((1, tm, gm_w), tok),
                  *gate_specs,
                  pl.BlockSpec((q_w, d), lambda b, i: (0, 0)),
                  pl.BlockSpec((gm_w, d), lambda b, i: (0, 0))],
        out_specs=pl.BlockSpec((1, tm, d), tok),
        out_shape=jax.ShapeDtypeStruct((bsz, l, d), BF16),
        compiler_params=_params("arbitrary", "arbitrary"),
        name="merge",
    )(h, attn, gm, *([w_in_b] * (2 * n_chunk)), w_ba_b, w_bg_b)


R_W0, R_W1 = range(2)
T_E0, T_E1, T_R0_HI, T_R0_LO, T_R1_HI, T_R1_LO = range(6)


ROUTE_SUB = 512


def _out_kernel(m_ref, x_ref, mod_ref, gp1_ref, gp2_ref, wo_ref, wr_ref, br_ref,
                x1_ref, h2_ref, ri_ref, rt_ref, cnt_ref, carry_ref, *, n_groups, per_group):
    @pl.when((pl.program_id(0) == 0) & (pl.program_id(1) == 0))
    def _():
        carry_ref[...] = jnp.zeros_like(carry_ref)

    tm = m_ref.shape[1]
    sub = min(ROUTE_SUB, tm)
    wr_hi, wr_lo = _split_bf16(wr_ref[...])
    wr_both = jnp.concatenate([wr_hi, wr_lo], axis=1)
    lane = lax.broadcasted_iota(jnp.int32, (sub, LANES), 1)
    row = lax.broadcasted_iota(jnp.int32, (sub, sub), 0)
    col = lax.broadcasted_iota(jnp.int32, (sub, sub), 1)
    tri = jnp.where(col < row, 1.0, 0.0).astype(BF16)
    sel = jnp.where(lax.broadcasted_iota(jnp.int32, (SUBLANES, LANES), 0)
                    == lax.broadcasted_iota(jnp.int32, (SUBLANES, LANES), 1), 1.0, 0.0).astype(BF16)
    neg = jnp.float32(-jnp.inf)
    carry = carry_ref[...]

    for sb in range(tm // sub):
        rs = slice(sb * sub, (sb + 1) * sub)
        mix = _dot(m_ref[0, rs, :], wo_ref[...])
        inv1 = lax.rsqrt(jnp.mean(mix * mix, axis=-1, keepdims=True) + EPS)
        x1 = x_ref[0, rs, :] + (mix * inv1) * (mod_ref[0, 2:3, :] * gp1_ref[...])
        x1_ref[0, rs, :] = x1
        inv2 = lax.rsqrt(jnp.mean(x1 * x1, axis=-1, keepdims=True) + EPS)
        h2 = (x1 * inv2) * (gp2_ref[...] * (1.0 + mod_ref[0, 4:5, :])) + mod_ref[0, 3:4, :]
        h2_ref[0, rs, :] = h2

        h_hi, h_lo = _split_bf16(h2)
        both = _dot(h_hi, wr_both)
        logits = both[:, :LANES] + both[:, LANES:] + _dot(h_lo, wr_hi) + br_ref[...]
        lg = jnp.where(lane < n_groups, logits, neg)
        gmax = jnp.max(lg, axis=-1, keepdims=True)
        p_top = 1.0 / jnp.sum(jnp.exp(lg - gmax), axis=-1, keepdims=True)
        gidx = jnp.min(jnp.where(lg == gmax, lane, LANES), axis=-1, keepdims=True)
        lo = n_groups + gidx * per_group
        le = jnp.where((lane >= lo) & (lane < lo + per_group), logits, neg)
        l1 = jnp.max(le, axis=-1, keepdims=True)
        i1 = jnp.min(jnp.where(le == l1, lane, LANES), axis=-1, keepdims=True)
        le2 = jnp.where(lane == i1, neg, le)
        l2 = jnp.max(le2, axis=-1, keepdims=True)
        i2 = jnp.min(jnp.where(le2 == l2, lane, LANES), axis=-1, keepdims=True)
        r = jnp.exp(l2 - l1)
        w0 = p_top / (1.0 + r)
        w1 = p_top * r / (1.0 + r)

        oh1 = lane == i1
        oh2 = lane == i2
        oh = jnp.where(oh1 | oh2, 1.0, 0.0)
        base = _dot(tri, oh.astype(BF16)) + carry
        rank0 = jnp.sum(jnp.where(oh1, base, 0.0), axis=-1, keepdims=True)
        rank1 = jnp.sum(jnp.where(oh2, base, 0.0), axis=-1, keepdims=True)
        carry = carry + jnp.sum(oh, axis=0, keepdims=True)

        e0 = (i1 - n_groups).astype(F32)
        e1 = (i2 - n_groups).astype(F32)
        rec = jnp.zeros_like(logits)
        for idx, val in ((R_W0, w0), (R_W1, w1)):
            rec = jnp.where(lane == idx, val, rec)
        ri_ref[0, rs, :] = rec

        r0_hi = jnp.floor(rank0 * (1.0 / 256.0))
        r1_hi = jnp.floor(rank1 * (1.0 / 256.0))
        ints = jnp.zeros_like(logits)
        for idx, val in ((T_E0, e0), (T_E1, e1), (T_R0_HI, r0_hi), (T_R0_LO, rank0 - 256.0 * r0_hi),
                         (T_R1_HI, r1_hi), (T_R1_LO, rank1 - 256.0 * r1_hi)):
            ints = jnp.where(lane == idx, val, ints)
        rt = lax.dot_general(sel, ints.astype(BF16), (((1,), (1,)), ((), ())), preferred_element_type=F32)
        rt_ref[:, rs] = rt.astype(jnp.int32)

    carry_ref[...] = carry
    cnt_ref[...] = carry


def _out_route(merged, x, mod, g_post1, g_pre2, w_o_b, w_r, b_r, n_groups, per_group):
    bsz, l, d = x.shape
    tm = _tile(l, 512)
    tok = lambda b, i: (b, i, 0)
    const2 = lambda b, i: (0, 0)
    return pl.pallas_call(
        functools.partial(_out_kernel, n_groups=n_groups, per_group=per_group),
        grid=(bsz, l // tm),
        in_specs=[pl.BlockSpec((1, tm, d), tok),
                  pl.BlockSpec((1, tm, d), tok),
                  pl.BlockSpec((1, N_MOD, d), lambda b, i: (b, 0, 0)),
                  pl.BlockSpec((1, d), const2),
                  pl.BlockSpec((1, d), const2),
                  pl.BlockSpec((d, d), const2),
                  pl.BlockSpec((d, LANES), const2),
                  pl.BlockSpec((1, LANES), const2)],
        out_specs=[pl.BlockSpec((1, tm, d), tok),
                   pl.BlockSpec((1, tm, d), tok),
                   pl.BlockSpec((1, tm, LANES), tok),
                   pl.BlockSpec((SUBLANES, tm), lambda b, i: (0, b * (l // tm) + i)),
                   pl.BlockSpec((1, LANES), const2)],
        out_shape=[jax.ShapeDtypeStruct((bsz, l, d), F32),
                   jax.ShapeDtypeStruct((bsz, l, d), F32),
                   jax.ShapeDtypeStruct((bsz, l, LANES), F32),
                   jax.ShapeDtypeStruct((SUBLANES, bsz * l), jnp.int32),
                   jax.ShapeDtypeStruct((1, LANES), F32)],
        scratch_shapes=[pltpu.VMEM((1, LANES), F32)],
        compiler_params=_params("arbitrary", "arbitrary"),
        name="out_route",
    )(merged, x, mod, g_post1.reshape(1, d), g_pre2.reshape(1, d), w_o_b, w_r, b_r)


def _dispatch_kernel(pos_ref, zf_ref, h_ref, xs_ref, zbuf, stage, sem, zsem, *, tm, n_c, n_tiles, n_tok, n_steps):
    i = pl.program_id(0)
    tile_rows = EXPERT_TILE * n_c
    prev = (i + 1) % 2

    def zero_copy(t):
        return pltpu.make_async_copy(zbuf, xs_ref.at[pl.ds(pl.multiple_of(t * tile_rows, tile_rows), tile_rows)], zsem)

    @pl.when(i == 0)
    def _():
        zbuf[...] = jnp.zeros_like(zbuf)

        def issue(t, c):
            @pl.when(zf_ref[t] != 0)
            def _():
                zero_copy(t).start()
            return c

        def drain(t, c):
            @pl.when(zf_ref[t] != 0)
            def _():
                zero_copy(t).wait()
            return c

        lax.fori_loop(0, n_tiles, issue, 0)
        lax.fori_loop(0, n_tiles, drain, 0)

    def issue_rows(rb, c):
        for u in range(DMA_UNROLL):
            r = rb * DMA_UNROLL + u
            src = stage.at[prev, pl.ds(pl.multiple_of(r * n_c, n_c), n_c)]
            for k in range(TOP_K):
                p = pos_ref[k * n_tok + (i - 1) * tm + r]
                pltpu.make_async_copy(src, xs_ref.at[pl.ds(pl.multiple_of(p * n_c, n_c), n_c)],
                                      sem).start(priority=k % 2)
        return c

    @pl.when(i >= 1)
    def _():
        lax.fori_loop(0, tm // DMA_UNROLL, issue_rows, 0)

    @pl.when(i < n_steps)
    def _():
        _store_row_tiled(stage.at[i % 2], 0, h_ref[...])

    @pl.when(i >= 1)
    def _():
        for k in range(TOP_K):
            pltpu.make_async_copy(stage.at[prev], xs_ref.at[pl.ds(0, tm * n_c)], sem).wait()


def _dispatch(h2, pos, zflag, n_rows):
    n, d = h2.shape
    n_c = d // LANES
    tm = _tile(n, 256)
    n_steps = n // tm
    n_tiles = n_rows // EXPERT_TILE
    return pl.pallas_call(
        functools.partial(_dispatch_kernel, tm=tm, n_c=n_c, n_tiles=n_tiles, n_tok=n, n_steps=n_steps),
        grid_spec=pltpu.PrefetchScalarGridSpec(
            num_scalar_prefetch=2,
            grid=(n_steps + 1,),
            in_specs=[pl.BlockSpec((tm, d), lambda i, p, z: (jnp.minimum(i, n_steps - 1), 0))],
            out_specs=pl.BlockSpec(memory_space=pl.ANY),
            scratch_shapes=[pltpu.VMEM((EXPERT_TILE * n_c, LANES), F32),
                            pltpu.VMEM((2, tm * n_c, LANES), F32),
                            pltpu.SemaphoreType.DMA(()),
                            pltpu.SemaphoreType.DMA(())]),
        out_shape=jax.ShapeDtypeStruct((n_rows * n_c, LANES), F32),
        compiler_params=_params("arbitrary"),
        name="dispatch",
    )(pos, zflag, h2)


def _cast_rows(src_ref, dst_ref, chunk=256):
    rows = dst_ref.shape[0]
    chunk = min(chunk, rows)

    def body(c, carry):
        r0 = pl.multiple_of(c * chunk, chunk)
        dst_ref[pl.ds(r0, chunk), :] = src_ref[pl.ds(r0, chunk), :].astype(BF16)
        return carry

    lax.fori_loop(0, rows // chunk, body, 0)


def _experts_kernel(te_ref, chg_ref, nxt_ref, nt_ref, xs_ref, wg_hbm, wu_hbm, wd_hbm, o_ref,
                    wf_g, wf_u, wf_d, wb_g, wb_u, wb_d, sem):
    t = pl.program_id(0)
    stages = ((wg_hbm, wf_g, wb_g), (wu_hbm, wf_u, wb_u), (wd_hbm, wf_d, wb_d))

    def fetch(e):
        return [pltpu.make_async_copy(src.at[e], dst, sem) for src, dst, _ in stages]

    @pl.when(t < nt_ref[0])
    def _():
        @pl.when(chg_ref[t] != 0)
        def _():
            @pl.when(t == 0)
            def _():
                for cp in fetch(te_ref[0]):
                    cp.start(priority=1)

            for cp in fetch(te_ref[t]):
                cp.wait()
            for _, wf, wb in stages:
                _cast_rows(wf, wb)

            @pl.when(nxt_ref[t] >= 0)
            def _():
                for cp in fetch(nxt_ref[t]):
                    cp.start(priority=1)

        x = _load_row_tiled(xs_ref, EXPERT_TILE, wb_g.shape[0] // LANES).astype(BF16)
        g = _dot(x, wb_g[...])
        u = _dot(x, wb_u[...])
        hid = (g * jax.nn.sigmoid(g) * u).astype(BF16)
        o_ref[...] = _dot(hid, wb_d[...])

    @pl.when(t >= nt_ref[0])
    def _():
        o_ref[...] = jnp.zeros_like(o_ref)


def _experts(xs, w_gate, w_up, w_down, plan):
    d, de = w_gate.shape[1], w_gate.shape[2]
    n_c = d // LANES
    n_rows = xs.shape[0] // n_c
    tile = lambda t, *_: (t, 0)
    hbm = pl.BlockSpec(memory_space=pl.ANY)
    return pl.pallas_call(
        _experts_kernel,
        grid_spec=pltpu.PrefetchScalarGridSpec(
            num_scalar_prefetch=len(plan),
            grid=(n_rows // EXPERT_TILE,),
            in_specs=[pl.BlockSpec((EXPERT_TILE * n_c, LANES), tile), hbm, hbm, hbm],
            out_specs=pl.BlockSpec((EXPERT_TILE, d), tile),
            scratch_shapes=[pltpu.VMEM((d, de), F32), pltpu.VMEM((d, de), F32), pltpu.VMEM((de, d), F32),
                            pltpu.VMEM((d, de), BF16), pltpu.VMEM((d, de), BF16), pltpu.VMEM((de, d), BF16),
                            pltpu.SemaphoreType.DMA(())]),
        out_shape=jax.ShapeDtypeStruct((n_rows, d), F32),
        compiler_params=_params("arbitrary"),
        name="experts",
    )(*plan, xs, w_gate, w_up, w_down)


def _combine_kernel(pos_ref, x1_ref, ri_ref, mod_ref, gp_ref, ys_ref, o_ref, buf, sem, *, tm, n_l, n_tok):
    step = pl.program_id(0) * n_l + pl.program_id(1)
    n_steps = pl.num_programs(0) * n_l
    slot = step % 2

    def issue(s, sl):
        def body(rb, c):
            for u in range(DMA_UNROLL):
                r = rb * DMA_UNROLL + u
                for k in range(TOP_K):
                    p = pos_ref[k * n_tok + s * tm + r]
                    pltpu.make_async_copy(ys_ref.at[pl.ds(p, 1)], buf.at[sl, k, rb, pl.ds(u, 1)],
                                          sem.at[sl]).start(priority=k % 2)
            return c

        lax.fori_loop(0, tm // DMA_UNROLL, body, 0)

    @pl.when(step == 0)
    def _():
        issue(0, 0)

    @pl.when(step + 1 < n_steps)
    def _():
        issue(step + 1, 1 - slot)

    for k in range(TOP_K):
        pltpu.make_async_copy(buf.at[slot, k], buf.at[slot, k], sem.at[slot]).wait()

    ri = ri_ref[0]
    d = o_ref.shape[-1]
    moe = (ri[:, R_W0:R_W0 + 1] * buf[slot, 0].reshape(tm, d)
           + ri[:, R_W1:R_W1 + 1] * buf[slot, 1].reshape(tm, d))
    o_ref[0] = x1_ref[0] + mod_ref[0, 5:6, :] * _rms(moe, gp_ref[...])


def _combine(x1, rinfo, mod, g_post2, ys, pos):
    bsz, l, d = x1.shape
    tm = _tile(l, 256)
    n_l = l // tm
    tok = lambda b, i, p: (b, i, 0)
    return pl.pallas_call(
        functools.partial(_combine_kernel, tm=tm, n_l=n_l, n_tok=bsz * l),
        grid_spec=pltpu.PrefetchScalarGridSpec(
            num_scalar_prefetch=1,
            grid=(bsz, n_l),
            in_specs=[pl.BlockSpec((1, tm, d), tok),
                      pl.BlockSpec((1, tm, LANES), tok),
                      pl.BlockSpec((1, N_MOD, d), lambda b, i, p: (b, 0, 0)),
                      pl.BlockSpec((1, d), lambda b, i, p: (0, 0)),
                      pl.BlockSpec(memory_space=pl.ANY)],
            out_specs=pl.BlockSpec((1, tm, d), tok),
            scratch_shapes=[pltpu.VMEM((2, TOP_K, tm // DMA_UNROLL, DMA_UNROLL, d), F32),
                            pltpu.SemaphoreType.DMA((2,))]),
        out_shape=jax.ShapeDtypeStruct((bsz, l, d), F32),
        compiler_params=_params("arbitrary", "arbitrary"),
        name="combine",
    )(pos, x1, rinfo, mod, g_post2.reshape(1, d), ys)


def _rope_tables(seq, head_dim):
    axis_dim = head_dim // 2
    t = jnp.arange(seq, dtype=jnp.int32)
    pos = jnp.stack([t // GRID_W, t % GRID_W], axis=-1).astype(F32)
    inv_freq = ROPE_THETA ** (-jnp.arange(0, axis_dim, 2, dtype=F32) / axis_dim)
    ang = pos[:, :, None] * inv_freq
    cos, sin = jnp.cos(ang), jnp.sin(ang)
    cos_t = jnp.concatenate([cos, cos], axis=-1).reshape(seq, head_dim)
    sin_t = jnp.concatenate([-sin, sin], axis=-1).reshape(seq, head_dim)
    return cos_t, sin_t


def _route_plan(rt, cnt, n_groups, n_experts, n_tiles):
    e = rt[T_E0:T_E1 + 1]
    rank = jnp.stack([rt[T_R0_HI] * 256 + rt[T_R0_LO], rt[T_R1_HI] * 256 + rt[T_R1_LO]])
    counts = cnt[0, n_groups:n_groups + n_experts].astype(jnp.int32)
    tiles_e = (counts + EXPERT_TILE - 1) // EXPERT_TILE
    ids = jnp.arange(n_experts, dtype=jnp.int32)
    tile_end = jnp.sum(jnp.where(ids[None, :] <= ids[:, None], tiles_e[None, :], 0), axis=1)
    tile_start = tile_end - tiles_e
    nt = tile_end[-1]
    row0 = tile_start * EXPERT_TILE
    pos = jnp.sum(jnp.where(e[None] == ids[:, None, None], row0[:, None, None], 0), axis=0) + rank
    t = jnp.arange(n_tiles, dtype=jnp.int32)
    owner = lambda q: jnp.sum((tile_end[None, :] <= q[:, None]).astype(jnp.int32), axis=1)
    te = owner(jnp.minimum(t, nt - 1))
    chg = ((t == 0) | (te != owner(jnp.minimum(jnp.maximum(t - 1, 0), nt - 1)))).astype(jnp.int32)
    partial_last = jnp.any((tile_end[None, :] - 1 == t[:, None]) & (counts[None, :] % EXPERT_TILE != 0), axis=1)
    zflag = ((t >= nt) | partial_last).astype(jnp.int32)
    used = tiles_e > 0
    later = used[None, :] & (ids[None, :] > te[:, None])
    nxt = jnp.min(jnp.where(later, ids[None, :], n_experts), axis=1)
    nxt = jnp.where(nxt == n_experts, -1, nxt)
    return pos.reshape(-1), (te, chg, nxt, nt.reshape(1)), zflag


def kernel(x, c, ctx, c_ctx, w_ada, b_ada, g_pre1, g_post1, g_pre2, g_post2, w_in, q_norm, k_norm,
           gm_ln, w_s, b_s, w_ba, w_bg, w_o, w_rg, b_rg, w_re, b_re, w_gate, w_up, w_down):
    bsz, seq, d = x.shape
    depth = w_ada.shape[0]
    head_dim = q_norm.shape[-1]
    q_w, gm_w = w_ba.shape[1], w_bg.shape[1]
    kv_w = (w_in.shape[2] - q_w - 2 * gm_w - 2 * d) // 2
    n_groups, per_group = w_re.shape[2], w_re.shape[3]
    n_experts = n_groups * per_group
    assert head_dim == LANES and w_s.shape[2] == LANES and gm_w // w_s.shape[1] == LANES
    assert n_groups + n_experts <= LANES and seq % GRID_W == 0
    col_q = 2 * kv_w
    col_u, col_v = col_q + q_w, col_q + q_w + gm_w
    col_ga, col_gg = col_v + gm_w, col_v + gm_w + d
    n_tok = bsz * seq
    n_rows = n_tok * TOP_K + n_experts * EXPERT_TILE
    n_tiles = n_rows // EXPERT_TILE

    cos_t, sin_t = _rope_tables(seq, head_dim)
    pad = (-(bsz + 1)) % (2 * SUBLANES)
    cs = jnp.concatenate([c, c_ctx[None, :], jnp.zeros((pad, d), F32)], axis=0)

    for l in range(depth):
        assert l + 1 == depth, "context-stream update for non-final layers is not implemented"
        mod = _ada(cs, w_ada[l], b_ada[l]).reshape(cs.shape[0], N_MOD, d)
        w_in_b = w_in[l].astype(BF16)
        w_ba_b, w_bg_b, w_o_b = w_ba[l].astype(BF16), w_bg[l].astype(BF16), w_o[l].astype(BF16)
        bs_full = jnp.repeat(b_s[l].T, LANES, axis=1)
        w_r = jnp.concatenate([w_rg[l], w_re[l].reshape(d, n_experts),
                               jnp.zeros((d, LANES - n_groups - n_experts), F32)], axis=1)
        b_r = jnp.concatenate([b_rg[l], b_re[l].reshape(n_experts),
                               jnp.zeros((LANES - n_groups - n_experts,), F32)]).reshape(1, LANES)

        hc = _prenorm(ctx, mod, g_pre1[l], lambda b: bsz)
        kc, vc = _project_kv(hc, w_in_b, k_norm[l], kv_w, None, None)
        hx, kx, vx, qx, gm = _inproj(x, mod, g_pre1[l], w_in_b, k_norm[l], q_norm[l], cos_t, sin_t, gm_ln[l],
                                     w_s[l], bs_full, kv_w, q_w, gm_w, head_dim ** -0.5 * LOG2_E)
        attn = _attention(qx, kc, vc, kx, vx, kv_w // head_dim)
        merged = _merge(hx, attn, gm, w_in_b, w_ba_b, w_bg_b, col_ga, col_gg)

        x1, h2, rinfo, rt, cnt = _out_route(merged, x, mod, g_post1[l], g_pre2[l], w_o_b, w_r, b_r,
                                            n_groups, per_group)
        pos, plan, zflag = _route_plan(rt, cnt, n_groups, n_experts, n_tiles)

        xs = _dispatch(h2.reshape(n_tok, d), pos, zflag, n_rows)
        ys = _experts(xs, w_gate[l], w_up[l], w_down[l], plan)
        x = _combine(x1, rinfo, mod, g_post2[l], ys, pos)
    return x
```

```python
import functools

import jax
import jax.numpy as jnp
from jax import lax
from jax.experimental import pallas as pl
from jax.experimental.pallas import tpu as pltpu

GRID_W = 64
ROPE_THETA = 10000.0
EPS = 1e-6
N_MOD = 6
TOP_K = 2
LOG2_E = 1.4426950408889634

LANES = 128
SUBLANES = 8
VMEM_LIMIT_BYTES = 56 * 1024 * 1024

EXPERT_TILE = 256
DMA_UNROLL = SUBLANES

F32 = jnp.float32
BF16 = jnp.bfloat16


def _params(*sem):
    return pltpu.CompilerParams(dimension_semantics=sem, vmem_limit_bytes=VMEM_LIMIT_BYTES)


def _dot(a, b):
    return jnp.dot(a, b, preferred_element_type=F32)


def _split_bf16(a):
    hi = a.astype(BF16)
    lo = (a - hi.astype(F32)).astype(BF16)
    return hi, lo


def _dot3(a, w):
    a_hi, a_lo = _split_bf16(a)
    w_hi, w_lo = _split_bf16(w)
    return _dot(a_hi, w_hi) + _dot(a_lo, w_hi) + _dot(a_hi, w_lo)


def _rms(x, g):
    return x * lax.rsqrt(jnp.mean(x * x, axis=-1, keepdims=True) + EPS) * g


def _store_row_tiled(ref, t0, val):
    rows, d = val.shape
    n_c = d // LANES
    for c in range(n_c):
        ref[pl.ds(t0 * n_c + c, rows, stride=n_c), :] = val[:, c * LANES:(c + 1) * LANES]


def _load_row_tiled(ref, rows, n_c):
    return jnp.concatenate([ref[pl.ds(c, rows, stride=n_c), :] for c in range(n_c)], axis=1)


def _tile(n, pref):
    t = min(n, pref)
    while n % t:
        t //= 2
    return t


def _ada_kernel(c_ref, w_ref, b_ref, o_ref):
    c = c_ref[...]
    a = c * jax.nn.sigmoid(c)
    o_ref[...] = _dot3(a, w_ref[...]) + b_ref[...]


def _ada(cs, w, b):
    m, d = cs.shape
    n = w.shape[1]
    tn = _tile(n, 1024)
    return pl.pallas_call(
        _ada_kernel,
        grid=(n // tn,),
        in_specs=[pl.BlockSpec((m, d), lambda j: (0, 0)),
                  pl.BlockSpec((d, tn), lambda j: (0, j)),
                  pl.BlockSpec((1, tn), lambda j: (0, j))],
        out_specs=pl.BlockSpec((m, tn), lambda j: (0, j)),
        out_shape=jax.ShapeDtypeStruct((m, n), F32),
        compiler_params=_params("arbitrary"),
        name="ada",
    )(cs, w, b.reshape(1, n))


def _swap32(x):
    lane = lax.broadcasted_iota(jnp.int32, x.shape, 1)
    fwd = pltpu.roll(x, LANES - 32, 1)
    bwd = pltpu.roll(x, 32, 1)
    return jnp.where((lane & 32) == 0, fwd, bwd)


def _norm_head(r, gain, cos, sin, scale):
    y = _rms(r, gain)
    if cos is not None:
        y = y * cos + _swap32(y) * sin
    if scale != 1.0:
        y = y * scale
    return y.astype(BF16)


def _ctx_kv_kernel(x_ref, mod_ref, g_ref, w_ref, kn_ref, k_ref, v_ref, *, n_kv):
    x = x_ref[0]
    inv = lax.rsqrt(jnp.mean(x * x, axis=-1, keepdims=True) + EPS)
    h = ((x * inv) * (g_ref[...] * (1.0 + mod_ref[0, 1:2, :])) + mod_ref[0, 0:1, :]).astype(BF16)
    r = _dot(h, w_ref[...])
    for hh in range(n_kv):
        sl = slice(hh * LANES, (hh + 1) * LANES)
        k_ref[0, :, sl] = _norm_head(r[:, sl], kn_ref[...], None, None, 1.0)
    v_ref[0] = r[:, n_kv * LANES:].astype(BF16)


def _ctx_kv(ctx, mod, mod_row, g_pre, w_in_b, k_norm, kv_w):
    bsz, l, d = ctx.shape
    tm = _tile(l, 512)
    out = jax.ShapeDtypeStruct((bsz, l, kv_w), BF16)
    tok = lambda b, i: (b, i, 0)
    return pl.pallas_call(
        functools.partial(_ctx_kv_kernel, n_kv=kv_w // LANES),
        grid=(bsz, l // tm),
        in_specs=[pl.BlockSpec((1, tm, d), tok),
                  pl.BlockSpec((1, N_MOD, d), lambda b, i: (mod_row, 0, 0)),
                  pl.BlockSpec((1, d), lambda b, i: (0, 0)),
                  pl.BlockSpec((d, 2 * kv_w), lambda b, i: (0, 0)),
                  pl.BlockSpec((1, LANES), lambda b, i: (0, 0))],
        out_specs=[pl.BlockSpec((1, tm, kv_w), tok)] * 2,
        out_shape=[out, out],
        compiler_params=_params("arbitrary", "arbitrary"),
        name="ctx_kv",
    )(ctx, mod, g_pre.reshape(1, d), w_in_b, k_norm.reshape(1, LANES))


def _gelu(x):
    c = 0.7978845608028654
    return x * (0.5 + 0.5 * jnp.tanh(x * (c + (c * 0.044715) * (x * x))))


COL_CHUNK = 512


def _inproj_kernel(x_ref, mod_ref, g_ref, w_ref, kn_ref, qn_ref, cos_ref, sin_ref, ln_ref, ws_ref, bs_ref,
                   h_ref, k_ref, v_ref, q_ref, gm_ref, *, kv_w, q_w, gm_w, scale):
    x = x_ref[0]
    inv = lax.rsqrt(jnp.mean(x * x, axis=-1, keepdims=True) + EPS)
    h = ((x * inv) * (g_ref[...] * (1.0 + mod_ref[0, 1:2, :])) + mod_ref[0, 0:1, :]).astype(BF16)
    h_ref[0] = h
    cos, sin = cos_ref[...], sin_ref[...]
    tm = h.shape[0]

    rk = _dot(h, w_ref[:, 0:kv_w])
    for hh in range(kv_w // LANES):
        sl = slice(hh * LANES, (hh + 1) * LANES)
        k_ref[0, :, sl] = _norm_head(rk[:, sl], kn_ref[...], cos, sin, 1.0)
    v_ref[0] = _dot(h, w_ref[:, kv_w:2 * kv_w]).astype(BF16)

    col_q = 2 * kv_w
    cq = min(COL_CHUNK, q_w)
    for j in range(q_w // cq):
        r = _dot(h, w_ref[:, col_q + j * cq:col_q + (j + 1) * cq])
        for hh in range(cq // LANES):
            sl = slice(hh * LANES, (hh + 1) * LANES)
            q_ref[0, :, j * cq + hh * LANES:j * cq + (hh + 1) * LANES] = _norm_head(
                r[:, sl], qn_ref[...], cos, sin, scale)

    col_u, col_v = col_q + q_w, col_q + q_w + gm_w
    cg = min(COL_CHUNK, gm_w)
    for j in range(gm_w // cg):
        gu = _gelu(_dot(h, w_ref[:, col_u + j * cg:col_u + (j + 1) * cg]))
        gv = _gelu(_dot(h, w_ref[:, col_v + j * cg:col_v + (j + 1) * cg]))
        for g in range(cg // LANES):
            cs = slice(g * LANES, (g + 1) * LANES)
            oc = slice(j * cg + g * LANES, j * cg + (g + 1) * LANES)
            v = gv[:, cs]
            vc = v - jnp.mean(v, axis=-1, keepdims=True)
            vn = vc * lax.rsqrt(jnp.mean(vc * vc, axis=-1, keepdims=True) + EPS) * ln_ref[:, oc]
            vn = vn.astype(BF16)
            w = ws_ref[j * (cg // LANES) + g].astype(BF16)
            for c in range(tm // LANES):
                rs = slice(c * LANES, (c + 1) * LANES)
                s = _dot(w, vn[rs, :]) + bs_ref[:, oc]
                gm_ref[0, rs, oc] = (gu[rs, cs] * s).astype(BF16)


def _inproj(x, mod, g_pre, w_in_b, k_norm, q_norm, cos, sin, gm_ln, w_s, bs_full, kv_w, q_w, gm_w, scale):
    bsz, l, d = x.shape
    tm = _tile(l, 512)
    n_cols = 2 * kv_w + q_w + 2 * gm_w
    assert tm % LANES == 0
    tok = lambda b, i: (b, i, 0)
    c2 = lambda b, i: (0, 0)
    return pl.pallas_call(
        functools.partial(_inproj_kernel, kv_w=kv_w, q_w=q_w, gm_w=gm_w, scale=scale),
        grid=(bsz, l // tm),
        in_specs=[pl.BlockSpec((1, tm, d), tok),
                  pl.BlockSpec((1, N_MOD, d), lambda b, i: (b, 0, 0)),
                  pl.BlockSpec((1, d), c2),
                  pl.BlockSpec((d, n_cols), c2, pipeline_mode=pl.Buffered(1)),
                  pl.BlockSpec((1, LANES), c2),
                  pl.BlockSpec((1, LANES), c2),
                  pl.BlockSpec((tm, LANES), lambda b, i: (i, 0)),
                  pl.BlockSpec((tm, LANES), lambda b, i: (i, 0)),
                  pl.BlockSpec((1, gm_w), c2),
                  pl.BlockSpec(w_s.shape, lambda b, i: (0, 0, 0)),
                  pl.BlockSpec(bs_full.shape, c2)],
        out_specs=[pl.BlockSpec((1, tm, d), tok),
                   pl.BlockSpec((1, tm, kv_w), tok),
                   pl.BlockSpec((1, tm, kv_w), tok),
                   pl.BlockSpec((1, tm, q_w), tok),
                   pl.BlockSpec((1, tm, gm_w), tok)],
        out_shape=[jax.ShapeDtypeStruct((bsz, l, d), BF16),
                   jax.ShapeDtypeStruct((bsz, l, kv_w), BF16),
                   jax.ShapeDtypeStruct((bsz, l, kv_w), BF16),
                   jax.ShapeDtypeStruct((bsz, l, q_w), BF16),
                   jax.ShapeDtypeStruct((bsz, l, gm_w), BF16)],
        compiler_params=_params("arbitrary", "arbitrary"),
        name="inproj",
    )(x, mod, g_pre.reshape(1, d), w_in_b, k_norm.reshape(1, LANES), q_norm.reshape(1, LANES), cos, sin,
      gm_ln.reshape(1, gm_w), w_s, bs_full)


def _attn_kernel(q_ref, kc_ref, vc_ref, kx_ref, vx_ref, o_ref, *, n_kv, grp):
    nt = (((1,), (1,)), ((), ()))
    for kv in range(n_kv):
        ks = slice(kv * LANES, (kv + 1) * LANES)
        kc, kx = kc_ref[0, :, ks], kx_ref[0, :, ks]
        vc = jnp.concatenate([vc_ref[0, :, ks], jnp.ones_like(kc)], axis=1)
        vx = jnp.concatenate([vx_ref[0, :, ks], jnp.ones_like(kx)], axis=1)
        for hh in range(grp):
            sl = slice((kv * grp + hh) * LANES, (kv * grp + hh + 1) * LANES)
            q = q_ref[0, :, sl]
            sc = lax.dot_general(q, kc, nt, preferred_element_type=F32)
            sx = lax.dot_general(q, kx, nt, preferred_element_type=F32)
            m = jnp.maximum(jnp.max(sc, axis=-1, keepdims=True), jnp.max(sx, axis=-1, keepdims=True))
            pc = jnp.exp2(sc - m).astype(BF16)
            px = jnp.exp2(sx - m).astype(BF16)
            o = _dot(pc, vc) + _dot(px, vx)
            o_ref[0, :, sl] = (o[:, :LANES] / o[:, LANES:LANES + 1]).astype(BF16)


def _attention(q, kc, vc, kx, vx, n_kv):
    bsz, l, q_w = q.shape
    lc, kv_w = kc.shape[1], kc.shape[2]
    tq = _tile(l, 1024)
    tok = lambda b, i: (b, i, 0)
    whole = lambda b, i: (b, 0, 0)
    return pl.pallas_call(
        functools.partial(_attn_kernel, n_kv=n_kv, grp=q_w // kv_w),
        grid=(bsz, l // tq),
        in_specs=[pl.BlockSpec((1, tq, q_w), tok),
                  pl.BlockSpec((1, lc, kv_w), whole),
                  pl.BlockSpec((1, lc, kv_w), whole),
                  pl.BlockSpec((1, l, kv_w), whole),
                  pl.BlockSpec((1, l, kv_w), whole)],
        out_specs=pl.BlockSpec((1, tq, q_w), tok),
        out_shape=jax.ShapeDtypeStruct((bsz, l, q_w), BF16),
        compiler_params=_params("arbitrary", "arbitrary"),
        name="attention",
    )(q, kc, vc, kx, vx)


def _merge_kernel(h_ref, a_ref, g_ref, *refs, n_chunk):
    wga, wgg = refs[:n_chunk], refs[n_chunk:2 * n_chunk]
    wba_ref, wbg_ref, o_ref = refs[2 * n_chunk:]
    h, a, g = h_ref[0], a_ref[0], g_ref[0]
    tn = wga[0].shape[1]
    for j in range(n_chunk):
        cs = slice(j * tn, (j + 1) * tn)
        ga = jax.nn.sigmoid(_dot(h, wga[j][...]))
        gg = jax.nn.sigmoid(_dot(h, wgg[j][...]))
        o_ref[0, :, cs] = (ga * _dot(a, wba_ref[:, cs]) + gg * _dot(g, wbg_ref[:, cs])).astype(BF16)


def _merge(h, attn, gm, w_in_b, w_ba_b, w_bg_b, col_ga, col_gg):
    bsz, l, d = h.shape
    q_w, gm_w = attn.shape[2], gm.shape[2]
    tm = _tile(l, 512)
    tn = _tile(d, COL_CHUNK)
    assert col_ga % tn == 0 and col_gg % tn == 0
    n_chunk = d // tn
    tok = lambda b, i: (b, i, 0)
    gate_specs = [pl.BlockSpec((d, tn), functools.partial(lambda b, i, c: (0, c), c=(c0 // tn) + j),
                               pipeline_mode=pl.Buffered(1))
                  for c0 in (col_ga, col_gg) for j in range(n_chunk)]
    return pl.pallas_call(
        functools.partial(_merge_kernel, n_chunk=n_chunk),
        grid=(bsz, l // tm),
        in_specs=[pl.BlockSpec((1, tm, d), tok),
                  pl.BlockSpec((1, tm, q_w), tok),
                  pl.BlockSpec((1, tm, gm_w), tok),
                  *gate_specs,
                  pl.BlockSpec((q_w, d), lambda b, i: (0, 0)),
                  pl.BlockSpec((gm_w, d), lambda b, i: (0, 0))],
        out_specs=pl.BlockSpec((1, tm, d), tok),
        out_shape=jax.ShapeDtypeStruct((bsz, l, d), BF16),
        compiler_params=_params("arbitrary", "arbitrary"),
        name="merge",
    )(h, attn, gm, *([w_in_b] * (2 * n_chunk)), w_ba_b, w_bg_b)


R_W0, R_W1 = range(2)
T_E0, T_E1, T_R0_HI, T_R0_LO, T_R1_HI, T_R1_LO = range(6)


ROUTE_SUB = 512


def _out_kernel(m_ref, x_ref, mod_ref, gp1_ref, gp2_ref, wo_ref, wr_ref, br_ref,
                x1_ref, h2_ref, ri_ref, rt_ref, cnt_ref, carry_ref, *, n_groups, per_group):
    @pl.when((pl.program_id(0) == 0) & (pl.program_id(1) == 0))
    def _():
        carry_ref[...] = jnp.zeros_like(carry_ref)

    tm = m_ref.shape[1]
    sub = min(ROUTE_SUB, tm)
    wr_hi, wr_lo = _split_bf16(wr_ref[...])
    wr_both = jnp.concatenate([wr_hi, wr_lo], axis=1)
    lane = lax.broadcasted_iota(jnp.int32, (sub, LANES), 1)
    row = lax.broadcasted_iota(jnp.int32, (sub, sub), 0)
    col = lax.broadcasted_iota(jnp.int32, (sub, sub), 1)
    tri = jnp.where(col < row, 1.0, 0.0).astype(BF16)
    sel = jnp.where(lax.broadcasted_iota(jnp.int32, (SUBLANES, LANES), 0)
                    == lax.broadcasted_iota(jnp.int32, (SUBLANES, LANES), 1), 1.0, 0.0).astype(BF16)
    neg = jnp.float32(-jnp.inf)
    carry = carry_ref[...]

    for sb in range(tm // sub):
        rs = slice(sb * sub, (sb + 1) * sub)
        mix = _dot(m_ref[0, rs, :], wo_ref[...])
        inv1 = lax.rsqrt(jnp.mean(mix * mix, axis=-1, keepdims=True) + EPS)
        x1 = x_ref[0, rs, :] + (mix * inv1) * (mod_ref[0, 2:3, :] * gp1_ref[...])
        x1_ref[0, rs, :] = x1
        inv2 = lax.rsqrt(jnp.mean(x1 * x1, axis=-1, keepdims=True) + EPS)
        h2 = (x1 * inv2) * (gp2_ref[...] * (1.0 + mod_ref[0, 4:5, :])) + mod_ref[0, 3:4, :]
        h2_ref[0, rs, :] = h2

        h_hi, h_lo = _split_bf16(h2)
        both = _dot(h_hi, wr_both)
        logits = both[:, :LANES] + both[:, LANES:] + _dot(h_lo, wr_hi) + br_ref[...]
        lg = jnp.where(lane < n_groups, logits, neg)
        gmax = jnp.max(lg, axis=-1, keepdims=True)
        p_top = 1.0 / jnp.sum(jnp.exp(lg - gmax), axis=-1, keepdims=True)
        gidx = jnp.min(jnp.where(lg == gmax, lane, LANES), axis=-1, keepdims=True)
        lo = n_groups + gidx * per_group
        le = jnp.where((lane >= lo) & (lane < lo + per_group), logits, neg)
        l1 = jnp.max(le, axis=-1, keepdims=True)
        i1 = jnp.min(jnp.where(le == l1, lane, LANES), axis=-1, keepdims=True)
        le2 = jnp.where(lane == i1, neg, le)
        l2 = jnp.max(le2, axis=-1, keepdims=True)
        i2 = jnp.min(jnp.where(le2 == l2, lane, LANES), axis=-1, keepdims=True)
        r = jnp.exp(l2 - l1)
        w0 = p_top / (1.0 + r)
        w1 = p_top * r / (1.0 + r)

        oh1 = lane == i1
        oh2 = lane == i2
        oh = jnp.where(oh1 | oh2, 1.0, 0.0)
        base = _dot(tri, oh.astype(BF16)) + carry
        rank0 = jnp.sum(jnp.where(oh1, base, 0.0), axis=-1, keepdims=True)
        rank1 = jnp.sum(jnp.where(oh2, base, 0.0), axis=-1, keepdims=True)
        carry = carry + jnp.sum(oh, axis=0, keepdims=True)

        e0 = (i1 - n_groups).astype(F32)
        e1 = (i2 - n_groups).astype(F32)
        rec = jnp.zeros_like(logits)
        for idx, val in ((R_W0, w0), (R_W1, w1)):
            rec = jnp.where(lane == idx, val, rec)
        ri_ref[0, rs, :] = rec

        r0_hi = jnp.floor(rank0 * (1.0 / 256.0))
        r1_hi = jnp.floor(rank1 * (1.0 / 256.0))
        ints = jnp.zeros_like(logits)
        for idx, val in ((T_E0, e0), (T_E1, e1), (T_R0_HI, r0_hi), (T_R0_LO, rank0 - 256.0 * r0_hi),
                         (T_R1_HI, r1_hi), (T_R1_LO, rank1 - 256.0 * r1_hi)):
            ints = jnp.where(lane == idx, val, ints)
        rt = lax.dot_general(sel, ints.astype(BF16), (((1,), (1,)), ((), ())), preferred_element_type=F32)
        rt_ref[:, rs] = rt.astype(jnp.int32)

    carry_ref[...] = carry
    cnt_ref[...] = carry


def _out_route(merged, x, mod, g_post1, g_pre2, w_o_b, w_r, b_r, n_groups, per_group):
    bsz, l, d = x.shape
    tm = _tile(l, 512)
    tok = lambda b, i: (b, i, 0)
    const2 = lambda b, i: (0, 0)
    return pl.pallas_call(
        functools.partial(_out_kernel, n_groups=n_groups, per_group=per_group),
        grid=(bsz, l // tm),
        in_specs=[pl.BlockSpec((1, tm, d), tok),
                  pl.BlockSpec((1, tm, d), tok),
                  pl.BlockSpec((1, N_MOD, d), lambda b, i: (b, 0, 0)),
                  pl.BlockSpec((1, d), const2),
                  pl.BlockSpec((1, d), const2),
                  pl.BlockSpec((d, d), const2),
                  pl.BlockSpec((d, LANES), const2),
                  pl.BlockSpec((1, LANES), const2)],
        out_specs=[pl.BlockSpec((1, tm, d), tok),
                   pl.BlockSpec((1, tm, d), tok),
                   pl.BlockSpec((1, tm, LANES), tok),
                   pl.BlockSpec((SUBLANES, tm), lambda b, i: (0, b * (l // tm) + i)),
                   pl.BlockSpec((1, LANES), const2)],
        out_shape=[jax.ShapeDtypeStruct((bsz, l, d), F32),
                   jax.ShapeDtypeStruct((bsz, l, d), F32),
                   jax.ShapeDtypeStruct((bsz, l, LANES), F32),
                   jax.ShapeDtypeStruct((SUBLANES, bsz * l), jnp.int32),
                   jax.ShapeDtypeStruct((1, LANES), F32)],
        scratch_shapes=[pltpu.VMEM((1, LANES), F32)],
        compiler_params=_params("arbitrary", "arbitrary"),
        name="out_route",
    )(merged, x, mod, g_post1.reshape(1, d), g_pre2.reshape(1, d), w_o_b, w_r, b_r)


def _dispatch_kernel(pos_ref, zf_ref, h_ref, xs_ref, zbuf, stage, sem, zsem, *, tm, n_c, n_tiles, n_tok, n_steps):
    i = pl.program_id(0)
    tile_rows = EXPERT_TILE * n_c
    prev = (i + 1) % 2

    def zero_copy(t):
        return pltpu.make_async_copy(zbuf, xs_ref.at[pl.ds(pl.multiple_of(t * tile_rows, tile_rows), tile_rows)], zsem)

    @pl.when(i == 0)
    def _():
        zbuf[...] = jnp.zeros_like(zbuf)

        def issue(t, c):
            @pl.when(zf_ref[t] != 0)
            def _():
                zero_copy(t).start()
            return c

        def drain(t, c):
            @pl.when(zf_ref[t] != 0)
            def _():
                zero_copy(t).wait()
            return c

        lax.fori_loop(0, n_tiles, issue, 0)
        lax.fori_loop(0, n_tiles, drain, 0)

    def issue_rows(rb, c):
        for u in range(DMA_UNROLL):
            r = rb * DMA_UNROLL + u
            src = stage.at[prev, pl.ds(pl.multiple_of(r * n_c, n_c), n_c)]
            for k in range(TOP_K):
                p = pos_ref[k * n_tok + (i - 1) * tm + r]
                pltpu.make_async_copy(src, xs_ref.at[pl.ds(pl.multiple_of(p * n_c, n_c), n_c)],
                                      sem).start(priority=k % 2)
        return c

    @pl.when(i >= 1)
    def _():
        lax.fori_loop(0, tm // DMA_UNROLL, issue_rows, 0)

    @pl.when(i < n_steps)
    def _():
        _store_row_tiled(stage.at[i % 2], 0, h_ref[...])

    @pl.when(i >= 1)
    def _():
        for k in range(TOP_K):
            pltpu.make_async_copy(stage.at[prev], xs_ref.at[pl.ds(0, tm * n_c)], sem).wait()


def _dispatch(h2, pos, zflag, n_rows):
    n, d = h2.shape
    n_c = d // LANES
    tm = _tile(n, 256)
    n_steps = n // tm
    n_tiles = n_rows // EXPERT_TILE
    return pl.pallas_call(
        functools.partial(_dispatch_kernel, tm=tm, n_c=n_c, n_tiles=n_tiles, n_tok=n, n_steps=n_steps),
        grid_spec=pltpu.PrefetchScalarGridSpec(
            num_scalar_prefetch=2,
            grid=(n_steps + 1,),
            in_specs=[pl.BlockSpec((tm, d), lambda i, p, z: (jnp.minimum(i, n_steps - 1), 0))],
            out_specs=pl.BlockSpec(memory_space=pl.ANY),
            scratch_shapes=[pltpu.VMEM((EXPERT_TILE * n_c, LANES), F32),
                            pltpu.VMEM((2, tm * n_c, LANES), F32),
                            pltpu.SemaphoreType.DMA(()),
                            pltpu.SemaphoreType.DMA(())]),
        out_shape=jax.ShapeDtypeStruct((n_rows * n_c, LANES), F32),
        compiler_params=_params("arbitrary"),
        name="dispatch",
    )(pos, zflag, h2)


def _cast_rows(src_ref, dst_ref, chunk=256):
    rows = dst_ref.shape[0]
    chunk = min(chunk, rows)

    def body(c, carry):
        r0 = pl.multiple_of(c * chunk, chunk)
        dst_ref[pl.ds(r0, chunk), :] = src_ref[pl.ds(r0, chunk), :].astype(BF16)
        return carry

    lax.fori_loop(0, rows // chunk, body, 0)


def _experts_kernel(te_ref, chg_ref, nxt_ref, nt_ref, xs_ref, wg_hbm, wu_hbm, wd_hbm, o_ref,
                    wf_g, wf_u, wf_d, wb_g, wb_u, wb_d, sem):
    t = pl.program_id(0)
    stages = ((wg_hbm, wf_g, wb_g), (wu_hbm, wf_u, wb_u), (wd_hbm, wf_d, wb_d))

    def fetch(e):
        return [pltpu.make_async_copy(src.at[e], dst, sem) for src, dst, _ in stages]

    @pl.when(t < nt_ref[0])
    def _():
        @pl.when(chg_ref[t] != 0)
        def _():
            @pl.when(t == 0)
            def _():
                for cp in fetch(te_ref[0]):
                    cp.start(priority=1)

            for cp in fetch(te_ref[t]):
                cp.wait()
            for _, wf, wb in stages:
                _cast_rows(wf, wb)

            @pl.when(nxt_ref[t] >= 0)
            def _():
                for cp in fetch(nxt_ref[t]):
                    cp.start(priority=1)

        x = _load_row_tiled(xs_ref, EXPERT_TILE, wb_g.shape[0] // LANES).astype(BF16)
        g = _dot(x, wb_g[...])
        u = _dot(x, wb_u[...])
        hid = (g * jax.nn.sigmoid(g) * u).astype(BF16)
        o_ref[...] = _dot(hid, wb_d[...])

    @pl.when(t >= nt_ref[0])
    def _():
        o_ref[...] = jnp.zeros_like(o_ref)


def _experts(xs, w_gate, w_up, w_down, plan):
    d, de = w_gate.shape[1], w_gate.shape[2]
    n_c = d // LANES
    n_rows = xs.shape[0] // n_c
    tile = lambda t, *_: (t, 0)
    hbm = pl.BlockSpec(memory_space=pl.ANY)
    return pl.pallas_call(
        _experts_kernel,
        grid_spec=pltpu.PrefetchScalarGridSpec(
            num_scalar_prefetch=len(plan),
            grid=(n_rows // EXPERT_TILE,),
            in_specs=[pl.BlockSpec((EXPERT_TILE * n_c, LANES), tile), hbm, hbm, hbm],
            out_specs=pl.BlockSpec((EXPERT_TILE, d), tile),
            scratch_shapes=[pltpu.VMEM((d, de), F32), pltpu.VMEM((d, de), F32), pltpu.VMEM((de, d), F32),
                            pltpu.VMEM((d, de), BF16), pltpu.VMEM((d, de), BF16), pltpu.VMEM((de, d), BF16),
                            pltpu.SemaphoreType.DMA(())]),
        out_shape=jax.ShapeDtypeStruct((n_rows, d), F32),
        compiler_params=_params("arbitrary"),
        name="experts",
    )(*plan, xs, w_gate, w_up, w_down)


def _combine_kernel(pos_ref, x1_ref, ri_ref, mod_ref, gp_ref, ys_ref, o_ref, buf, sem, *, tm, n_l, n_tok):
    step = pl.program_id(0) * n_l + pl.program_id(1)
    n_steps = pl.num_programs(0) * n_l
    slot = step % 2

    def issue(s, sl):
        def body(rb, c):
            for u in range(DMA_UNROLL):
                r = rb * DMA_UNROLL + u
                for k in range(TOP_K):
                    p = pos_ref[k * n_tok + s * tm + r]
                    pltpu.make_async_copy(ys_ref.at[pl.ds(p, 1)], buf.at[sl, k, rb, pl.ds(u, 1)],
                                          sem.at[sl]).start(priority=k % 2)
            return c

        lax.fori_loop(0, tm // DMA_UNROLL, body, 0)

    @pl.when(step == 0)
    def _():
        issue(0, 0)

    @pl.when(step + 1 < n_steps)
    def _():
        issue(step + 1, 1 - slot)

    for k in range(TOP_K):
        pltpu.make_async_copy(buf.at[slot, k], buf.at[slot, k], sem.at[slot]).wait()

    ri = ri_ref[0]
    d = o_ref.shape[-1]
    moe = (ri[:, R_W0:R_W0 + 1] * buf[slot, 0].reshape(tm, d)
           + ri[:, R_W1:R_W1 + 1] * buf[slot, 1].reshape(tm, d))
    o_ref[0] = x1_ref[0] + mod_ref[0, 5:6, :] * _rms(moe, gp_ref[...])


def _combine(x1, rinfo, mod, g_post2, ys, pos):
    bsz, l, d = x1.shape
    tm = _tile(l, 256)
    n_l = l // tm
    tok = lambda b, i, p: (b, i, 0)
    return pl.pallas_call(
        functools.partial(_combine_kernel, tm=tm, n_l=n_l, n_tok=bsz * l),
        grid_spec=pltpu.PrefetchScalarGridSpec(
            num_scalar_prefetch=1,
            grid=(bsz, n_l),
            in_specs=[pl.BlockSpec((1, tm, d), tok),
                      pl.BlockSpec((1, tm, LANES), tok),
                      pl.BlockSpec((1, N_MOD, d), lambda b, i, p: (b, 0, 0)),
                      pl.BlockSpec((1, d), lambda b, i, p: (0, 0)),
                      pl.BlockSpec(memory_space=pl.ANY)],
            out_specs=pl.BlockSpec((1, tm, d), tok),
            scratch_shapes=[pltpu.VMEM((2, TOP_K, tm // DMA_UNROLL, DMA_UNROLL, d), F32),
                            pltpu.SemaphoreType.DMA((2,))]),
        out_shape=jax.ShapeDtypeStruct((bsz, l, d), F32),
        compiler_params=_params("arbitrary", "arbitrary"),
        name="combine",
    )(pos, x1, rinfo, mod, g_post2.reshape(1, d), ys)


def _rope_tables(seq, head_dim):
    axis_dim = head_dim // 2
    t = jnp.arange(seq, dtype=jnp.int32)
    pos = jnp.stack([t // GRID_W, t % GRID_W], axis=-1).astype(F32)
    inv_freq = ROPE_THETA ** (-jnp.arange(0, axis_dim, 2, dtype=F32) / axis_dim)
    ang = pos[:, :, None] * inv_freq
    cos, sin = jnp.cos(ang), jnp.sin(ang)
    cos_t = jnp.concatenate([cos, cos], axis=-1).reshape(seq, head_dim)
    sin_t = jnp.concatenate([-sin, sin], axis=-1).reshape(seq, head_dim)
    return cos_t, sin_t


def _route_plan(rt, cnt, n_groups, n_experts, n_tiles):
    e = rt[T_E0:T_E1 + 1]
    rank = jnp.stack([rt[T_R0_HI] * 256 + rt[T_R0_LO], rt[T_R1_HI] * 256 + rt[T_R1_LO]])
    counts = cnt[0, n_groups:n_groups + n_experts].astype(jnp.int32)
    tiles_e = (counts + EXPERT_TILE - 1) // EXPERT_TILE
    ids = jnp.arange(n_experts, dtype=jnp.int32)
    tile_end = jnp.sum(jnp.where(ids[None, :] <= ids[:, None], tiles_e[None, :], 0), axis=1)
    tile_start = tile_end - tiles_e
    nt = tile_end[-1]
    row0 = tile_start * EXPERT_TILE
    pos = jnp.sum(jnp.where(e[None] == ids[:, None, None], row0[:, None, None], 0), axis=0) + rank
    t = jnp.arange(n_tiles, dtype=jnp.int32)
    owner = lambda q: jnp.sum((tile_end[None, :] <= q[:, None]).astype(jnp.int32), axis=1)
    te = owner(jnp.minimum(t, nt - 1))
    chg = ((t == 0) | (te != owner(jnp.minimum(jnp.maximum(t - 1, 0), nt - 1)))).astype(jnp.int32)
    partial_last = jnp.any((tile_end[None, :] - 1 == t[:, None]) & (counts[None, :] % EXPERT_TILE != 0), axis=1)
    zflag = ((t >= nt) | partial_last).astype(jnp.int32)
    used = tiles_e > 0
    later = used[None, :] & (ids[None, :] > te[:, None])
    nxt = jnp.min(jnp.where(later, ids[None, :], n_experts), axis=1)
    nxt = jnp.where(nxt == n_experts, -1, nxt)
    return pos.reshape(-1), (te, chg, nxt, nt.reshape(1)), zflag


def kernel(x, c, ctx, c_ctx, w_ada, b_ada, g_pre1, g_post1, g_pre2, g_post2, w_in, q_norm, k_norm,
           gm_ln, w_s, b_s, w_ba, w_bg, w_o, w_rg, b_rg, w_re, b_re, w_gate, w_up, w_down):
    bsz, seq, d = x.shape
    depth = w_ada.shape[0]
    head_dim = q_norm.shape[-1]
    q_w, gm_w = w_ba.shape[1], w_bg.shape[1]
    kv_w = (w_in.shape[2] - q_w - 2 * gm_w - 2 * d) // 2
    n_groups, per_group = w_re.shape[2], w_re.shape[3]
    n_experts = n_groups * per_group
    assert head_dim == LANES and w_s.shape[2] == LANES and gm_w // w_s.shape[1] == LANES
    assert n_groups + n_experts <= LANES and seq % GRID_W == 0
    col_q = 2 * kv_w
    col_u, col_v = col_q + q_w, col_q + q_w + gm_w
    col_ga, col_gg = col_v + gm_w, col_v + gm_w + d
    n_tok = bsz * seq
    n_rows = n_tok * TOP_K + n_experts * EXPERT_TILE
    n_tiles = n_rows // EXPERT_TILE

    cos_t, sin_t = _rope_tables(seq, head_dim)
    pad = (-(bsz + 1)) % (2 * SUBLANES)
    cs = jnp.concatenate([c, c_ctx[None, :], jnp.zeros((pad, d), F32)], axis=0)

    for l in range(depth):
        assert l + 1 == depth, "context-stream update for non-final layers is not implemented"
        mod = _ada(cs, w_ada[l], b_ada[l]).reshape(cs.shape[0], N_MOD, d)
        w_in_b = w_in[l].astype(BF16)
        w_ba_b, w_bg_b, w_o_b = w_ba[l].astype(BF16), w_bg[l].astype(BF16), w_o[l].astype(BF16)
        bs_full = jnp.repeat(b_s[l].T, LANES, axis=1)
        w_r = jnp.concatenate([w_rg[l], w_re[l].reshape(d, n_experts),
                               jnp.zeros((d, LANES - n_groups - n_experts), F32)], axis=1)
        b_r = jnp.concatenate([b_rg[l], b_re[l].reshape(n_experts),
                               jnp.zeros((LANES - n_groups - n_experts,), F32)]).reshape(1, LANES)

        kc, vc = _ctx_kv(ctx, mod, bsz, g_pre1[l], w_in_b, k_norm[l], kv_w)
        hx, kx, vx, qx, gm = _inproj(x, mod, g_pre1[l], w_in_b, k_norm[l], q_norm[l], cos_t, sin_t, gm_ln[l],
                                     w_s[l], bs_full, kv_w, q_w, gm_w, head_dim ** -0.5 * LOG2_E)
        attn = _attention(qx, kc, vc, kx, vx, kv_w // head_dim)
        merged = _merge(hx, attn, gm, w_in_b, w_ba_b, w_bg_b, col_ga, col_gg)

        x1, h2, rinfo, rt, cnt = _out_route(merged, x, mod, g_post1[l], g_pre2[l], w_o_b, w_r, b_r,
                                            n_groups, per_group)
        pos, plan, zflag = _route_plan(rt, cnt, n_groups, n_experts, n_tiles)

        xs = _dispatch(h2.reshape(n_tok, d), pos, zflag, n_rows)
        ys = _experts(xs, w_gate[l], w_up[l], w_down[l], plan)
        x = _combine(x1, rinfo, mod, g_post2[l], ys, pos)
    return x
```

```python
import functools

import jax
import jax.numpy as jnp
from jax import lax
from jax.experimental import pallas as pl
from jax.experimental.pallas import tpu as pltpu

GRID_W = 64
ROPE_THETA = 10000.0
EPS = 1e-6
N_MOD = 6
TOP_K = 2
LOG2_E = 1.4426950408889634

LANES = 128
SUBLANES = 8
VMEM_LIMIT_BYTES = 56 * 1024 * 1024

EXPERT_TILE = 256
DMA_UNROLL = SUBLANES

F32 = jnp.float32
BF16 = jnp.bfloat16


def _params(*sem):
    return pltpu.CompilerParams(dimension_semantics=sem, vmem_limit_bytes=VMEM_LIMIT_BYTES)


def _dot(a, b):
    return jnp.dot(a, b, preferred_element_type=F32)


def _split_bf16(a):
    hi = a.astype(BF16)
    lo = (a - hi.astype(F32)).astype(BF16)
    return hi, lo


def _dot3(a, w):
    a_hi, a_lo = _split_bf16(a)
    w_hi, w_lo = _split_bf16(w)
    return _dot(a_hi, w_hi) + _dot(a_lo, w_hi) + _dot(a_hi, w_lo)


def _rms(x, g):
    return x * lax.rsqrt(jnp.mean(x * x, axis=-1, keepdims=True) + EPS) * g


def _store_row_tiled(ref, t0, val):
    rows, d = val.shape
    n_c = d // LANES
    for c in range(n_c):
        ref[pl.ds(t0 * n_c + c, rows, stride=n_c), :] = val[:, c * LANES:(c + 1) * LANES]


def _load_row_tiled(ref, rows, n_c):
    return jnp.concatenate([ref[pl.ds(c, rows, stride=n_c), :] for c in range(n_c)], axis=1)


def _tile(n, pref):
    t = min(n, pref)
    while n % t:
        t //= 2
    return t


def _ada_kernel(c_ref, w_ref, b_ref, o_ref):
    c = c_ref[...]
    a = c * jax.nn.sigmoid(c)
    o_ref[...] = _dot3(a, w_ref[...]) + b_ref[...]


def _ada(cs, w, b):
    m, d = cs.shape
    n = w.shape[1]
    tn = _tile(n, 1024)
    return pl.pallas_call(
        _ada_kernel,
        grid=(n // tn,),
        in_specs=[pl.BlockSpec((m, d), lambda j: (0, 0)),
                  pl.BlockSpec((d, tn), lambda j: (0, j)),
                  pl.BlockSpec((1, tn), lambda j: (0, j))],
        out_specs=pl.BlockSpec((m, tn), lambda j: (0, j)),
        out_shape=jax.ShapeDtypeStruct((m, n), F32),
        compiler_params=_params("arbitrary"),
        name="ada",
    )(cs, w, b.reshape(1, n))


def _swap32(x):
    lane = lax.broadcasted_iota(jnp.int32, x.shape, 1)
    fwd = pltpu.roll(x, LANES - 32, 1)
    bwd = pltpu.roll(x, 32, 1)
    return jnp.where((lane & 32) == 0, fwd, bwd)


def _norm_head(r, gain, cos, sin, scale):
    y = _rms(r, gain)
    if cos is not None:
        y = y * cos + _swap32(y) * sin
    if scale != 1.0:
        y = y * scale
    return y.astype(BF16)


def _ctx_kv_kernel(x_ref, mod_ref, g_ref, w_ref, kn_ref, k_ref, v_ref, *, n_kv):
    x = x_ref[0]
    inv = lax.rsqrt(jnp.mean(x * x, axis=-1, keepdims=True) + EPS)
    h = ((x * inv) * (g_ref[...] * (1.0 + mod_ref[0, 1:2, :])) + mod_ref[0, 0:1, :]).astype(BF16)
    r = _dot(h, w_ref[...])
    for hh in range(n_kv):
        sl = slice(hh * LANES, (hh + 1) * LANES)
        k_ref[0, :, sl] = _norm_head(r[:, sl], kn_ref[...], None, None, 1.0)
    v_ref[0] = r[:, n_kv * LANES:].astype(BF16)


def _ctx_kv(ctx, mod, mod_row, g_pre, w_in_b, k_norm, kv_w):
    bsz, l, d = ctx.shape
    tm = _tile(l, 512)
    out = jax.ShapeDtypeStruct((bsz, l, kv_w), BF16)
    tok = lambda b, i: (b, i, 0)
    return pl.pallas_call(
        functools.partial(_ctx_kv_kernel, n_kv=kv_w // LANES),
        grid=(bsz, l // tm),
        in_specs=[pl.BlockSpec((1, tm, d), tok),
                  pl.BlockSpec((1, N_MOD, d), lambda b, i: (mod_row, 0, 0)),
                  pl.BlockSpec((1, d), lambda b, i: (0, 0)),
                  pl.BlockSpec((d, 2 * kv_w), lambda b, i: (0, 0)),
                  pl.BlockSpec((1, LANES), lambda b, i: (0, 0))],
        out_specs=[pl.BlockSpec((1, tm, kv_w), tok)] * 2,
        out_shape=[out, out],
        compiler_params=_params("arbitrary", "arbitrary"),
        name="ctx_kv",
    )(ctx, mod, g_pre.reshape(1, d), w_in_b, k_norm.reshape(1, LANES))


def _gelu(x):
    c = 0.7978845608028654
    return x * (0.5 + 0.5 * jnp.tanh(x * (c + (c * 0.044715) * (x * x))))


COL_CHUNK = 512


def _inproj_kernel(x_ref, mod_ref, g_ref, w_ref, kn_ref, qn_ref, cos_ref, sin_ref, ln_ref, ws_ref, bs_ref,
                   h_ref, k_ref, v_ref, q_ref, gm_ref, *, kv_w, q_w, gm_w, scale):
    x = x_ref[0]
    inv = lax.rsqrt(jnp.mean(x * x, axis=-1, keepdims=True) + EPS)
    h = ((x * inv) * (g_ref[...] * (1.0 + mod_ref[0, 1:2, :])) + mod_ref[0, 0:1, :]).astype(BF16)
    h_ref[0] = h
    cos, sin = cos_ref[...], sin_ref[...]
    tm = h.shape[0]

    rk = _dot(h, w_ref[:, 0:kv_w])
    for hh in range(kv_w // LANES):
        sl = slice(hh * LANES, (hh + 1) * LANES)
        k_ref[0, :, sl] = _norm_head(rk[:, sl], kn_ref[...], cos, sin, 1.0)
    v_ref[0] = _dot(h, w_ref[:, kv_w:2 * kv_w]).astype(BF16)

    col_q = 2 * kv_w
    cq = min(COL_CHUNK, q_w)
    for j in range(q_w // cq):
        r = _dot(h, w_ref[:, col_q + j * cq:col_q + (j + 1) * cq])
        for hh in range(cq // LANES):
            sl = slice(hh * LANES, (hh + 1) * LANES)
            q_ref[0, :, j * cq + hh * LANES:j * cq + (hh + 1) * LANES] = _norm_head(
                r[:, sl], qn_ref[...], cos, sin, scale)

    col_u, col_v = col_q + q_w, col_q + q_w + gm_w
    cg = min(COL_CHUNK, gm_w)
    for j in range(gm_w // cg):
        gu = _gelu(_dot(h, w_ref[:, col_u + j * cg:col_u + (j + 1) * cg]))
        gv = _gelu(_dot(h, w_ref[:, col_v + j * cg:col_v + (j + 1) * cg]))
        for g in range(cg // LANES):
            cs = slice(g * LANES, (g + 1) * LANES)
            oc = slice(j * cg + g * LANES, j * cg + (g + 1) * LANES)
            v = gv[:, cs]
            vc = v - jnp.mean(v, axis=-1, keepdims=True)
            vn = vc * lax.rsqrt(jnp.mean(vc * vc, axis=-1, keepdims=True) + EPS) * ln_ref[:, oc]
            vn = vn.astype(BF16)
            w = ws_ref[j * (cg // LANES) + g].astype(BF16)
            for c in range(tm // LANES):
                rs = slice(c * LANES, (c + 1) * LANES)
                s = _dot(w, vn[rs, :]) + bs_ref[:, oc]
                gm_ref[0, rs, oc] = (gu[rs, cs] * s).astype(BF16)


def _inproj(x, mod, g_pre, w_in_b, k_norm, q_norm, cos, sin, gm_ln, w_s, bs_full, kv_w, q_w, gm_w, scale):
    bsz, l, d = x.shape
    tm = _tile(l, 512)
    n_cols = 2 * kv_w + q_w + 2 * gm_w
    assert tm % LANES == 0
    tok = lambda b, i: (b, i, 0)
    c2 = lambda b, i: (0, 0)
    return pl.pallas_call(
        functools.partial(_inproj_kernel, kv_w=kv_w, q_w=q_w, gm_w=gm_w, scale=scale),
        grid=(bsz, l // tm),
        in_specs=[pl.BlockSpec((1, tm, d), tok),
                  pl.BlockSpec((1, N_MOD, d), lambda b, i: (b, 0, 0)),
                  pl.BlockSpec((1, d), c2),
                  pl.BlockSpec((d, n_cols), c2, pipeline_mode=pl.Buffered(1)),
                  pl.BlockSpec((1, LANES), c2),
                  pl.BlockSpec((1, LANES), c2),
                  pl.BlockSpec((tm, LANES), lambda b, i: (i, 0)),
                  pl.BlockSpec((tm, LANES), lambda b, i: (i, 0)),
                  pl.BlockSpec((1, gm_w), c2),
                  pl.BlockSpec(w_s.shape, lambda b, i: (0, 0, 0)),
                  pl.BlockSpec(bs_full.shape, c2)],
        out_specs=[pl.BlockSpec((1, tm, d), tok),
                   pl.BlockSpec((1, tm, kv_w), tok),
                   pl.BlockSpec((1, tm, kv_w), tok),
                   pl.BlockSpec((1, tm, q_w), tok),
                   pl.BlockSpec((1, tm, gm_w), tok)],
        out_shape=[jax.ShapeDtypeStruct((bsz, l, d), BF16),
                   jax.ShapeDtypeStruct((bsz, l, kv_w), BF16),
                   jax.ShapeDtypeStruct((bsz, l, kv_w), BF16),
                   jax.ShapeDtypeStruct((bsz, l, q_w), BF16),
                   jax.ShapeDtypeStruct((bsz, l, gm_w), BF16)],
        compiler_params=_params("arbitrary", "arbitrary"),
        name="inproj",
    )(x, mod, g_pre.reshape(1, d), w_in_b, k_norm.reshape(1, LANES), q_norm.reshape(1, LANES), cos, sin,
      gm_ln.reshape(1, gm_w), w_s, bs_full)


def _attn_kernel(q_ref, kc_ref, vc_ref, kx_ref, vx_ref, o_ref, *, n_kv, grp):
    nt = (((1,), (1,)), ((), ()))
    for kv in range(n_kv):
        ks = slice(kv * LANES, (kv + 1) * LANES)
        kc, kx = kc_ref[0, :, ks], kx_ref[0, :, ks]
        vc = jnp.concatenate([vc_ref[0, :, ks], jnp.ones_like(kc)], axis=1)
        vx = jnp.concatenate([vx_ref[0, :, ks], jnp.ones_like(kx)], axis=1)
        for hh in range(grp):
            sl = slice((kv * grp + hh) * LANES, (kv * grp + hh + 1) * LANES)
            q = q_ref[0, :, sl]
            sc = lax.dot_general(q, kc, nt, preferred_element_type=F32)
            sx = lax.dot_general(q, kx, nt, preferred_element_type=F32)
            m = jnp.maximum(jnp.max(sc, axis=-1, keepdims=True), jnp.max(sx, axis=-1, keepdims=True))
            pc = jnp.exp2(sc - m).astype(BF16)
            px = jnp.exp2(sx - m).astype(BF16)
            o = _dot(pc, vc) + _dot(px, vx)
            o_ref[0, :, sl] = (o[:, :LANES] / o[:, LANES:LANES + 1]).astype(BF16)


def _attention(q, kc, vc, kx, vx, n_kv):
    bsz, l, q_w = q.shape
    lc, kv_w = kc.shape[1], kc.shape[2]
    tq = _tile(l, 1024)
    tok = lambda b, i: (b, i, 0)
    whole = lambda b, i: (b, 0, 0)
    return pl.pallas_call(
        functools.partial(_attn_kernel, n_kv=n_kv, grp=q_w // kv_w),
        grid=(bsz, l // tq),
        in_specs=[pl.BlockSpec((1, tq, q_w), tok),
                  pl.BlockSpec((1, lc, kv_w), whole),
                  pl.BlockSpec((1, lc, kv_w), whole),
                  pl.BlockSpec((1, l, kv_w), whole),
                  pl.BlockSpec((1, l, kv_w), whole)],
        out_specs=pl.BlockSpec((1, tq, q_w), tok),
        out_shape=jax.ShapeDtypeStruct((bsz, l, q_w), BF16),
        compiler_params=_params("arbitrary", "arbitrary"),
        name="attention",
    )(q, kc, vc, kx, vx)


def _merge_kernel(h_ref, a_ref, g_ref, *refs, n_chunk):
    wga, wgg = refs[:n_chunk], refs[n_chunk:2 * n_chunk]
    wba_ref, wbg_ref, o_ref = refs[2 * n_chunk:]
    h, a, g = h_ref[0], a_ref[0], g_ref[0]
    tn = wga[0].shape[1]
    for j in range(n_chunk):
        cs = slice(j * tn, (j + 1) * tn)
        ga = jax.nn.sigmoid(_dot(h, wga[j][...]))
        gg = jax.nn.sigmoid(_dot(h, wgg[j][...]))
        o_ref[0, :, cs] = (ga * _dot(a, wba_ref[:, cs]) + gg * _dot(g, wbg_ref[:, cs])).astype(BF16)


def _merge(h, attn, gm, w_in_b, w_ba_b, w_bg_b, col_ga, col_gg):
    bsz, l, d = h.shape
    q_w, gm_w = attn.shape[2], gm.shape[2]
    tm = _tile(l, 512)
    tn = _tile(d, COL_CHUNK)
    assert col_ga % tn == 0 and col_gg % tn == 0
    n_chunk = d // tn
    tok = lambda b, i: (b, i, 0)
    gate_specs = [pl.BlockSpec((d, tn), functools.partial(lambda b, i, c: (0, c), c=(c0 // tn) + j),
                               pipeline_mode=pl.Buffered(1))
                  for c0 in (col_ga, col_gg) for j in range(n_chunk)]
    return pl.pallas_call(
        functools.partial(_merge_kernel, n_chunk=n_chunk),
        grid=(bsz, l // tm),
        in_specs=[pl.BlockSpec((1, tm, d), tok),
                  pl.BlockSpec((1, tm, q_w), tok),
                  pl.BlockSpec((1, tm, gm_w), tok),
                  *gate_specs,
                  pl.BlockSpec((q_w, d), lambda b, i: (0, 0)),
                  pl.BlockSpec((gm_w, d), lambda b, i: (0, 0))],
        out_specs=pl.BlockSpec((1, tm, d), tok),
        out_shape=jax.ShapeDtypeStruct((bsz, l, d), BF16),
        compiler_params=_params("arbitrary", "arbitrary"),
        name="merge",
    )(h, attn, gm, *([w_in_b] * (2 * n_chunk)), w_ba_b, w_bg_b)


R_W0, R_W1 = range(2)
T_E0, T_E1, T_R0_HI, T_R0_LO, T_R1_HI, T_R1_LO = range(6)


ROUTE_SUB = 512


def _out_kernel(m_ref, x_ref, mod_ref, gp1_ref, gp2_ref, wo_ref, wr_ref, br_ref,
                x1_ref, h2_ref, ri_ref, rt_ref, cnt_ref, carry_ref, *, n_groups, per_group):
    @pl.when((pl.program_id(0) == 0) & (pl.program_id(1) == 0))
    def _():
        carry_ref[...] = jnp.zeros_like(carry_ref)

    tm = m_ref.shape[1]
    sub = min(ROUTE_SUB, tm)
    wr_hi, wr_lo = _split_bf16(wr_ref[...])
    wr_both = jnp.concatenate([wr_hi, wr_lo], axis=1)
    lane = lax.broadcasted_iota(jnp.int32, (sub, LANES), 1)
    row = lax.broadcasted_iota(jnp.int32, (sub, sub), 0)
    col = lax.broadcasted_iota(jnp.int32, (sub, sub), 1)
    tri = jnp.where(col < row, 1.0, 0.0).astype(BF16)
    sel = jnp.where(lax.broadcasted_iota(jnp.int32, (SUBLANES, LANES), 0)
                    == lax.broadcasted_iota(jnp.int32, (SUBLANES, LANES), 1), 1.0, 0.0).astype(BF16)
    neg = jnp.float32(-jnp.inf)
    carry = carry_ref[...]

    for sb in range(tm // sub):
        rs = slice(sb * sub, (sb + 1) * sub)
        mix = _dot(m_ref[0, rs, :], wo_ref[...])
        inv1 = lax.rsqrt(jnp.mean(mix * mix, axis=-1, keepdims=True) + EPS)
        x1 = x_ref[0, rs, :] + (mix * inv1) * (mod_ref[0, 2:3, :] * gp1_ref[...])
        x1_ref[0, rs, :] = x1
        inv2 = lax.rsqrt(jnp.mean(x1 * x1, axis=-1, keepdims=True) + EPS)
        h2 = (x1 * inv2) * (gp2_ref[...] * (1.0 + mod_ref[0, 4:5, :])) + mod_ref[0, 3:4, :]
        h2_ref[0, rs, :] = h2

        h_hi, h_lo = _split_bf16(h2)
        both = _dot(h_hi, wr_both)
        logits = both[:, :LANES] + both[:, LANES:] + _dot(h_lo, wr_hi) + br_ref[...]
        lg = jnp.where(lane < n_groups, logits, neg)
        gmax = jnp.max(lg, axis=-1, keepdims=True)
        p_top = 1.0 / jnp.sum(jnp.exp(lg - gmax), axis=-1, keepdims=True)
        gidx = jnp.min(jnp.where(lg == gmax, lane, LANES), axis=-1, keepdims=True)
        lo = n_groups + gidx * per_group
        le = jnp.where((lane >= lo) & (lane < lo + per_group), logits, neg)
        l1 = jnp.max(le, axis=-1, keepdims=True)
        i1 = jnp.min(jnp.where(le == l1, lane, LANES), axis=-1, keepdims=True)
        le2 = jnp.where(lane == i1, neg, le)
        l2 = jnp.max(le2, axis=-1, keepdims=True)
        i2 = jnp.min(jnp.where(le2 == l2, lane, LANES), axis=-1, keepdims=True)
        r = jnp.exp(l2 - l1)
        w0 = p_top / (1.0 + r)
        w1 = p_top * r / (1.0 + r)

        oh1 = lane == i1
        oh2 = lane == i2
        oh = jnp.where(oh1 | oh2, 1.0, 0.0)
        base = _dot(tri, oh.astype(BF16)) + carry
        rank0 = jnp.sum(jnp.where(oh1, base, 0.0), axis=-1, keepdims=True)
        rank1 = jnp.sum(jnp.where(oh2, base, 0.0), axis=-1, keepdims=True)
        carry = carry + jnp.sum(oh, axis=0, keepdims=True)

        e0 = (i1 - n_groups).astype(F32)
        e1 = (i2 - n_groups).astype(F32)
        rec = jnp.zeros_like(logits)
        for idx, val in ((R_W0, w0), (R_W1, w1)):
            rec = jnp.where(lane == idx, val, rec)
        ri_ref[0, rs, :] = rec

        r0_hi = jnp.floor(rank0 * (1.0 / 256.0))
        r1_hi = jnp.floor(rank1 * (1.0 / 256.0))
        ints = jnp.zeros_like(logits)
        for idx, val in ((T_E0, e0), (T_E1, e1), (T_R0_HI, r0_hi), (T_R0_LO, rank0 - 256.0 * r0_hi),
                         (T_R1_HI, r1_hi), (T_R1_LO, rank1 - 256.0 * r1_hi)):
            ints = jnp.where(lane == idx, val, ints)
        rt = lax.dot_general(sel, ints.astype(BF16), (((1,), (1,)), ((), ())), preferred_element_type=F32)
        rt_ref[:, rs] = rt.astype(jnp.int32)

    carry_ref[...] = carry
    cnt_ref[...] = carry


def _out_route(merged, x, mod, g_post1, g_pre2, w_o_b, w_r, b_r, n_groups, per_group):
    bsz, l, d = x.shape
    tm = _tile(l, 512)
    tok = lambda b, i: (b, i, 0)
    const2 = lambda b, i: (0, 0)
    return pl.pallas_call(
        functools.partial(_out_kernel, n_groups=n_groups, per_group=per_group),
        grid=(bsz, l // tm),
        in_specs=[pl.BlockSpec((1, tm, d), tok),
                  pl.BlockSpec((1, tm, d), tok),
                  pl.BlockSpec((1, N_MOD, d), lambda b, i: (b, 0, 0)),
                  pl.BlockSpec((1, d), const2),
                  pl.BlockSpec((1, d), const2),
                  pl.BlockSpec((d, d), const2),
                  pl.BlockSpec((d, LANES), const2),
                  pl.BlockSpec((1, LANES), const2)],
        out_specs=[pl.BlockSpec((1, tm, d), tok),
                   pl.BlockSpec((1, tm, d), tok),
                   pl.BlockSpec((1, tm, LANES), tok),
                   pl.BlockSpec((SUBLANES, tm), lambda b, i: (0, b * (l // tm) + i)),
                   pl.BlockSpec((1, LANES), const2)],
        out_shape=[jax.ShapeDtypeStruct((bsz, l, d), F32),
                   jax.ShapeDtypeStruct((bsz, l, d), F32),
                   jax.ShapeDtypeStruct((bsz, l, LANES), F32),
                   jax.ShapeDtypeStruct((SUBLANES, bsz * l), jnp.int32),
                   jax.ShapeDtypeStruct((1, LANES), F32)],
        scratch_shapes=[pltpu.VMEM((1, LANES), F32)],
        compiler_params=_params("arbitrary", "arbitrary"),
        name="out_route",
    )(merged, x, mod, g_post1.reshape(1, d), g_pre2.reshape(1, d), w_o_b, w_r, b_r)


def _dispatch_kernel(pos_ref, zf_ref, h_ref, xs_ref, zbuf, stage, sem, zsem, *, tm, n_c, n_tiles, n_tok, n_steps):
    i = pl.program_id(0)
    tile_rows = EXPERT_TILE * n_c
    prev = (i + 1) % 2

    def zero_copy(t):
        return pltpu.make_async_copy(zbuf, xs_ref.at[pl.ds(pl.multiple_of(t * tile_rows, tile_rows), tile_rows)], zsem)

    @pl.when(i == 0)
    def _():
        zbuf[...] = jnp.zeros_like(zbuf)

        def issue(t, c):
            @pl.when(zf_ref[t] != 0)
            def _():
                zero_copy(t).start()
            return c

        def drain(t, c):
            @pl.when(zf_ref[t] != 0)
            def _():
                zero_copy(t).wait()
            return c

        lax.fori_loop(0, n_tiles, issue, 0)
        lax.fori_loop(0, n_tiles, drain, 0)

    def issue_rows(rb, c):
        for u in range(DMA_UNROLL):
            r = rb * DMA_UNROLL + u
            src = stage.at[prev, pl.ds(pl.multiple_of(r * n_c, n_c), n_c)]
            for k in range(TOP_K):
                p = pos_ref[k * n_tok + (i - 1) * tm + r]
                pltpu.make_async_copy(src, xs_ref.at[pl.ds(pl.multiple_of(p * n_c, n_c), n_c)],
                                      sem).start(priority=k % 2)
        return c

    @pl.when(i >= 1)
    def _():
        lax.fori_loop(0, tm // DMA_UNROLL, issue_rows, 0)

    @pl.when(i < n_steps)
    def _():
        _store_row_tiled(stage.at[i % 2], 0, h_ref[...])

    @pl.when(i >= 1)
    def _():
        for k in range(TOP_K):
            pltpu.make_async_copy(stage.at[prev], xs_ref.at[pl.ds(0, tm * n_c)], sem).wait()


def _dispatch(h2, pos, zflag, n_rows):
    n, d = h2.shape
    n_c = d // LANES
    tm = _tile(n, 512)
    n_steps = n // tm
    n_tiles = n_rows // EXPERT_TILE
    return pl.pallas_call(
        functools.partial(_dispatch_kernel, tm=tm, n_c=n_c, n_tiles=n_tiles, n_tok=n, n_steps=n_steps),
        grid_spec=pltpu.PrefetchScalarGridSpec(
            num_scalar_prefetch=2,
            grid=(n_steps + 1,),
            in_specs=[pl.BlockSpec((tm, d), lambda i, p, z: (jnp.minimum(i, n_steps - 1), 0))],
            out_specs=pl.BlockSpec(memory_space=pl.ANY),
            scratch_shapes=[pltpu.VMEM((EXPERT_TILE * n_c, LANES), F32),
                            pltpu.VMEM((2, tm * n_c, LANES), F32),
                            pltpu.SemaphoreType.DMA(()),
                            pltpu.SemaphoreType.DMA(())]),
        out_shape=jax.ShapeDtypeStruct((n_rows * n_c, LANES), F32),
        compiler_params=_params("arbitrary"),
        name="dispatch",
    )(pos, zflag, h2)


def _cast_rows(src_ref, dst_ref, chunk=256):
    rows = dst_ref.shape[0]
    chunk = min(chunk, rows)

    def body(c, carry):
        r0 = pl.multiple_of(c * chunk, chunk)
        dst_ref[pl.ds(r0, chunk), :] = src_ref[pl.ds(r0, chunk), :].astype(BF16)
        return carry

    lax.fori_loop(0, rows // chunk, body, 0)


def _experts_kernel(te_ref, chg_ref, nxt_ref, nt_ref, xs_ref, wg_hbm, wu_hbm, wd_hbm, o_ref,
                    wf_g, wf_u, wf_d, wb_g, wb_u, wb_d, sem):
    t = pl.program_id(0)
    stages = ((wg_hbm, wf_g, wb_g), (wu_hbm, wf_u, wb_u), (wd_hbm, wf_d, wb_d))

    def fetch(e):
        return [pltpu.make_async_copy(src.at[e], dst, sem) for src, dst, _ in stages]

    @pl.when(t < nt_ref[0])
    def _():
        @pl.when(chg_ref[t] != 0)
        def _():
            @pl.when(t == 0)
            def _():
                for cp in fetch(te_ref[0]):
                    cp.start(priority=1)

            for cp in fetch(te_ref[t]):
                cp.wait()
            for _, wf, wb in stages:
                _cast_rows(wf, wb)

            @pl.when(nxt_ref[t] >= 0)
            def _():
                for cp in fetch(nxt_ref[t]):
                    cp.start(priority=1)

        x = _load_row_tiled(xs_ref, EXPERT_TILE, wb_g.shape[0] // LANES).astype(BF16)
        g = _dot(x, wb_g[...])
        u = _dot(x, wb_u[...])
        hid = (g * jax.nn.sigmoid(g) * u).astype(BF16)
        o_ref[...] = _dot(hid, wb_d[...])

    @pl.when(t >= nt_ref[0])
    def _():
        o_ref[...] = jnp.zeros_like(o_ref)


def _experts(xs, w_gate, w_up, w_down, plan):
    d, de = w_gate.shape[1], w_gate.shape[2]
    n_c = d // LANES
    n_rows = xs.shape[0] // n_c
    tile = lambda t, *_: (t, 0)
    hbm = pl.BlockSpec(memory_space=pl.ANY)
    return pl.pallas_call(
        _experts_kernel,
        grid_spec=pltpu.PrefetchScalarGridSpec(
            num_scalar_prefetch=len(plan),
            grid=(n_rows // EXPERT_TILE,),
            in_specs=[pl.BlockSpec((EXPERT_TILE * n_c, LANES), tile), hbm, hbm, hbm],
            out_specs=pl.BlockSpec((EXPERT_TILE, d), tile),
            scratch_shapes=[pltpu.VMEM((d, de), F32), pltpu.VMEM((d, de), F32), pltpu.VMEM((de, d), F32),
                            pltpu.VMEM((d, de), BF16), pltpu.VMEM((d, de), BF16), pltpu.VMEM((de, d), BF16),
                            pltpu.SemaphoreType.DMA(())]),
        out_shape=jax.ShapeDtypeStruct((n_rows, d), F32),
        compiler_params=_params("arbitrary"),
        name="experts",
    )(*plan, xs, w_gate, w_up, w_down)


def _combine_kernel(pos_ref, x1_ref, ri_ref, mod_ref, gp_ref, ys_ref, o_ref, buf, sem, *, tm, n_l, n_tok):
    step = pl.program_id(0) * n_l + pl.program_id(1)
    n_steps = pl.num_programs(0) * n_l
    slot = step % 2

    def issue(s, sl):
        def body(rb, c):
            for u in range(DMA_UNROLL):
                r = rb * DMA_UNROLL + u
                for k in range(TOP_K):
                    p = pos_ref[k * n_tok + s * tm + r]
                    pltpu.make_async_copy(ys_ref.at[pl.ds(p, 1)], buf.at[sl, k, rb, pl.ds(u, 1)],
                                          sem.at[sl]).start(priority=k % 2)
            return c

        lax.fori_loop(0, tm // DMA_UNROLL, body, 0)

    @pl.when(step == 0)
    def _():
        issue(0, 0)

    @pl.when(step + 1 < n_steps)
    def _():
        issue(step + 1, 1 - slot)

    for k in range(TOP_K):
        pltpu.make_async_copy(buf.at[slot, k], buf.at[slot, k], sem.at[slot]).wait()

    ri = ri_ref[0]
    d = o_ref.shape[-1]
    moe = (ri[:, R_W0:R_W0 + 1] * buf[slot, 0].reshape(tm, d)
           + ri[:, R_W1:R_W1 + 1] * buf[slot, 1].reshape(tm, d))
    o_ref[0] = x1_ref[0] + mod_ref[0, 5:6, :] * _rms(moe, gp_ref[...])


def _combine(x1, rinfo, mod, g_post2, ys, pos):
    bsz, l, d = x1.shape
    tm = _tile(l, 512)
    n_l = l // tm
    tok = lambda b, i, p: (b, i, 0)
    return pl.pallas_call(
        functools.partial(_combine_kernel, tm=tm, n_l=n_l, n_tok=bsz * l),
        grid_spec=pltpu.PrefetchScalarGridSpec(
            num_scalar_prefetch=1,
            grid=(bsz, n_l),
            in_specs=[pl.BlockSpec((1, tm, d), tok),
                      pl.BlockSpec((1, tm, LANES), tok),
                      pl.BlockSpec((1, N_MOD, d), lambda b, i, p: (b, 0, 0)),
                      pl.BlockSpec((1, d), lambda b, i, p: (0, 0)),
                      pl.BlockSpec(memory_space=pl.ANY)],
            out_specs=pl.BlockSpec((1, tm, d), tok),
            scratch_shapes=[pltpu.VMEM((2, TOP_K, tm // DMA_UNROLL, DMA_UNROLL, d), F32),
                            pltpu.SemaphoreType.DMA((2,))]),
        out_shape=jax.ShapeDtypeStruct((bsz, l, d), F32),
        compiler_params=_params("arbitrary", "arbitrary"),
        name="combine",
    )(pos, x1, rinfo, mod, g_post2.reshape(1, d), ys)


def _rope_tables(seq, head_dim):
    axis_dim = head_dim // 2
    t = jnp.arange(seq, dtype=jnp.int32)
    pos = jnp.stack([t // GRID_W, t % GRID_W], axis=-1).astype(F32)
    inv_freq = ROPE_THETA ** (-jnp.arange(0, axis_dim, 2, dtype=F32) / axis_dim)
    ang = pos[:, :, None] * inv_freq
    cos, sin = jnp.cos(ang), jnp.sin(ang)
    cos_t = jnp.concatenate([cos, cos], axis=-1).reshape(seq, head_dim)
    sin_t = jnp.concatenate([-sin, sin], axis=-1).reshape(seq, head_dim)
    return cos_t, sin_t


def _route_plan(rt, cnt, n_groups, n_experts, n_tiles):
    e = rt[T_E0:T_E1 + 1]
    rank = jnp.stack([rt[T_R0_HI] * 256 + rt[T_R0_LO], rt[T_R1_HI] * 256 + rt[T_R1_LO]])
    counts = cnt[0, n_groups:n_groups + n_experts].astype(jnp.int32)
    tiles_e = (counts + EXPERT_TILE - 1) // EXPERT_TILE
    ids = jnp.arange(n_experts, dtype=jnp.int32)
    tile_end = jnp.sum(jnp.where(ids[None, :] <= ids[:, None], tiles_e[None, :], 0), axis=1)
    tile_start = tile_end - tiles_e
    nt = tile_end[-1]
    row0 = tile_start * EXPERT_TILE
    pos = jnp.sum(jnp.where(e[None] == ids[:, None, None], row0[:, None, None], 0), axis=0) + rank
    t = jnp.arange(n_tiles, dtype=jnp.int32)
    owner = lambda q: jnp.sum((tile_end[None, :] <= q[:, None]).astype(jnp.int32), axis=1)
    te = owner(jnp.minimum(t, nt - 1))
    chg = ((t == 0) | (te != owner(jnp.minimum(jnp.maximum(t - 1, 0), nt - 1)))).astype(jnp.int32)
    partial_last = jnp.any((tile_end[None, :] - 1 == t[:, None]) & (counts[None, :] % EXPERT_TILE != 0), axis=1)
    zflag = ((t >= nt) | partial_last).astype(jnp.int32)
    used = tiles_e > 0
    later = used[None, :] & (ids[None, :] > te[:, None])
    nxt = jnp.min(jnp.where(later, ids[None, :], n_experts), axis=1)
    nxt = jnp.where(nxt == n_experts, -1, nxt)
    return pos.reshape(-1), (te, chg, nxt, nt.reshape(1)), zflag


def kernel(x, c, ctx, c_ctx, w_ada, b_ada, g_pre1, g_post1, g_pre2, g_post2, w_in, q_norm, k_norm,
           gm_ln, w_s, b_s, w_ba, w_bg, w_o, w_rg, b_rg, w_re, b_re, w_gate, w_up, w_down):
    bsz, seq, d = x.shape
    depth = w_ada.shape[0]
    head_dim = q_norm.shape[-1]
    q_w, gm_w = w_ba.shape[1], w_bg.shape[1]
    kv_w = (w_in.shape[2] - q_w - 2 * gm_w - 2 * d) // 2
    n_groups, per_group = w_re.shape[2], w_re.shape[3]
    n_experts = n_groups * per_group
    assert head_dim == LANES and w_s.shape[2] == LANES and gm_w // w_s.shape[1] == LANES
    assert n_groups + n_experts <= LANES and seq % GRID_W == 0
    col_q = 2 * kv_w
    col_u, col_v = col_q + q_w, col_q + q_w + gm_w
    col_ga, col_gg = col_v + gm_w, col_v + gm_w + d
    n_tok = bsz * seq
    n_rows = n_tok * TOP_K + n_experts * EXPERT_TILE
    n_tiles = n_rows // EXPERT_TILE

    cos_t, sin_t = _rope_tables(seq, head_dim)
    pad = (-(bsz + 1)) % (2 * SUBLANES)
    cs = jnp.concatenate([c, c_ctx[None, :], jnp.zeros((pad, d), F32)], axis=0)

    for l in range(depth):
        assert l + 1 == depth, "context-stream update for non-final layers is not implemented"
        mod = _ada(cs, w_ada[l], b_ada[l]).reshape(cs.shape[0], N_MOD, d)
        w_in_b = w_in[l].astype(BF16)
        w_ba_b, w_bg_b, w_o_b = w_ba[l].astype(BF16), w_bg[l].astype(BF16), w_o[l].astype(BF16)
        bs_full = jnp.repeat(b_s[l].T, LANES, axis=1)
        w_r = jnp.concatenate([w_rg[l], w_re[l].reshape(d, n_experts),
                               jnp.zeros((d, LANES - n_groups - n_experts), F32)], axis=1)
        b_r = jnp.concatenate([b_rg[l], b_re[l].reshape(n_experts),
                               jnp.zeros((LANES - n_groups - n_experts,), F32)]).reshape(1, LANES)

        kc, vc = _ctx_kv(ctx, mod, bsz, g_pre1[l], w_in_b, k_norm[l], kv_w)
        hx, kx, vx, qx, gm = _inproj(x, mod, g_pre1[l], w_in_b, k_norm[l], q_norm[l], cos_t, sin_t, gm_ln[l],
                                     w_s[l], bs_full, kv_w, q_w, gm_w, head_dim ** -0.5 * LOG2_E)
        attn = _attention(qx, kc, vc, kx, vx, kv_w // head_dim)
        merged = _merge(hx, attn, gm, w_in_b, w_ba_b, w_bg_b, col_ga, col_gg)

        x1, h2, rinfo, rt, cnt = _out_route(merged, x, mod, g_post1[l], g_pre2[l], w_o_b, w_r, b_r,
                                            n_groups, per_group)
        pos, plan, zflag = _route_plan(rt, cnt, n_groups, n_experts, n_tiles)

        xs = _dispatch(h2.reshape(n_tok, d), pos, zflag, n_rows)
        ys = _experts(xs, w_gate[l], w_up[l], w_down[l], plan)
        x = _combine(x1, rinfo, mod, g_post2[l], ys, pos)
    return x
```

```python
import functools

import jax
import jax.numpy as jnp
from jax import lax
from jax.experimental import pallas as pl
from jax.experimental.pallas import tpu as pltpu

GRID_W = 64
ROPE_THETA = 10000.0
EPS = 1e-6
N_MOD = 6
TOP_K = 2
LOG2_E = 1.4426950408889634

LANES = 128
SUBLANES = 8
VMEM_LIMIT_BYTES = 56 * 1024 * 1024

EXPERT_TILE = 256
TOKEN_TILE = 512
ATTN_Q_TILE = 1024
DISPATCH_TILE = 1024
COMBINE_TILE = 256
ADA_COL_TILE = 1024
COL_CHUNK = 512
DMA_UNROLL = SUBLANES

F32 = jnp.float32
BF16 = jnp.bfloat16


def _params(*sem):
    return pltpu.CompilerParams(dimension_semantics=sem, vmem_limit_bytes=VMEM_LIMIT_BYTES)


def _dot(a, b):
    return jnp.dot(a, b, preferred_element_type=F32)


def _split_bf16(a):
    hi = a.astype(BF16)
    lo = (a - hi.astype(F32)).astype(BF16)
    return hi, lo


def _dot3(a, w):
    a_hi, a_lo = _split_bf16(a)
    w_hi, w_lo = _split_bf16(w)
    return _dot(a_hi, w_hi) + _dot(a_lo, w_hi) + _dot(a_hi, w_lo)


def _rms(x, g):
    return x * lax.rsqrt(jnp.mean(x * x, axis=-1, keepdims=True) + EPS) * g


def _store_row_tiled(ref, t0, val):
    rows, d = val.shape
    n_c = d // LANES
    for c in range(n_c):
        ref[pl.ds(t0 * n_c + c, rows, stride=n_c), :] = val[:, c * LANES:(c + 1) * LANES]


def _load_row_tiled(ref, rows, n_c):
    return jnp.concatenate([ref[pl.ds(c, rows, stride=n_c), :] for c in range(n_c)], axis=1)


def _tile(n, pref):
    t = min(n, pref)
    while n % t:
        t //= 2
    return t


def _ada_kernel(c_ref, w_ref, b_ref, o_ref):
    c = c_ref[...]
    a = c * jax.nn.sigmoid(c)
    o_ref[...] = _dot3(a, w_ref[...]) + b_ref[...]


def _ada(cs, w, b):
    m, d = cs.shape
    n = w.shape[1]
    tn = _tile(n, ADA_COL_TILE)
    return pl.pallas_call(
        _ada_kernel,
        grid=(n // tn,),
        in_specs=[pl.BlockSpec((m, d), lambda j: (0, 0)),
                  pl.BlockSpec((d, tn), lambda j: (0, j)),
                  pl.BlockSpec((1, tn), lambda j: (0, j))],
        out_specs=pl.BlockSpec((m, tn), lambda j: (0, j)),
        out_shape=jax.ShapeDtypeStruct((m, n), F32),
        compiler_params=_params("arbitrary"),
        name="ada",
    )(cs, w, b.reshape(1, n))


def _swap32(x):
    lane = lax.broadcasted_iota(jnp.int32, x.shape, 1)
    fwd = pltpu.roll(x, LANES - 32, 1)
    bwd = pltpu.roll(x, 32, 1)
    return jnp.where((lane & 32) == 0, fwd, bwd)


def _norm_head(r, gain, cos, sin, scale):
    y = _rms(r, gain)
    if cos is not None:
        y = y * cos + _swap32(y) * sin
    if scale != 1.0:
        y = y * scale
    return y.astype(BF16)


def _ctx_kv_kernel(x_ref, mod_ref, g_ref, w_ref, kn_ref, k_ref, v_ref, *, n_kv):
    x = x_ref[0]
    inv = lax.rsqrt(jnp.mean(x * x, axis=-1, keepdims=True) + EPS)
    h = ((x * inv) * (g_ref[...] * (1.0 + mod_ref[0, 1:2, :])) + mod_ref[0, 0:1, :]).astype(BF16)
    r = _dot(h, w_ref[...])
    for hh in range(n_kv):
        sl = slice(hh * LANES, (hh + 1) * LANES)
        k_ref[0, :, sl] = _norm_head(r[:, sl], kn_ref[...], None, None, 1.0)
    v_ref[0] = r[:, n_kv * LANES:].astype(BF16)


def _ctx_kv(ctx, mod, mod_row, g_pre, w_in_b, k_norm, kv_w):
    bsz, l, d = ctx.shape
    tm = _tile(l, TOKEN_TILE)
    out = jax.ShapeDtypeStruct((bsz, l, kv_w), BF16)
    tok = lambda b, i: (b, i, 0)
    return pl.pallas_call(
        functools.partial(_ctx_kv_kernel, n_kv=kv_w // LANES),
        grid=(bsz, l // tm),
        in_specs=[pl.BlockSpec((1, tm, d), tok),
                  pl.BlockSpec((1, N_MOD, d), lambda b, i: (mod_row, 0, 0)),
                  pl.BlockSpec((1, d), lambda b, i: (0, 0)),
                  pl.BlockSpec((d, 2 * kv_w), lambda b, i: (0, 0)),
                  pl.BlockSpec((1, LANES), lambda b, i: (0, 0))],
        out_specs=[pl.BlockSpec((1, tm, kv_w), tok)] * 2,
        out_shape=[out, out],
        compiler_params=_params("arbitrary", "arbitrary"),
        name="ctx_kv",
    )(ctx, mod, g_pre.reshape(1, d), w_in_b, k_norm.reshape(1, LANES))


def _gelu(x):
    c = 0.7978845608028654
    return x * (0.5 + 0.5 * jnp.tanh(x * (c + (c * 0.044715) * (x * x))))


def _inproj_kernel(x_ref, mod_ref, g_ref, w_ref, kn_ref, qn_ref, cos_ref, sin_ref, ln_ref, ws_ref, bs_ref,
                   h_ref, k_ref, v_ref, q_ref, gm_ref, *, kv_w, q_w, gm_w, scale):
    x = x_ref[0]
    inv = lax.rsqrt(jnp.mean(x * x, axis=-1, keepdims=True) + EPS)
    h = ((x * inv) * (g_ref[...] * (1.0 + mod_ref[0, 1:2, :])) + mod_ref[0, 0:1, :]).astype(BF16)
    h_ref[0] = h
    cos, sin = cos_ref[...], sin_ref[...]
    tm = h.shape[0]

    rk = _dot(h, w_ref[:, 0:kv_w])
    for hh in range(kv_w // LANES):
        sl = slice(hh * LANES, (hh + 1) * LANES)
        k_ref[0, :, sl] = _norm_head(rk[:, sl], kn_ref[...], cos, sin, 1.0)
    v_ref[0] = _dot(h, w_ref[:, kv_w:2 * kv_w]).astype(BF16)

    col_q = 2 * kv_w
    cq = min(COL_CHUNK, q_w)
    for j in range(q_w // cq):
        r = _dot(h, w_ref[:, col_q + j * cq:col_q + (j + 1) * cq])
        for hh in range(cq // LANES):
            sl = slice(hh * LANES, (hh + 1) * LANES)
            q_ref[0, :, j * cq + hh * LANES:j * cq + (hh + 1) * LANES] = _norm_head(
                r[:, sl], qn_ref[...], cos, sin, scale)

    col_u, col_v = col_q + q_w, col_q + q_w + gm_w
    cg = min(COL_CHUNK, gm_w)
    for j in range(gm_w // cg):
        gu = _gelu(_dot(h, w_ref[:, col_u + j * cg:col_u + (j + 1) * cg]))
        gv = _gelu(_dot(h, w_ref[:, col_v + j * cg:col_v + (j + 1) * cg]))
        for g in range(cg // LANES):
            cs = slice(g * LANES, (g + 1) * LANES)
            oc = slice(j * cg + g * LANES, j * cg + (g + 1) * LANES)
            v = gv[:, cs]
            vc = v - jnp.mean(v, axis=-1, keepdims=True)
            vn = vc * lax.rsqrt(jnp.mean(vc * vc, axis=-1, keepdims=True) + EPS) * ln_ref[:, oc]
            vn = vn.astype(BF16)
            w = ws_ref[j * (cg // LANES) + g].astype(BF16)
            for c in range(tm // LANES):
                rs = slice(c * LANES, (c + 1) * LANES)
                s = _dot(w, vn[rs, :]) + bs_ref[:, oc]
                gm_ref[0, rs, oc] = (gu[rs, cs] * s).astype(BF16)


def _inproj(x, mod, g_pre, w_in_b, k_norm, q_norm, cos, sin, gm_ln, w_s, bs_full, kv_w, q_w, gm_w, scale):
    bsz, l, d = x.shape
    tm = _tile(l, TOKEN_TILE)
    n_cols = 2 * kv_w + q_w + 2 * gm_w
    assert tm % LANES == 0
    tok = lambda b, i: (b, i, 0)
    c2 = lambda b, i: (0, 0)
    return pl.pallas_call(
        functools.partial(_inproj_kernel, kv_w=kv_w, q_w=q_w, gm_w=gm_w, scale=scale),
        grid=(bsz, l // tm),
        in_specs=[pl.BlockSpec((1, tm, d), tok),
                  pl.BlockSpec((1, N_MOD, d), lambda b, i: (b, 0, 0)),
                  pl.BlockSpec((1, d), c2),
                  pl.BlockSpec((d, n_cols), c2, pipeline_mode=pl.Buffered(1)),
                  pl.BlockSpec((1, LANES), c2),
                  pl.BlockSpec((1, LANES), c2),
                  pl.BlockSpec((tm, LANES), lambda b, i: (i, 0)),
                  pl.BlockSpec((tm, LANES), lambda b, i: (i, 0)),
                  pl.BlockSpec((1, gm_w), c2),
                  pl.BlockSpec(w_s.shape, lambda b, i: (0, 0, 0)),
                  pl.BlockSpec(bs_full.shape, c2)],
        out_specs=[pl.BlockSpec((1, tm, d), tok),
                   pl.BlockSpec((1, tm, kv_w), tok),
                   pl.BlockSpec((1, tm, kv_w), tok),
                   pl.BlockSpec((1, tm, q_w), tok),
                   pl.BlockSpec((1, tm, gm_w), tok)],
        out_shape=[jax.ShapeDtypeStruct((bsz, l, d), BF16),
                   jax.ShapeDtypeStruct((bsz, l, kv_w), BF16),
                   jax.ShapeDtypeStruct((bsz, l, kv_w), BF16),
                   jax.ShapeDtypeStruct((bsz, l, q_w), BF16),
                   jax.ShapeDtypeStruct((bsz, l, gm_w), BF16)],
        compiler_params=_params("arbitrary", "arbitrary"),
        name="inproj",
    )(x, mod, g_pre.reshape(1, d), w_in_b, k_norm.reshape(1, LANES), q_norm.reshape(1, LANES), cos, sin,
      gm_ln.reshape(1, gm_w), w_s, bs_full)


def _attn_kernel(q_ref, kc_ref, vc_ref, kx_ref, vx_ref, o_ref, *, n_kv, grp):
    nt = (((1,), (1,)), ((), ()))
    for kv in range(n_kv):
        ks = slice(kv * LANES, (kv + 1) * LANES)
        kc, kx = kc_ref[0, :, ks], kx_ref[0, :, ks]
        vc = jnp.concatenate([vc_ref[0, :, ks], jnp.ones_like(kc)], axis=1)
        vx = jnp.concatenate([vx_ref[0, :, ks], jnp.ones_like(kx)], axis=1)
        for hh in range(grp):
            sl = slice((kv * grp + hh) * LANES, (kv * grp + hh + 1) * LANES)
            q = q_ref[0, :, sl]
            sc = lax.dot_general(q, kc, nt, preferred_element_type=F32)
            sx = lax.dot_general(q, kx, nt, preferred_element_type=F32)
            m = jnp.maximum(jnp.max(sc, axis=-1, keepdims=True), jnp.max(sx, axis=-1, keepdims=True))
            pc = jnp.exp2(sc - m).astype(BF16)
            px = jnp.exp2(sx - m).astype(BF16)
            o = _dot(pc, vc) + _dot(px, vx)
            o_ref[0, :, sl] = (o[:, :LANES] / o[:, LANES:LANES + 1]).astype(BF16)


def _attention(q, kc, vc, kx, vx, n_kv):
    bsz, l, q_w = q.shape
    lc, kv_w = kc.shape[1], kc.shape[2]
    tq = _tile(l, ATTN_Q_TILE)
    tok = lambda b, i: (b, i, 0)
    whole = lambda b, i: (b, 0, 0)
    return pl.pallas_call(
        functools.partial(_attn_kernel, n_kv=n_kv, grp=q_w // kv_w),
        grid=(bsz, l // tq),
        in_specs=[pl.BlockSpec((1, tq, q_w), tok),
                  pl.BlockSpec((1, lc, kv_w), whole),
                  pl.BlockSpec((1, lc, kv_w), whole),
                  pl.BlockSpec((1, l, kv_w), whole),
                  pl.BlockSpec((1, l, kv_w), whole)],
        out_specs=pl.BlockSpec((1, tq, q_w), tok),
        out_shape=jax.ShapeDtypeStruct((bsz, l, q_w), BF16),
        compiler_params=_params("arbitrary", "arbitrary"),
        name="attention",
    )(q, kc, vc, kx, vx)


def _merge_kernel(h_ref, a_ref, g_ref, *refs, n_chunk):
    wga, wgg = refs[:n_chunk], refs[n_chunk:2 * n_chunk]
    wba_ref, wbg_ref, o_ref = refs[2 * n_chunk:]
    h, a, g = h_ref[0], a_ref[0], g_ref[0]
    tn = wga[0].shape[1]
    for j in range(n_chunk):
        cs = slice(j * tn, (j + 1) * tn)
        ga = jax.nn.sigmoid(_dot(h, wga[j][...]))
        gg = jax.nn.sigmoid(_dot(h, wgg[j][...]))
        o_ref[0, :, cs] = (ga * _dot(a, wba_ref[:, cs]) + gg * _dot(g, wbg_ref[:, cs])).astype(BF16)


def _merge(h, attn, gm, w_in_b, w_ba_b, w_bg_b, col_ga, col_gg):
    bsz, l, d = h.shape
    q_w, gm_w = attn.shape[2], gm.shape[2]
    tm = _tile(l, TOKEN_TILE)
    tn = _tile(d, COL_CHUNK)
    assert col_ga % tn == 0 and col_gg % tn == 0
    n_chunk = d // tn
    tok = lambda b, i: (b, i, 0)
    gate_specs = [pl.BlockSpec((d, tn), functools.partial(lambda b, i, c: (0, c), c=(c0 // tn) + j),
                               pipeline_mode=pl.Buffered(1))
                  for c0 in (col_ga, col_gg) for j in range(n_chunk)]
    return pl.pallas_call(
        functools.partial(_merge_kernel, n_chunk=n_chunk),
        grid=(bsz, l // tm),
        in_specs=[pl.BlockSpec((1, tm, d), tok),
                  pl.BlockSpec((1, tm, q_w), tok),
                  pl.BlockSpec((1, tm, gm_w), tok),
                  *gate_specs,
                  pl.BlockSpec((q_w, d), lambda b, i: (0, 0)),
                  pl.BlockSpec((gm_w, d), lambda b, i: (0, 0))],
        out_specs=pl.BlockSpec((1, tm, d), tok),
        out_shape=jax.ShapeDtypeStruct((bsz, l, d), BF16),
        compiler_params=_params("arbitrary", "arbitrary"),
        name="merge",
    )(h, attn, gm, *([w_in_b] * (2 * n_chunk)), w_ba_b, w_bg_b)


R_W0, R_W1 = range(2)
T_E0, T_E1, T_R0_HI, T_R0_LO, T_R1_HI, T_R1_LO = range(6)


def _out_kernel(m_ref, x_ref, mod_ref, gp1_ref, gp2_ref, wo_ref, wr_ref, br_ref,
                x1_ref, h2_ref, ri_ref, rt_ref, cnt_ref, carry_ref, *, n_groups, per_group):
    @pl.when((pl.program_id(0) == 0) & (pl.program_id(1) == 0))
    def _():
        carry_ref[...] = jnp.zeros_like(carry_ref)

    tm = m_ref.shape[1]
    sub = tm
    wr_hi, wr_lo = _split_bf16(wr_ref[...])
    wr_both = jnp.concatenate([wr_hi, wr_lo], axis=1)
    lane = lax.broadcasted_iota(jnp.int32, (sub, LANES), 1)
    row = lax.broadcasted_iota(jnp.int32, (sub, sub), 0)
    col = lax.broadcasted_iota(jnp.int32, (sub, sub), 1)
    tri = jnp.where(col < row, 1.0, 0.0).astype(BF16)
    sel = jnp.where(lax.broadcasted_iota(jnp.int32, (SUBLANES, LANES), 0)
                    == lax.broadcasted_iota(jnp.int32, (SUBLANES, LANES), 1), 1.0, 0.0).astype(BF16)
    neg = jnp.float32(-jnp.inf)
    carry = carry_ref[...]

    for sb in range(tm // sub):
        rs = slice(sb * sub, (sb + 1) * sub)
        mix = _dot(m_ref[0, rs, :], wo_ref[...])
        inv1 = lax.rsqrt(jnp.mean(mix * mix, axis=-1, keepdims=True) + EPS)
        x1 = x_ref[0, rs, :] + (mix * inv1) * (mod_ref[0, 2:3, :] * gp1_ref[...])
        x1_ref[0, rs, :] = x1
        inv2 = lax.rsqrt(jnp.mean(x1 * x1, axis=-1, keepdims=True) + EPS)
        h2 = (x1 * inv2) * (gp2_ref[...] * (1.0 + mod_ref[0, 4:5, :])) + mod_ref[0, 3:4, :]
        h2_ref[0, rs, :] = h2

        h_hi, h_lo = _split_bf16(h2)
        both = _dot(h_hi, wr_both)
        logits = both[:, :LANES] + both[:, LANES:] + _dot(h_lo, wr_hi) + br_ref[...]
        lg = jnp.where(lane < n_groups, logits, neg)
        gmax = jnp.max(lg, axis=-1, keepdims=True)
        p_top = 1.0 / jnp.sum(jnp.exp(lg - gmax), axis=-1, keepdims=True)
        gidx = jnp.min(jnp.where(lg == gmax, lane, LANES), axis=-1, keepdims=True)
        lo = n_groups + gidx * per_group
        le = jnp.where((lane >= lo) & (lane < lo + per_group), logits, neg)
        l1 = jnp.max(le, axis=-1, keepdims=True)
        i1 = jnp.min(jnp.where(le == l1, lane, LANES), axis=-1, keepdims=True)
        le2 = jnp.where(lane == i1, neg, le)
        l2 = jnp.max(le2, axis=-1, keepdims=True)
        i2 = jnp.min(jnp.where(le2 == l2, lane, LANES), axis=-1, keepdims=True)
        r = jnp.exp(l2 - l1)
        w0 = p_top / (1.0 + r)
        w1 = p_top * r / (1.0 + r)

        oh1 = lane == i1
        oh2 = lane == i2
        oh = jnp.where(oh1 | oh2, 1.0, 0.0)
        base = _dot(tri, oh.astype(BF16)) + carry
        rank0 = jnp.sum(jnp.where(oh1, base, 0.0), axis=-1, keepdims=True)
        rank1 = jnp.sum(jnp.where(oh2, base, 0.0), axis=-1, keepdims=True)
        carry = carry + jnp.sum(oh, axis=0, keepdims=True)

        e0 = (i1 - n_groups).astype(F32)
        e1 = (i2 - n_groups).astype(F32)
        rec = jnp.zeros_like(logits)
        for idx, val in ((R_W0, w0), (R_W1, w1)):
            rec = jnp.where(lane == idx, val, rec)
        ri_ref[0, rs, :] = rec

        r0_hi = jnp.floor(rank0 * (1.0 / 256.0))
        r1_hi = jnp.floor(rank1 * (1.0 / 256.0))
        ints = jnp.zeros_like(logits)
        for idx, val in ((T_E0, e0), (T_E1, e1), (T_R0_HI, r0_hi), (T_R0_LO, rank0 - 256.0 * r0_hi),
                         (T_R1_HI, r1_hi), (T_R1_LO, rank1 - 256.0 * r1_hi)):
            ints = jnp.where(lane == idx, val, ints)
        rt = lax.dot_general(sel, ints.astype(BF16), (((1,), (1,)), ((), ())), preferred_element_type=F32)
        rt_ref[:, rs] = rt.astype(jnp.int32)

    carry_ref[...] = carry
    cnt_ref[...] = carry


def _out_route(merged, x, mod, g_post1, g_pre2, w_o_b, w_r, b_r, n_groups, per_group):
    bsz, l, d = x.shape
    tm = _tile(l, TOKEN_TILE)
    tok = lambda b, i: (b, i, 0)
    const2 = lambda b, i: (0, 0)
    return pl.pallas_call(
        functools.partial(_out_kernel, n_groups=n_groups, per_group=per_group),
        grid=(bsz, l // tm),
        in_specs=[pl.BlockSpec((1, tm, d), tok),
                  pl.BlockSpec((1, tm, d), tok),
                  pl.BlockSpec((1, N_MOD, d), lambda b, i: (b, 0, 0)),
                  pl.BlockSpec((1, d), const2),
                  pl.BlockSpec((1, d), const2),
                  pl.BlockSpec((d, d), const2),
                  pl.BlockSpec((d, LANES), const2),
                  pl.BlockSpec((1, LANES), const2)],
        out_specs=[pl.BlockSpec((1, tm, d), tok),
                   pl.BlockSpec((1, tm, d), tok),
                   pl.BlockSpec((1, tm, LANES), tok),
                   pl.BlockSpec((SUBLANES, tm), lambda b, i: (0, b * (l // tm) + i)),
                   pl.BlockSpec((1, LANES), const2)],
        out_shape=[jax.ShapeDtypeStruct((bsz, l, d), F32),
                   jax.ShapeDtypeStruct((bsz, l, d), F32),
                   jax.ShapeDtypeStruct((bsz, l, LANES), F32),
                   jax.ShapeDtypeStruct((SUBLANES, bsz * l), jnp.int32),
                   jax.ShapeDtypeStruct((1, LANES), F32)],
        scratch_shapes=[pltpu.VMEM((1, LANES), F32)],
        compiler_params=_params("arbitrary", "arbitrary"),
        name="out_route",
    )(merged, x, mod, g_post1.reshape(1, d), g_pre2.reshape(1, d), w_o_b, w_r, b_r)


def _dispatch_kernel(pos_ref, zf_ref, h_ref, xs_ref, zbuf, stage, sem, zsem, *, tm, n_c, n_tiles, n_tok, n_steps):
    i = pl.program_id(0)
    tile_rows = EXPERT_TILE * n_c
    prev = (i + 1) % 2

    def zero_copy(t):
        return pltpu.make_async_copy(zbuf, xs_ref.at[pl.ds(pl.multiple_of(t * tile_rows, tile_rows), tile_rows)], zsem)

    @pl.when(i == 0)
    def _():
        zbuf[...] = jnp.zeros_like(zbuf)

        def issue(t, c):
            @pl.when(zf_ref[t] != 0)
            def _():
                zero_copy(t).start()
            return c

        def drain(t, c):
            @pl.when(zf_ref[t] != 0)
            def _():
                zero_copy(t).wait()
            return c

        lax.fori_loop(0, n_tiles, issue, 0)
        lax.fori_loop(0, n_tiles, drain, 0)

    def issue_rows(rb, c):
        for u in range(DMA_UNROLL):
            r = rb * DMA_UNROLL + u
            src = stage.at[prev, pl.ds(pl.multiple_of(r * n_c, n_c), n_c)]
            for k in range(TOP_K):
                p = pos_ref[k * n_tok + (i - 1) * tm + r]
                pltpu.make_async_copy(src, xs_ref.at[pl.ds(pl.multiple_of(p * n_c, n_c), n_c)],
                                      sem).start(priority=k % 2)
        return c

    @pl.when(i >= 1)
    def _():
        lax.fori_loop(0, tm // DMA_UNROLL, issue_rows, 0)

    @pl.when(i < n_steps)
    def _():
        _store_row_tiled(stage.at[i % 2], 0, h_ref[...])

    @pl.when(i >= 1)
    def _():
        for k in range(TOP_K):
            pltpu.make_async_copy(stage.at[prev], xs_ref.at[pl.ds(0, tm * n_c)], sem).wait()


def _dispatch(h2, pos, zflag, n_rows):
    n, d = h2.shape
    n_c = d // LANES
    tm = _tile(n, DISPATCH_TILE)
    n_steps = n // tm
    n_tiles = n_rows // EXPERT_TILE
    return pl.pallas_call(
        functools.partial(_dispatch_kernel, tm=tm, n_c=n_c, n_tiles=n_tiles, n_tok=n, n_steps=n_steps),
        grid_spec=pltpu.PrefetchScalarGridSpec(
            num_scalar_prefetch=2,
            grid=(n_steps + 1,),
            in_specs=[pl.BlockSpec((tm, d), lambda i, p, z: (jnp.minimum(i, n_steps - 1), 0))],
            out_specs=pl.BlockSpec(memory_space=pl.ANY),
            scratch_shapes=[pltpu.VMEM((EXPERT_TILE * n_c, LANES), F32),
                            pltpu.VMEM((2, tm * n_c, LANES), F32),
                            pltpu.SemaphoreType.DMA(()),
                            pltpu.SemaphoreType.DMA(())]),
        out_shape=jax.ShapeDtypeStruct((n_rows * n_c, LANES), F32),
        compiler_params=_params("arbitrary"),
        name="dispatch",
    )(pos, zflag, h2)


def _cast_rows(src_ref, dst_ref, chunk=256):
    rows = dst_ref.shape[0]
    chunk = min(chunk, rows)

    def body(c, carry):
        r0 = pl.multiple_of(c * chunk, chunk)
        dst_ref[pl.ds(r0, chunk), :] = src_ref[pl.ds(r0, chunk), :].astype(BF16)
        return carry

    lax.fori_loop(0, rows // chunk, body, 0)


def _experts_kernel(te_ref, chg_ref, nxt_ref, nt_ref, xs_ref, wg_hbm, wu_hbm, wd_hbm, o_ref,
                    wf_g, wf_u, wf_d, wb_g, wb_u, wb_d, sem):
    t = pl.program_id(0)
    stages = ((wg_hbm, wf_g, wb_g), (wu_hbm, wf_u, wb_u), (wd_hbm, wf_d, wb_d))

    def fetch(e):
        return [pltpu.make_async_copy(src.at[e], dst, sem) for src, dst, _ in stages]

    @pl.when(t < nt_ref[0])
    def _():
        @pl.when(chg_ref[t] != 0)
        def _():
            @pl.when(t == 0)
            def _():
                for cp in fetch(te_ref[0]):
                    cp.start(priority=1)

            for cp in fetch(te_ref[t]):
                cp.wait()
            for _, wf, wb in stages:
                _cast_rows(wf, wb)

            @pl.when(nxt_ref[t] >= 0)
            def _():
                for cp in fetch(nxt_ref[t]):
                    cp.start(priority=1)

        x = _load_row_tiled(xs_ref, EXPERT_TILE, wb_g.shape[0] // LANES).astype(BF16)
        g = _dot(x, wb_g[...])
        u = _dot(x, wb_u[...])
        hid = (g * jax.nn.sigmoid(g) * u).astype(BF16)
        o_ref[...] = _dot(hid, wb_d[...])

    @pl.when(t >= nt_ref[0])
    def _():
        o_ref[...] = jnp.zeros_like(o_ref)


def _experts(xs, w_gate, w_up, w_down, plan):
    d, de = w_gate.shape[1], w_gate.shape[2]
    n_c = d // LANES
    n_rows = xs.shape[0] // n_c
    tile = lambda t, *_: (t, 0)
    hbm = pl.BlockSpec(memory_space=pl.ANY)
    return pl.pallas_call(
        _experts_kernel,
        grid_spec=pltpu.PrefetchScalarGridSpec(
            num_scalar_prefetch=len(plan),
            grid=(n_rows // EXPERT_TILE,),
            in_specs=[pl.BlockSpec((EXPERT_TILE * n_c, LANES), tile), hbm, hbm, hbm],
            out_specs=pl.BlockSpec((EXPERT_TILE, d), tile),
            scratch_shapes=[pltpu.VMEM((d, de), F32), pltpu.VMEM((d, de), F32), pltpu.VMEM((de, d), F32),
                            pltpu.VMEM((d, de), BF16), pltpu.VMEM((d, de), BF16), pltpu.VMEM((de, d), BF16),
                            pltpu.SemaphoreType.DMA(())]),
        out_shape=jax.ShapeDtypeStruct((n_rows, d), F32),
        compiler_params=_params("arbitrary"),
        name="experts",
    )(*plan, xs, w_gate, w_up, w_down)


def _combine_kernel(pos_ref, x1_ref, ri_ref, mod_ref, gp_ref, ys_ref, o_ref, buf, sem, *, tm, n_l, n_tok):
    step = pl.program_id(0) * n_l + pl.program_id(1)
    n_steps = pl.num_programs(0) * n_l
    slot = step % 2

    def issue(s, sl):
        def body(rb, c):
            for u in range(DMA_UNROLL):
                r = rb * DMA_UNROLL + u
                for k in range(TOP_K):
                    p = pos_ref[k * n_tok + s * tm + r]
                    pltpu.make_async_copy(ys_ref.at[pl.ds(p, 1)], buf.at[sl, k, rb, pl.ds(u, 1)],
                                          sem.at[sl]).start(priority=k % 2)
            return c

        lax.fori_loop(0, tm // DMA_UNROLL, body, 0)

    @pl.when(step == 0)
    def _():
        issue(0, 0)

    @pl.when(step + 1 < n_steps)
    def _():
        issue(step + 1, 1 - slot)

    for k in range(TOP_K):
        pltpu.make_async_copy(buf.at[slot, k], buf.at[slot, k], sem.at[slot]).wait()

    ri = ri_ref[0]
    d = o_ref.shape[-1]
    moe = (ri[:, R_W0:R_W0 + 1] * buf[slot, 0].reshape(tm, d)
           + ri[:, R_W1:R_W1 + 1] * buf[slot, 1].reshape(tm, d))
    o_ref[0] = x1_ref[0] + mod_ref[0, 5:6, :] * _rms(moe, gp_ref[...])


def _combine(x1, rinfo, mod, g_post2, ys, pos):
    bsz, l, d = x1.shape
    tm = _tile(l, COMBINE_TILE)
    n_l = l // tm
    tok = lambda b, i, p: (b, i, 0)
    return pl.pallas_call(
        functools.partial(_combine_kernel, tm=tm, n_l=n_l, n_tok=bsz * l),
        grid_spec=pltpu.PrefetchScalarGridSpec(
            num_scalar_prefetch=1,
            grid=(bsz, n_l),
            in_specs=[pl.BlockSpec((1, tm, d), tok),
                      pl.BlockSpec((1, tm, LANES), tok),
                      pl.BlockSpec((1, N_MOD, d), lambda b, i, p: (b, 0, 0)),
                      pl.BlockSpec((1, d), lambda b, i, p: (0, 0)),
                      pl.BlockSpec(memory_space=pl.ANY)],
            out_specs=pl.BlockSpec((1, tm, d), tok),
            scratch_shapes=[pltpu.VMEM((2, TOP_K, tm // DMA_UNROLL, DMA_UNROLL, d), F32),
                            pltpu.SemaphoreType.DMA((2,))]),
        out_shape=jax.ShapeDtypeStruct((bsz, l, d), F32),
        compiler_params=_params("arbitrary", "arbitrary"),
        name="combine",
    )(pos, x1, rinfo, mod, g_post2.reshape(1, d), ys)


def _rope_tables(seq, head_dim):
    axis_dim = head_dim // 2
    t = jnp.arange(seq, dtype=jnp.int32)
    pos = jnp.stack([t // GRID_W, t % GRID_W], axis=-1).astype(F32)
    inv_freq = ROPE_THETA ** (-jnp.arange(0, axis_dim, 2, dtype=F32) / axis_dim)
    ang = pos[:, :, None] * inv_freq
    cos, sin = jnp.cos(ang), jnp.sin(ang)
    cos_t = jnp.concatenate([cos, cos], axis=-1).reshape(seq, head_dim)
    sin_t = jnp.concatenate([-sin, sin], axis=-1).reshape(seq, head_dim)
    return cos_t, sin_t


def _route_plan(rt, cnt, n_groups, n_experts, n_tiles):
    e = rt[T_E0:T_E1 + 1]
    rank = jnp.stack([rt[T_R0_HI] * 256 + rt[T_R0_LO], rt[T_R1_HI] * 256 + rt[T_R1_LO]])
    counts = cnt[0, n_groups:n_groups + n_experts].astype(jnp.int32)
    tiles_e = (counts + EXPERT_TILE - 1) // EXPERT_TILE
    ids = jnp.arange(n_experts, dtype=jnp.int32)
    tile_end = jnp.sum(jnp.where(ids[None, :] <= ids[:, None], tiles_e[None, :], 0), axis=1)
    tile_start = tile_end - tiles_e
    nt = tile_end[-1]
    row0 = tile_start * EXPERT_TILE
    pos = jnp.sum(jnp.where(e[None] == ids[:, None, None], row0[:, None, None], 0), axis=0) + rank
    t = jnp.arange(n_tiles, dtype=jnp.int32)
    owner = lambda q: jnp.sum((tile_end[None, :] <= q[:, None]).astype(jnp.int32), axis=1)
    te = owner(jnp.minimum(t, nt - 1))
    chg = ((t == 0) | (te != owner(jnp.minimum(jnp.maximum(t - 1, 0), nt - 1)))).astype(jnp.int32)
    partial_last = jnp.any((tile_end[None, :] - 1 == t[:, None]) & (counts[None, :] % EXPERT_TILE != 0), axis=1)
    zflag = ((t >= nt) | partial_last).astype(jnp.int32)
    used = tiles_e > 0
    later = used[None, :] & (ids[None, :] > te[:, None])
    nxt = jnp.min(jnp.where(later, ids[None, :], n_experts), axis=1)
    nxt = jnp.where(nxt == n_experts, -1, nxt)
    return pos.reshape(-1), (te, chg, nxt, nt.reshape(1)), zflag


def kernel(x, c, ctx, c_ctx, w_ada, b_ada, g_pre1, g_post1, g_pre2, g_post2, w_in, q_norm, k_norm,
           gm_ln, w_s, b_s, w_ba, w_bg, w_o, w_rg, b_rg, w_re, b_re, w_gate, w_up, w_down):
    bsz, seq, d = x.shape
    depth = w_ada.shape[0]
    head_dim = q_norm.shape[-1]
    q_w, gm_w = w_ba.shape[1], w_bg.shape[1]
    kv_w = (w_in.shape[2] - q_w - 2 * gm_w - 2 * d) // 2
    n_groups, per_group = w_re.shape[2], w_re.shape[3]
    n_experts = n_groups * per_group
    assert head_dim == LANES and w_s.shape[2] == LANES and gm_w // w_s.shape[1] == LANES
    assert n_groups + n_experts <= LANES and seq % GRID_W == 0
    col_q = 2 * kv_w
    col_u, col_v = col_q + q_w, col_q + q_w + gm_w
    col_ga, col_gg = col_v + gm_w, col_v + gm_w + d
    n_tok = bsz * seq
    n_rows = n_tok * TOP_K + n_experts * EXPERT_TILE
    n_tiles = n_rows // EXPERT_TILE

    cos_t, sin_t = _rope_tables(seq, head_dim)
    pad = (-(bsz + 1)) % (2 * SUBLANES)
    cs = jnp.concatenate([c, c_ctx[None, :], jnp.zeros((pad, d), F32)], axis=0)

    for l in range(depth):
        assert l + 1 == depth, "context-stream update for non-final layers is not implemented"
        mod = _ada(cs, w_ada[l], b_ada[l]).reshape(cs.shape[0], N_MOD, d)
        w_in_b = w_in[l].astype(BF16)
        w_ba_b, w_bg_b, w_o_b = w_ba[l].astype(BF16), w_bg[l].astype(BF16), w_o[l].astype(BF16)
        bs_full = jnp.repeat(b_s[l].T, LANES, axis=1)
        w_r = jnp.concatenate([w_rg[l], w_re[l].reshape(d, n_experts),
                               jnp.zeros((d, LANES - n_groups - n_experts), F32)], axis=1)
        b_r = jnp.concatenate([b_rg[l], b_re[l].reshape(n_experts),
                               jnp.zeros((LANES - n_groups - n_experts,), F32)]).reshape(1, LANES)

        kc, vc = _ctx_kv(ctx, mod, bsz, g_pre1[l], w_in_b, k_norm[l], kv_w)
        hx, kx, vx, qx, gm = _inproj(x, mod, g_pre1[l], w_in_b, k_norm[l], q_norm[l], cos_t, sin_t, gm_ln[l],
                                     w_s[l], bs_full, kv_w, q_w, gm_w, head_dim ** -0.5 * LOG2_E)
        attn = _attention(qx, kc, vc, kx, vx, kv_w // head_dim)
        merged = _merge(hx, attn, gm, w_in_b, w_ba_b, w_bg_b, col_ga, col_gg)

        x1, h2, rinfo, rt, cnt = _out_route(merged, x, mod, g_post1[l], g_pre2[l], w_o_b, w_r, b_r,
                                            n_groups, per_group)
        pos, plan, zflag = _route_plan(rt, cnt, n_groups, n_experts, n_tiles)

        xs = _dispatch(h2.reshape(n_tok, d), pos, zflag, n_rows)
        ys = _experts(xs, w_gate[l], w_up[l], w_down[l], plan)
        x = _combine(x1, rinfo, mod, g_post2[l], ys, pos)
    return x
```

```python
import functools

import jax
import jax.numpy as jnp
from jax import lax
from jax.experimental import pallas as pl
from jax.experimental.pallas import tpu as pltpu

GRID_W = 64
ROPE_THETA = 10000.0
EPS = 1e-6
N_MOD = 6
TOP_K = 2
LOG2_E = 1.4426950408889634

LANES = 128
SUBLANES = 8
VMEM_LIMIT_BYTES = 56 * 1024 * 1024

EXPERT_TILE = 256
TOKEN_TILE = 512
ATTN_Q_TILE = 1024
DISPATCH_TILE = 512
COMBINE_TILE = 256
ADA_COL_TILE = 1024
COL_CHUNK = 512
DMA_UNROLL = SUBLANES

F32 = jnp.float32
BF16 = jnp.bfloat16


def _params(*sem):
    return pltpu.CompilerParams(dimension_semantics=sem, vmem_limit_bytes=VMEM_LIMIT_BYTES)


def _dot(a, b):
    return jnp.dot(a, b, preferred_element_type=F32)


def _split_bf16(a):
    hi = a.astype(BF16)
    lo = (a - hi.astype(F32)).astype(BF16)
    return hi, lo


def _dot3(a, w):
    a_hi, a_lo = _split_bf16(a)
    w_hi, w_lo = _split_bf16(w)
    return _dot(a_hi, w_hi) + _dot(a_lo, w_hi) + _dot(a_hi, w_lo)


def _rms(x, g):
    return x * lax.rsqrt(jnp.mean(x * x, axis=-1, keepdims=True) + EPS) * g


def _store_row_tiled(ref, t0, val):
    rows, d = val.shape
    n_c = d // LANES
    for c in range(n_c):
        ref[pl.ds(t0 * n_c + c, rows, stride=n_c), :] = val[:, c * LANES:(c + 1) * LANES]


def _load_row_tiled(ref, rows, n_c):
    return jnp.concatenate([ref[pl.ds(c, rows, stride=n_c), :] for c in range(n_c)], axis=1)


def _tile(n, pref):
    t = min(n, pref)
    while n % t:
        t //= 2
    return t


def _ada_kernel(c_ref, w_ref, b_ref, o_ref):
    c = c_ref[...]
    a = c * jax.nn.sigmoid(c)
    o_ref[...] = _dot3(a, w_ref[...]) + b_ref[...]


def _ada(cs, w, b):
    m, d = cs.shape
    n = w.shape[1]
    tn = _tile(n, ADA_COL_TILE)
    return pl.pallas_call(
        _ada_kernel,
        grid=(n // tn,),
        in_specs=[pl.BlockSpec((m, d), lambda j: (0, 0)),
                  pl.BlockSpec((d, tn), lambda j: (0, j)),
                  pl.BlockSpec((1, tn), lambda j: (0, j))],
        out_specs=pl.BlockSpec((m, tn), lambda j: (0, j)),
        out_shape=jax.ShapeDtypeStruct((m, n), F32),
        compiler_params=_params("arbitrary"),
        name="ada",
    )(cs, w, b.reshape(1, n))


def _swap32(x):
    lane = lax.broadcasted_iota(jnp.int32, x.shape, 1)
    fwd = pltpu.roll(x, LANES - 32, 1)
    bwd = pltpu.roll(x, 32, 1)
    return jnp.where((lane & 32) == 0, fwd, bwd)


def _norm_head(r, gain, cos, sin, scale):
    y = _rms(r, gain)
    if cos is not None:
        y = y * cos + _swap32(y) * sin
    if scale != 1.0:
        y = y * scale
    return y.astype(BF16)


def _ctx_kv_kernel(x_ref, mod_ref, g_ref, w_ref, kn_ref, k_ref, v_ref, *, n_kv):
    x = x_ref[0]
    inv = lax.rsqrt(jnp.mean(x * x, axis=-1, keepdims=True) + EPS)
    h = ((x * inv) * (g_ref[...] * (1.0 + mod_ref[0, 1:2, :])) + mod_ref[0, 0:1, :]).astype(BF16)
    r = _dot(h, w_ref[...])
    for hh in range(n_kv):
        sl = slice(hh * LANES, (hh + 1) * LANES)
        k_ref[0, :, sl] = _norm_head(r[:, sl], kn_ref[...], None, None, 1.0)
    v_ref[0] = r[:, n_kv * LANES:].astype(BF16)


def _ctx_kv(ctx, mod, mod_row, g_pre, w_in_b, k_norm, kv_w):
    bsz, l, d = ctx.shape
    tm = _tile(l, TOKEN_TILE)
    out = jax.ShapeDtypeStruct((bsz, l, kv_w), BF16)
    tok = lambda b, i: (b, i, 0)
    return pl.pallas_call(
        functools.partial(_ctx_kv_kernel, n_kv=kv_w // LANES),
        grid=(bsz, l // tm),
        in_specs=[pl.BlockSpec((1, tm, d), tok),
                  pl.BlockSpec((1, N_MOD, d), lambda b, i: (mod_row, 0, 0)),
                  pl.BlockSpec((1, d), lambda b, i: (0, 0)),
                  pl.BlockSpec((d, 2 * kv_w), lambda b, i: (0, 0)),
                  pl.BlockSpec((1, LANES), lambda b, i: (0, 0))],
        out_specs=[pl.BlockSpec((1, tm, kv_w), tok)] * 2,
        out_shape=[out, out],
        compiler_params=_params("arbitrary", "arbitrary"),
        name="ctx_kv",
    )(ctx, mod, g_pre.reshape(1, d), w_in_b, k_norm.reshape(1, LANES))


def _gelu(x):
    c = 0.7978845608028654
    return x * (0.5 + 0.5 * jnp.tanh(x * (c + (c * 0.044715) * (x * x))))


def _inproj_kernel(x_ref, mod_ref, g_ref, w_ref, kn_ref, qn_ref, cos_ref, sin_ref, ln_ref, ws_ref, bs_ref,
                   h_ref, k_ref, v_ref, q_ref, gm_ref, *, kv_w, q_w, gm_w, scale):
    x = x_ref[0]
    inv = lax.rsqrt(jnp.mean(x * x, axis=-1, keepdims=True) + EPS)
    h = ((x * inv) * (g_ref[...] * (1.0 + mod_ref[0, 1:2, :])) + mod_ref[0, 0:1, :]).astype(BF16)
    h_ref[0] = h
    cos, sin = cos_ref[...], sin_ref[...]
    tm = h.shape[0]

    rk = _dot(h, w_ref[:, 0:kv_w])
    for hh in range(kv_w // LANES):
        sl = slice(hh * LANES, (hh + 1) * LANES)
        k_ref[0, :, sl] = _norm_head(rk[:, sl], kn_ref[...], cos, sin, 1.0)
    v_ref[0] = _dot(h, w_ref[:, kv_w:2 * kv_w]).astype(BF16)

    col_q = 2 * kv_w
    cq = min(COL_CHUNK, q_w)
    for j in range(q_w // cq):
        r = _dot(h, w_ref[:, col_q + j * cq:col_q + (j + 1) * cq])
        for hh in range(cq // LANES):
            sl = slice(hh * LANES, (hh + 1) * LANES)
            q_ref[0, :, j * cq + hh * LANES:j * cq + (hh + 1) * LANES] = _norm_head(
                r[:, sl], qn_ref[...], cos, sin, scale)

    col_u, col_v = col_q + q_w, col_q + q_w + gm_w
    cg = min(COL_CHUNK, gm_w)
    for j in range(gm_w // cg):
        gu = _gelu(_dot(h, w_ref[:, col_u + j * cg:col_u + (j + 1) * cg]))
        gv = _gelu(_dot(h, w_ref[:, col_v + j * cg:col_v + (j + 1) * cg]))
        for g in range(cg // LANES):
            cs = slice(g * LANES, (g + 1) * LANES)
            oc = slice(j * cg + g * LANES, j * cg + (g + 1) * LANES)
            v = gv[:, cs]
            vc = v - jnp.mean(v, axis=-1, keepdims=True)
            vn = vc * lax.rsqrt(jnp.mean(vc * vc, axis=-1, keepdims=True) + EPS) * ln_ref[:, oc]
            vn = vn.astype(BF16)
            w = ws_ref[j * (cg // LANES) + g].astype(BF16)
            for c in range(tm // LANES):
                rs = slice(c * LANES, (c + 1) * LANES)
                s = _dot(w, vn[rs, :]) + bs_ref[:, oc]
                gm_ref[0, rs, oc] = (gu[rs, cs] * s).astype(BF16)


def _inproj(x, mod, g_pre, w_in_b, k_norm, q_norm, cos, sin, gm_ln, w_s, bs_full, kv_w, q_w, gm_w, scale):
    bsz, l, d = x.shape
    tm = _tile(l, TOKEN_TILE)
    n_cols = 2 * kv_w + q_w + 2 * gm_w
    assert tm % LANES == 0
    tok = lambda b, i: (b, i, 0)
    c2 = lambda b, i: (0, 0)
    return pl.pallas_call(
        functools.partial(_inproj_kernel, kv_w=kv_w, q_w=q_w, gm_w=gm_w, scale=scale),
        grid=(bsz, l // tm),
        in_specs=[pl.BlockSpec((1, tm, d), tok),
                  pl.BlockSpec((1, N_MOD, d), lambda b, i: (b, 0, 0)),
                  pl.BlockSpec((1, d), c2),
                  pl.BlockSpec((d, n_cols), c2, pipeline_mode=pl.Buffered(1)),
                  pl.BlockSpec((1, LANES), c2),
                  pl.BlockSpec((1, LANES), c2),
                  pl.BlockSpec((tm, LANES), lambda b, i: (i, 0)),
                  pl.BlockSpec((tm, LANES), lambda b, i: (i, 0)),
                  pl.BlockSpec((1, gm_w), c2),
                  pl.BlockSpec(w_s.shape, lambda b, i: (0, 0, 0)),
                  pl.BlockSpec(bs_full.shape, c2)],
        out_specs=[pl.BlockSpec((1, tm, d), tok),
                   pl.BlockSpec((1, tm, kv_w), tok),
                   pl.BlockSpec((1, tm, kv_w), tok),
                   pl.BlockSpec((1, tm, q_w), tok),
                   pl.BlockSpec((1, tm, gm_w), tok)],
        out_shape=[jax.ShapeDtypeStruct((bsz, l, d), BF16),
                   jax.ShapeDtypeStruct((bsz, l, kv_w), BF16),
                   jax.ShapeDtypeStruct((bsz, l, kv_w), BF16),
                   jax.ShapeDtypeStruct((bsz, l, q_w), BF16),
                   jax.ShapeDtypeStruct((bsz, l, gm_w), BF16)],
        compiler_params=_params("arbitrary", "arbitrary"),
        name="inproj",
    )(x, mod, g_pre.reshape(1, d), w_in_b, k_norm.reshape(1, LANES), q_norm.reshape(1, LANES), cos, sin,
      gm_ln.reshape(1, gm_w), w_s, bs_full)


def _attn_kernel(q_ref, kc_ref, vc_ref, kx_ref, vx_ref, o_ref, *, n_kv, grp):
    nt = (((1,), (1,)), ((), ()))
    for kv in range(n_kv):
        ks = slice(kv * LANES, (kv + 1) * LANES)
        kc, kx = kc_ref[0, :, ks], kx_ref[0, :, ks]
        vc = jnp.concatenate([vc_ref[0, :, ks], jnp.ones_like(kc)], axis=1)
        vx = jnp.concatenate([vx_ref[0, :, ks], jnp.ones_like(kx)], axis=1)
        for hh in range(grp):
            sl = slice((kv * grp + hh) * LANES, (kv * grp + hh + 1) * LANES)
            q = q_ref[0, :, sl]
            sc = lax.dot_general(q, kc, nt, preferred_element_type=F32)
            sx = lax.dot_general(q, kx, nt, preferred_element_type=F32)
            m = jnp.maximum(jnp.max(sc, axis=-1, keepdims=True), jnp.max(sx, axis=-1, keepdims=True))
            pc = jnp.exp2(sc - m).astype(BF16)
            px = jnp.exp2(sx - m).astype(BF16)
            o = _dot(pc, vc) + _dot(px, vx)
            o_ref[0, :, sl] = (o[:, :LANES] / o[:, LANES:LANES + 1]).astype(BF16)


def _attention(q, kc, vc, kx, vx, n_kv):
    bsz, l, q_w = q.shape
    lc, kv_w = kc.shape[1], kc.shape[2]
    tq = _tile(l, ATTN_Q_TILE)
    tok = lambda b, i: (b, i, 0)
    whole = lambda b, i: (b, 0, 0)
    return pl.pallas_call(
        functools.partial(_attn_kernel, n_kv=n_kv, grp=q_w // kv_w),
        grid=(bsz, l // tq),
        in_specs=[pl.BlockSpec((1, tq, q_w), tok),
                  pl.BlockSpec((1, lc, kv_w), whole),
                  pl.BlockSpec((1, lc, kv_w), whole),
                  pl.BlockSpec((1, l, kv_w), whole),
                  pl.BlockSpec((1, l, kv_w), whole)],
        out_specs=pl.BlockSpec((1, tq, q_w), tok),
        out_shape=jax.ShapeDtypeStruct((bsz, l, q_w), BF16),
        compiler_params=_params("arbitrary", "arbitrary"),
        name="attention",
    )(q, kc, vc, kx, vx)


def _merge_kernel(h_ref, a_ref, g_ref, *refs, n_chunk):
    wga, wgg = refs[:n_chunk], refs[n_chunk:2 * n_chunk]
    wba_ref, wbg_ref, o_ref = refs[2 * n_chunk:]
    h, a, g = h_ref[0], a_ref[0], g_ref[0]
    tn = wga[0].shape[1]
    for j in range(n_chunk):
        cs = slice(j * tn, (j + 1) * tn)
        ga = jax.nn.sigmoid(_dot(h, wga[j][...]))
        gg = jax.nn.sigmoid(_dot(h, wgg[j][...]))
        o_ref[0, :, cs] = (ga * _dot(a, wba_ref[:, cs]) + gg * _dot(g, wbg_ref[:, cs])).astype(BF16)


def _merge(h, attn, gm, w_in_b, w_ba_b, w_bg_b, col_ga, col_gg):
    bsz, l, d = h.shape
    q_w, gm_w = attn.shape[2], gm.shape[2]
    tm = _tile(l, TOKEN_TILE)
    tn = _tile(d, COL_CHUNK)
    assert col_ga % tn == 0 and col_gg % tn == 0
    n_chunk = d // tn
    tok = lambda b, i: (b, i, 0)
    gate_specs = [pl.BlockSpec((d, tn), functools.partial(lambda b, i, c: (0, c), c=(c0 // tn) + j),
                               pipeline_mode=pl.Buffered(1))
                  for c0 in (col_ga, col_gg) for j in range(n_chunk)]
    return pl.pallas_call(
        functools.partial(_merge_kernel, n_chunk=n_chunk),
        grid=(bsz, l // tm),
        in_specs=[pl.BlockSpec((1, tm, d), tok),
                  pl.BlockSpec((1, tm, q_w), tok),
                  pl.BlockSpec((1, tm, gm_w), tok),
                  *gate_specs,
                  pl.BlockSpec((q_w, d), lambda b, i: (0, 0)),
                  pl.BlockSpec((gm_w, d), lambda b, i: (0, 0))],
        out_specs=pl.BlockSpec((1, tm, d), tok),
        out_shape=jax.ShapeDtypeStruct((bsz, l, d), BF16),
        compiler_params=_params("arbitrary", "arbitrary"),
        name="merge",
    )(h, attn, gm, *([w_in_b] * (2 * n_chunk)), w_ba_b, w_bg_b)


R_W0, R_W1 = range(2)
T_E0, T_E1, T_R0_HI, T_R0_LO, T_R1_HI, T_R1_LO = range(6)


def _out_kernel(m_ref, x_ref, mod_ref, gp1_ref, gp2_ref, wo_ref, wr_ref, br_ref,
                x1_ref, h2_ref, ri_ref, rt_ref, cnt_ref, carry_ref, *, n_groups, per_group):
    @pl.when((pl.program_id(0) == 0) & (pl.program_id(1) == 0))
    def _():
        carry_ref[...] = jnp.zeros_like(carry_ref)

    tm = m_ref.shape[1]
    sub = tm
    wr_hi, wr_lo = _split_bf16(wr_ref[...])
    wr_both = jnp.concatenate([wr_hi, wr_lo], axis=1)
    lane = lax.broadcasted_iota(jnp.int32, (sub, LANES), 1)
    row = lax.broadcasted_iota(jnp.int32, (sub, sub), 0)
    col = lax.broadcasted_iota(jnp.int32, (sub, sub), 1)
    tri = jnp.where(col < row, 1.0, 0.0).astype(BF16)
    sel = jnp.where(lax.broadcasted_iota(jnp.int32, (SUBLANES, LANES), 0)
                    == lax.broadcasted_iota(jnp.int32, (SUBLANES, LANES), 1), 1.0, 0.0).astype(BF16)
    neg = jnp.float32(-jnp.inf)
    carry = carry_ref[...]

    for sb in range(tm // sub):
        rs = slice(sb * sub, (sb + 1) * sub)
        mix = _dot(m_ref[0, rs, :], wo_ref[...])
        inv1 = lax.rsqrt(jnp.mean(mix * mix, axis=-1, keepdims=True) + EPS)
        x1 = x_ref[0, rs, :] + (mix * inv1) * (mod_ref[0, 2:3, :] * gp1_ref[...])
        x1_ref[0, rs, :] = x1
        inv2 = lax.rsqrt(jnp.mean(x1 * x1, axis=-1, keepdims=True) + EPS)
        h2 = (x1 * inv2) * (gp2_ref[...] * (1.0 + mod_ref[0, 4:5, :])) + mod_ref[0, 3:4, :]
        h2_ref[0, rs, :] = h2

        h_hi, h_lo = _split_bf16(h2)
        both = _dot(h_hi, wr_both)
        logits = both[:, :LANES] + both[:, LANES:] + _dot(h_lo, wr_hi) + br_ref[...]
        lg = jnp.where(lane < n_groups, logits, neg)
        gmax = jnp.max(lg, axis=-1, keepdims=True)
        p_top = 1.0 / jnp.sum(jnp.exp(lg - gmax), axis=-1, keepdims=True)
        gidx = jnp.min(jnp.where(lg == gmax, lane, LANES), axis=-1, keepdims=True)
        lo = n_groups + gidx * per_group
        le = jnp.where((lane >= lo) & (lane < lo + per_group), logits, neg)
        l1 = jnp.max(le, axis=-1, keepdims=True)
        i1 = jnp.min(jnp.where(le == l1, lane, LANES), axis=-1, keepdims=True)
        le2 = jnp.where(lane == i1, neg, le)
        l2 = jnp.max(le2, axis=-1, keepdims=True)
        i2 = jnp.min(jnp.where(le2 == l2, lane, LANES), axis=-1, keepdims=True)
        r = jnp.exp(l2 - l1)
        w0 = p_top / (1.0 + r)
        w1 = p_top * r / (1.0 + r)

        oh1 = lane == i1
        oh2 = lane == i2
        oh = jnp.where(oh1 | oh2, 1.0, 0.0)
        base = _dot(tri, oh.astype(BF16)) + carry
        rank0 = jnp.sum(jnp.where(oh1, base, 0.0), axis=-1, keepdims=True)
        rank1 = jnp.sum(jnp.where(oh2, base, 0.0), axis=-1, keepdims=True)
        carry = carry + jnp.sum(oh, axis=0, keepdims=True)

        e0 = (i1 - n_groups).astype(F32)
        e1 = (i2 - n_groups).astype(F32)
        rec = jnp.zeros_like(logits)
        for idx, val in ((R_W0, w0), (R_W1, w1)):
            rec = jnp.where(lane == idx, val, rec)
        ri_ref[0, rs, :] = rec

        r0_hi = jnp.floor(rank0 * (1.0 / 256.0))
        r1_hi = jnp.floor(rank1 * (1.0 / 256.0))
        ints = jnp.zeros_like(logits)
        for idx, val in ((T_E0, e0), (T_E1, e1), (T_R0_HI, r0_hi), (T_R0_LO, rank0 - 256.0 * r0_hi),
                         (T_R1_HI, r1_hi), (T_R1_LO, rank1 - 256.0 * r1_hi)):
            ints = jnp.where(lane == idx, val, ints)
        rt = lax.dot_general(sel, ints.astype(BF16), (((1,), (1,)), ((), ())), preferred_element_type=F32)
        rt_ref[:, rs] = rt.astype(jnp.int32)

    carry_ref[...] = carry
    cnt_ref[...] = carry


def _out_route(merged, x, mod, g_post1, g_pre2, w_o_b, w_r, b_r, n_groups, per_group):
    bsz, l, d = x.shape
    tm = _tile(l, TOKEN_TILE)
    tok = lambda b, i: (b, i, 0)
    const2 = lambda b, i: (0, 0)
    return pl.pallas_call(
        functools.partial(_out_kernel, n_groups=n_groups, per_group=per_group),
        grid=(bsz, l // tm),
        in_specs=[pl.BlockSpec((1, tm, d), tok),
                  pl.BlockSpec((1, tm, d), tok),
                  pl.BlockSpec((1, N_MOD, d), lambda b, i: (b, 0, 0)),
                  pl.BlockSpec((1, d), const2),
                  pl.BlockSpec((1, d), const2),
                  pl.BlockSpec((d, d), const2),
                  pl.BlockSpec((d, LANES), const2),
                  pl.BlockSpec((1, LANES), const2)],
        out_specs=[pl.BlockSpec((1, tm, d), tok),
                   pl.BlockSpec((1, tm, d), tok),
                   pl.BlockSpec((1, tm, LANES), tok),
                   pl.BlockSpec((SUBLANES, tm), lambda b, i: (0, b * (l // tm) + i)),
                   pl.BlockSpec((1, LANES), const2)],
        out_shape=[jax.ShapeDtypeStruct((bsz, l, d), F32),
                   jax.ShapeDtypeStruct((bsz, l, d), F32),
                   jax.ShapeDtypeStruct((bsz, l, LANES), F32),
                   jax.ShapeDtypeStruct((SUBLANES, bsz * l), jnp.int32),
                   jax.ShapeDtypeStruct((1, LANES), F32)],
        scratch_shapes=[pltpu.VMEM((1, LANES), F32)],
        compiler_params=_params("arbitrary", "arbitrary"),
        name="out_route",
    )(merged, x, mod, g_post1.reshape(1, d), g_pre2.reshape(1, d), w_o_b, w_r, b_r)


def _dispatch_kernel(pos_ref, zf_ref, h_ref, xs_ref, zbuf, stage, sem, zsem, *, tm, n_c, n_tiles, n_tok, n_steps):
    i = pl.program_id(0)
    tile_rows = EXPERT_TILE * n_c
    prev = (i + 1) % 2

    def zero_copy(t):
        return pltpu.make_async_copy(zbuf, xs_ref.at[pl.ds(pl.multiple_of(t * tile_rows, tile_rows), tile_rows)], zsem)

    @pl.when(i == 0)
    def _():
        zbuf[...] = jnp.zeros_like(zbuf)

        def issue(t, c):
            @pl.when(zf_ref[t] != 0)
            def _():
                zero_copy(t).start()
            return c

        def drain(t, c):
            @pl.when(zf_ref[t] != 0)
            def _():
                zero_copy(t).wait()
            return c

        lax.fori_loop(0, n_tiles, issue, 0)
        lax.fori_loop(0, n_tiles, drain, 0)

    def issue_rows(rb, c):
        for u in range(DMA_UNROLL):
            r = rb * DMA_UNROLL + u
            src = stage.at[prev, pl.ds(pl.multiple_of(r * n_c, n_c), n_c)]
            for k in range(TOP_K):
                p = pos_ref[k * n_tok + (i - 1) * tm + r]
                pltpu.make_async_copy(src, xs_ref.at[pl.ds(pl.multiple_of(p * n_c, n_c), n_c)],
                                      sem).start(priority=k % 2)
        return c

    @pl.when(i >= 1)
    def _():
        lax.fori_loop(0, tm // DMA_UNROLL, issue_rows, 0)

    @pl.when(i < n_steps)
    def _():
        _store_row_tiled(stage.at[i % 2], 0, h_ref[...])

    @pl.when(i >= 1)
    def _():
        for k in range(TOP_K):
            pltpu.make_async_copy(stage.at[prev], xs_ref.at[pl.ds(0, tm * n_c)], sem).wait()


def _dispatch(h2, pos, zflag, n_rows):
    n, d = h2.shape
    n_c = d // LANES
    tm = _tile(n, DISPATCH_TILE)
    n_steps = n // tm
    n_tiles = n_rows // EXPERT_TILE
    return pl.pallas_call(
        functools.partial(_dispatch_kernel, tm=tm, n_c=n_c, n_tiles=n_tiles, n_tok=n, n_steps=n_steps),
        grid_spec=pltpu.PrefetchScalarGridSpec(
            num_scalar_prefetch=2,
            grid=(n_steps + 1,),
            in_specs=[pl.BlockSpec((tm, d), lambda i, p, z: (jnp.minimum(i, n_steps - 1), 0))],
            out_specs=pl.BlockSpec(memory_space=pl.ANY),
            scratch_shapes=[pltpu.VMEM((EXPERT_TILE * n_c, LANES), F32),
                            pltpu.VMEM((2, tm * n_c, LANES), F32),
                            pltpu.SemaphoreType.DMA(()),
                            pltpu.SemaphoreType.DMA(())]),
        out_shape=jax.ShapeDtypeStruct((n_rows * n_c, LANES), F32),
        compiler_params=_params("arbitrary"),
        name="dispatch",
    )(pos, zflag, h2)


def _cast_rows(src_ref, dst_ref, chunk=256):
    rows = dst_ref.shape[0]
    chunk = min(chunk, rows)

    def body(c, carry):
        r0 = pl.multiple_of(c * chunk, chunk)
        dst_ref[pl.ds(r0, chunk), :] = src_ref[pl.ds(r0, chunk), :].astype(BF16)
        return carry

    lax.fori_loop(0, rows // chunk, body, 0)


def _experts_kernel(te_ref, chg_ref, nxt_ref, nt_ref, xs_ref, wg_hbm, wu_hbm, wd_hbm, o_ref,
                    wf_g, wf_u, wf_d, wb_g, wb_u, wb_d, sem):
    t = pl.program_id(0)
    stages = ((wg_hbm, wf_g, wb_g), (wu_hbm, wf_u, wb_u), (wd_hbm, wf_d, wb_d))

    def fetch(e):
        return [pltpu.make_async_copy(src.at[e], dst, sem) for src, dst, _ in stages]

    @pl.when(t < nt_ref[0])
    def _():
        @pl.when(chg_ref[t] != 0)
        def _():
            @pl.when(t == 0)
            def _():
                for cp in fetch(te_ref[0]):
                    cp.start(priority=1)

            for cp in fetch(te_ref[t]):
                cp.wait()
            for _, wf, wb in stages:
                _cast_rows(wf, wb)

            @pl.when(nxt_ref[t] >= 0)
            def _():
                for cp in fetch(nxt_ref[t]):
                    cp.start(priority=1)

        x = _load_row_tiled(xs_ref, EXPERT_TILE, wb_g.shape[0] // LANES).astype(BF16)
        g = _dot(x, wb_g[...])
        u = _dot(x, wb_u[...])
        hid = (g * jax.nn.sigmoid(g) * u).astype(BF16)
        o_ref[...] = _dot(hid, wb_d[...])

    @pl.when(t >= nt_ref[0])
    def _():
        o_ref[...] = jnp.zeros_like(o_ref)


def _experts(xs, w_gate, w_up, w_down, plan):
    d, de = w_gate.shape[1], w_gate.shape[2]
    n_c = d // LANES
    n_rows = xs.shape[0] // n_c
    tile = lambda t, *_: (t, 0)
    hbm = pl.BlockSpec(memory_space=pl.ANY)
    return pl.pallas_call(
        _experts_kernel,
        grid_spec=pltpu.PrefetchScalarGridSpec(
            num_scalar_prefetch=len(plan),
            grid=(n_rows // EXPERT_TILE,),
            in_specs=[pl.BlockSpec((EXPERT_TILE * n_c, LANES), tile), hbm, hbm, hbm],
            out_specs=pl.BlockSpec((EXPERT_TILE, d), tile),
            scratch_shapes=[pltpu.VMEM((d, de), F32), pltpu.VMEM((d, de), F32), pltpu.VMEM((de, d), F32),
                            pltpu.VMEM((d, de), BF16), pltpu.VMEM((d, de), BF16), pltpu.VMEM((de, d), BF16),
                            pltpu.SemaphoreType.DMA(())]),
        out_shape=jax.ShapeDtypeStruct((n_rows, d), F32),
        compiler_params=_params("arbitrary"),
        name="experts",
    )(*plan, xs, w_gate, w_up, w_down)


def _combine_kernel(pos_ref, x1_ref, ri_ref, mod_ref, gp_ref, ys_ref, o_ref, buf, sem, *, tm, n_l, n_tok):
    step = pl.program_id(0) * n_l + pl.program_id(1)
    n_steps = pl.num_programs(0) * n_l
    slot = step % 2

    def issue(s, sl):
        def body(rb, c):
            for u in range(DMA_UNROLL):
                r = rb * DMA_UNROLL + u
                for k in range(TOP_K):
                    p = pos_ref[k * n_tok + s * tm + r]
                    pltpu.make_async_copy(ys_ref.at[pl.ds(p, 1)], buf.at[sl, k, rb, pl.ds(u, 1)],
                                          sem.at[sl]).start(priority=k % 2)
            return c

        lax.fori_loop(0, tm // DMA_UNROLL, body, 0)

    @pl.when(step == 0)
    def _():
        issue(0, 0)

    @pl.when(step + 1 < n_steps)
    def _():
        issue(step + 1, 1 - slot)

    for k in range(TOP_K):
        pltpu.make_async_copy(buf.at[slot, k], buf.at[slot, k], sem.at[slot]).wait()

    ri = ri_ref[0]
    d = o_ref.shape[-1]
    moe = (ri[:, R_W0:R_W0 + 1] * buf[slot, 0].reshape(tm, d)
           + ri[:, R_W1:R_W1 + 1] * buf[slot, 1].reshape(tm, d))
    o_ref[0] = x1_ref[0] + mod_ref[0, 5:6, :] * _rms(moe, gp_ref[...])


def _combine(x1, rinfo, mod, g_post2, ys, pos):
    bsz, l, d = x1.shape
    tm = _tile(l, COMBINE_TILE)
    n_l = l // tm
    tok = lambda b, i, p: (b, i, 0)
    return pl.pallas_call(
        functools.partial(_combine_kernel, tm=tm, n_l=n_l, n_tok=bsz * l),
        grid_spec=pltpu.PrefetchScalarGridSpec(
            num_scalar_prefetch=1,
            grid=(bsz, n_l),
            in_specs=[pl.BlockSpec((1, tm, d), tok),
                      pl.BlockSpec((1, tm, LANES), tok),
                      pl.BlockSpec((1, N_MOD, d), lambda b, i, p: (b, 0, 0)),
                      pl.BlockSpec((1, d), lambda b, i, p: (0, 0)),
                      pl.BlockSpec(memory_space=pl.ANY)],
            out_specs=pl.BlockSpec((1, tm, d), tok),
            scratch_shapes=[pltpu.VMEM((2, TOP_K, tm // DMA_UNROLL, DMA_UNROLL, d), F32),
                            pltpu.SemaphoreType.DMA((2,))]),
        out_shape=jax.ShapeDtypeStruct((bsz, l, d), F32),
        compiler_params=_params("arbitrary", "arbitrary"),
        name="combine",
    )(pos, x1, rinfo, mod, g_post2.reshape(1, d), ys)


def _rope_tables(seq, head_dim):
    axis_dim = head_dim // 2
    t = jnp.arange(seq, dtype=jnp.int32)
    pos = jnp.stack([t // GRID_W, t % GRID_W], axis=-1).astype(F32)
    inv_freq = ROPE_THETA ** (-jnp.arange(0, axis_dim, 2, dtype=F32) / axis_dim)
    ang = pos[:, :, None] * inv_freq
    cos, sin = jnp.cos(ang), jnp.sin(ang)
    cos_t = jnp.concatenate([cos, cos], axis=-1).reshape(seq, head_dim)
    sin_t = jnp.concatenate([-sin, sin], axis=-1).reshape(seq, head_dim)
    return cos_t, sin_t


def _route_plan(rt, cnt, n_groups, n_experts, n_tiles):
    e = rt[T_E0:T_E1 + 1]
    rank = jnp.stack([rt[T_R0_HI] * 256 + rt[T_R0_LO], rt[T_R1_HI] * 256 + rt[T_R1_LO]])
    counts = cnt[0, n_groups:n_groups + n_experts].astype(jnp.int32)
    tiles_e = (counts + EXPERT_TILE - 1) // EXPERT_TILE
    ids = jnp.arange(n_experts, dtype=jnp.int32)
    tile_end = jnp.sum(jnp.where(ids[None, :] <= ids[:, None], tiles_e[None, :], 0), axis=1)
    tile_start = tile_end - tiles_e
    nt = tile_end[-1]
    row0 = tile_start * EXPERT_TILE
    pos = jnp.sum(jnp.where(e[None] == ids[:, None, None], row0[:, None, None], 0), axis=0) + rank
    t = jnp.arange(n_tiles, dtype=jnp.int32)
    owner = lambda q: jnp.sum((tile_end[None, :] <= q[:, None]).astype(jnp.int32), axis=1)
    te = owner(jnp.minimum(t, nt - 1))
    chg = ((t == 0) | (te != owner(jnp.minimum(jnp.maximum(t - 1, 0), nt - 1)))).astype(jnp.int32)
    partial_last = jnp.any((tile_end[None, :] - 1 == t[:, None]) & (counts[None, :] % EXPERT_TILE != 0), axis=1)
    zflag = ((t >= nt) | partial_last).astype(jnp.int32)
    used = tiles_e > 0
    later = used[None, :] & (ids[None, :] > te[:, None])
    nxt = jnp.min(jnp.where(later, ids[None, :], n_experts), axis=1)
    nxt = jnp.where(nxt == n_experts, -1, nxt)
    return pos.reshape(-1), (te, chg, nxt, nt.reshape(1)), zflag


def kernel(x, c, ctx, c_ctx, w_ada, b_ada, g_pre1, g_post1, g_pre2, g_post2, w_in, q_norm, k_norm,
           gm_ln, w_s, b_s, w_ba, w_bg, w_o, w_rg, b_rg, w_re, b_re, w_gate, w_up, w_down):
    bsz, seq, d = x.shape
    depth = w_ada.shape[0]
    head_dim = q_norm.shape[-1]
    q_w, gm_w = w_ba.shape[1], w_bg.shape[1]
    kv_w = (w_in.shape[2] - q_w - 2 * gm_w - 2 * d) // 2
    n_groups, per_group = w_re.shape[2], w_re.shape[3]
    n_experts = n_groups * per_group
    assert head_dim == LANES and w_s.shape[2] == LANES and gm_w // w_s.shape[1] == LANES
    assert n_groups + n_experts <= LANES and seq % GRID_W == 0
    col_q = 2 * kv_w
    col_u, col_v = col_q + q_w, col_q + q_w + gm_w
    col_ga, col_gg = col_v + gm_w, col_v + gm_w + d
    n_tok = bsz * seq
    n_rows = n_tok * TOP_K + n_experts * EXPERT_TILE
    n_tiles = n_rows // EXPERT_TILE

    cos_t, sin_t = _rope_tables(seq, head_dim)
    pad = (-(bsz + 1)) % (2 * SUBLANES)
    cs = jnp.concatenate([c, c_ctx[None, :], jnp.zeros((pad, d), F32)], axis=0)

    for l in range(depth):
        assert l + 1 == depth, "context-stream update for non-final layers is not implemented"
        mod = _ada(cs, w_ada[l], b_ada[l]).reshape(cs.shape[0], N_MOD, d)
        w_in_b = w_in[l].astype(BF16)
        w_ba_b, w_bg_b, w_o_b = w_ba[l].astype(BF16), w_bg[l].astype(BF16), w_o[l].astype(BF16)
        bs_full = jnp.repeat(b_s[l].T, LANES, axis=1)
        w_r = jnp.concatenate([w_rg[l], w_re[l].reshape(d, n_experts),
                               jnp.zeros((d, LANES - n_groups - n_experts), F32)], axis=1)
        b_r = jnp.concatenate([b_rg[l], b_re[l].reshape(n_experts),
                               jnp.zeros((LANES - n_groups - n_experts,), F32)]).reshape(1, LANES)

        kc, vc = _ctx_kv(ctx, mod, bsz, g_pre1[l], w_in_b, k_norm[l], kv_w)
        hx, kx, vx, qx, gm = _inproj(x, mod, g_pre1[l], w_in_b, k_norm[l], q_norm[l], cos_t, sin_t, gm_ln[l],
                                     w_s[l], bs_full, kv_w, q_w, gm_w, head_dim ** -0.5 * LOG2_E)
        attn = _attention(qx, kc, vc, kx, vx, kv_w // head_dim)
        merged = _merge(hx, attn, gm, w_in_b, w_ba_b, w_bg_b, col_ga, col_gg)

        x1, h2, rinfo, rt, cnt = _out_route(merged, x, mod, g_post1[l], g_pre2[l], w_o_b, w_r, b_r,
                                            n_groups, per_group)
        pos, plan, zflag = _route_plan(rt, cnt, n_groups, n_experts, n_tiles)

        xs = _dispatch(h2.reshape(n_tok, d), pos, zflag, n_rows)
        ys = _experts(xs, w_gate[l], w_up[l], w_down[l], plan)
        x = _combine(x1, rinfo, mod, g_post2[l], ys, pos)
    return x
```

```python
import functools

import jax
import jax.numpy as jnp
from jax import lax
from jax.experimental import pallas as pl
from jax.experimental.pallas import tpu as pltpu

GRID_W = 64
ROPE_THETA = 10000.0
EPS = 1e-6
N_MOD = 6
TOP_K = 2
LOG2_E = 1.4426950408889634

LANES = 128
SUBLANES = 8
VMEM_LIMIT_BYTES = 56 * 1024 * 1024

EXPERT_TILE = 256
TOKEN_TILE = 512
ATTN_Q_TILE = 1024
DISPATCH_TILE = 512
COMBINE_TILE = 256
ADA_COL_TILE = 1536
COL_CHUNK = 512
DMA_UNROLL = SUBLANES

F32 = jnp.float32
BF16 = jnp.bfloat16


def _params(*sem):
    return pltpu.CompilerParams(dimension_semantics=sem, vmem_limit_bytes=VMEM_LIMIT_BYTES)


def _dot(a, b):
    return jnp.dot(a, b, preferred_element_type=F32)


def _split_bf16(a):
    hi = a.astype(BF16)
    lo = (a - hi.astype(F32)).astype(BF16)
    return hi, lo


def _dot3(a, w):
    a_hi, a_lo = _split_bf16(a)
    w_hi, w_lo = _split_bf16(w)
    return _dot(a_hi, w_hi) + _dot(a_lo, w_hi) + _dot(a_hi, w_lo)


def _rms(x, g):
    return x * lax.rsqrt(jnp.mean(x * x, axis=-1, keepdims=True) + EPS) * g


def _store_row_tiled(ref, t0, val):
    rows, d = val.shape
    n_c = d // LANES
    for c in range(n_c):
        ref[pl.ds(t0 * n_c + c, rows, stride=n_c), :] = val[:, c * LANES:(c + 1) * LANES]


def _load_row_tiled(ref, rows, n_c):
    return jnp.concatenate([ref[pl.ds(c, rows, stride=n_c), :] for c in range(n_c)], axis=1)


def _tile(n, pref):
    t = min(n, pref)
    while n % t:
        t //= 2
    return t


def _ada_kernel(c_ref, w_ref, b_ref, o_ref):
    c = c_ref[...]
    a = c * jax.nn.sigmoid(c)
    o_ref[...] = _dot3(a, w_ref[...]) + b_ref[...]


def _ada(cs, w, b):
    m, d = cs.shape
    n = w.shape[1]
    tn = _tile(n, ADA_COL_TILE)
    return pl.pallas_call(
        _ada_kernel,
        grid=(n // tn,),
        in_specs=[pl.BlockSpec((m, d), lambda j: (0, 0)),
                  pl.BlockSpec((d, tn), lambda j: (0, j)),
                  pl.BlockSpec((1, tn), lambda j: (0, j))],
        out_specs=pl.BlockSpec((m, tn), lambda j: (0, j)),
        out_shape=jax.ShapeDtypeStruct((m, n), F32),
        compiler_params=_params("arbitrary"),
        name="ada",
    )(cs, w, b.reshape(1, n))


def _swap32(x):
    lane = lax.broadcasted_iota(jnp.int32, x.shape, 1)
    fwd = pltpu.roll(x, LANES - 32, 1)
    bwd = pltpu.roll(x, 32, 1)
    return jnp.where((lane & 32) == 0, fwd, bwd)


def _norm_head(r, gain, cos, sin, scale):
    y = _rms(r, gain)
    if cos is not None:
        y = y * cos + _swap32(y) * sin
    if scale != 1.0:
        y = y * scale
    return y.astype(BF16)


def _ctx_kv_kernel(x_ref, mod_ref, g_ref, w_ref, kn_ref, k_ref, v_ref, *, n_kv):
    x = x_ref[0]
    inv = lax.rsqrt(jnp.mean(x * x, axis=-1, keepdims=True) + EPS)
    h = ((x * inv) * (g_ref[...] * (1.0 + mod_ref[0, 1:2, :])) + mod_ref[0, 0:1, :]).astype(BF16)
    r = _dot(h, w_ref[...])
    for hh in range(n_kv):
        sl = slice(hh * LANES, (hh + 1) * LANES)
        k_ref[0, :, sl] = _norm_head(r[:, sl], kn_ref[...], None, None, 1.0)
    v_ref[0] = r[:, n_kv * LANES:].astype(BF16)


def _ctx_kv(ctx, mod, mod_row, g_pre, w_in_b, k_norm, kv_w):
    bsz, l, d = ctx.shape
    tm = _tile(l, TOKEN_TILE)
    out = jax.ShapeDtypeStruct((bsz, l, kv_w), BF16)
    tok = lambda b, i: (b, i, 0)
    return pl.pallas_call(
        functools.partial(_ctx_kv_kernel, n_kv=kv_w // LANES),
        grid=(bsz, l // tm),
        in_specs=[pl.BlockSpec((1, tm, d), tok),
                  pl.BlockSpec((1, N_MOD, d), lambda b, i: (mod_row, 0, 0)),
                  pl.BlockSpec((1, d), lambda b, i: (0, 0)),
                  pl.BlockSpec((d, 2 * kv_w), lambda b, i: (0, 0)),
                  pl.BlockSpec((1, LANES), lambda b, i: (0, 0))],
        out_specs=[pl.BlockSpec((1, tm, kv_w), tok)] * 2,
        out_shape=[out, out],
        compiler_params=_params("arbitrary", "arbitrary"),
        name="ctx_kv",
    )(ctx, mod, g_pre.reshape(1, d), w_in_b, k_norm.reshape(1, LANES))


def _gelu(x):
    c = 0.7978845608028654
    return x * (0.5 + 0.5 * jnp.tanh(x * (c + (c * 0.044715) * (x * x))))


def _inproj_kernel(x_ref, mod_ref, g_ref, w_ref, kn_ref, qn_ref, cos_ref, sin_ref, ln_ref, ws_ref, bs_ref,
                   h_ref, k_ref, v_ref, q_ref, gm_ref, *, kv_w, q_w, gm_w, scale):
    x = x_ref[0]
    inv = lax.rsqrt(jnp.mean(x * x, axis=-1, keepdims=True) + EPS)
    h = ((x * inv) * (g_ref[...] * (1.0 + mod_ref[0, 1:2, :])) + mod_ref[0, 0:1, :]).astype(BF16)
    h_ref[0] = h
    cos, sin = cos_ref[...], sin_ref[...]
    tm = h.shape[0]

    rk = _dot(h, w_ref[:, 0:kv_w])
    for hh in range(kv_w // LANES):
        sl = slice(hh * LANES, (hh + 1) * LANES)
        k_ref[0, :, sl] = _norm_head(rk[:, sl], kn_ref[...], cos, sin, 1.0)
    v_ref[0] = _dot(h, w_ref[:, kv_w:2 * kv_w]).astype(BF16)

    col_q = 2 * kv_w
    cq = min(COL_CHUNK, q_w)
    for j in range(q_w // cq):
        r = _dot(h, w_ref[:, col_q + j * cq:col_q + (j + 1) * cq])
        for hh in range(cq // LANES):
            sl = slice(hh * LANES, (hh + 1) * LANES)
            q_ref[0, :, j * cq + hh * LANES:j * cq + (hh + 1) * LANES] = _norm_head(
                r[:, sl], qn_ref[...], cos, sin, scale)

    col_u, col_v = col_q + q_w, col_q + q_w + gm_w
    cg = min(COL_CHUNK, gm_w)
    for j in range(gm_w // cg):
        gu = _gelu(_dot(h, w_ref[:, col_u + j * cg:col_u + (j + 1) * cg]))
        gv = _gelu(_dot(h, w_ref[:, col_v + j * cg:col_v + (j + 1) * cg]))
        for g in range(cg // LANES):
            cs = slice(g * LANES, (g + 1) * LANES)
            oc = slice(j * cg + g * LANES, j * cg + (g + 1) * LANES)
            v = gv[:, cs]
            vc = v - jnp.mean(v, axis=-1, keepdims=True)
            vn = vc * lax.rsqrt(jnp.mean(vc * vc, axis=-1, keepdims=True) + EPS) * ln_ref[:, oc]
            vn = vn.astype(BF16)
            w = ws_ref[j * (cg // LANES) + g].astype(BF16)
            for c in range(tm // LANES):
                rs = slice(c * LANES, (c + 1) * LANES)
                s = _dot(w, vn[rs, :]) + bs_ref[:, oc]
                gm_ref[0, rs, oc] = (gu[rs, cs] * s).astype(BF16)


def _inproj(x, mod, g_pre, w_in_b, k_norm, q_norm, cos, sin, gm_ln, w_s, bs_full, kv_w, q_w, gm_w, scale):
    bsz, l, d = x.shape
    tm = _tile(l, TOKEN_TILE)
    n_cols = 2 * kv_w + q_w + 2 * gm_w
    assert tm % LANES == 0
    tok = lambda b, i: (b, i, 0)
    c2 = lambda b, i: (0, 0)
    return pl.pallas_call(
        functools.partial(_inproj_kernel, kv_w=kv_w, q_w=q_w, gm_w=gm_w, scale=scale),
        grid=(bsz, l // tm),
        in_specs=[pl.BlockSpec((1, tm, d), tok),
                  pl.BlockSpec((1, N_MOD, d), lambda b, i: (b, 0, 0)),
                  pl.BlockSpec((1, d), c2),
                  pl.BlockSpec((d, n_cols), c2, pipeline_mode=pl.Buffered(1)),
                  pl.BlockSpec((1, LANES), c2),
                  pl.BlockSpec((1, LANES), c2),
                  pl.BlockSpec((tm, LANES), lambda b, i: (i, 0)),
                  pl.BlockSpec((tm, LANES), lambda b, i: (i, 0)),
                  pl.BlockSpec((1, gm_w), c2),
                  pl.BlockSpec(w_s.shape, lambda b, i: (0, 0, 0)),
                  pl.BlockSpec(bs_full.shape, c2)],
        out_specs=[pl.BlockSpec((1, tm, d), tok),
                   pl.BlockSpec((1, tm, kv_w), tok),
                   pl.BlockSpec((1, tm, kv_w), tok),
                   pl.BlockSpec((1, tm, q_w), tok),
                   pl.BlockSpec((1, tm, gm_w), tok)],
        out_shape=[jax.ShapeDtypeStruct((bsz, l, d), BF16),
                   jax.ShapeDtypeStruct((bsz, l, kv_w), BF16),
                   jax.ShapeDtypeStruct((bsz, l, kv_w), BF16),
                   jax.ShapeDtypeStruct((bsz, l, q_w), BF16),
                   jax.ShapeDtypeStruct((bsz, l, gm_w), BF16)],
        compiler_params=_params("arbitrary", "arbitrary"),
        name="inproj",
    )(x, mod, g_pre.reshape(1, d), w_in_b, k_norm.reshape(1, LANES), q_norm.reshape(1, LANES), cos, sin,
      gm_ln.reshape(1, gm_w), w_s, bs_full)


def _attn_kernel(q_ref, kc_ref, vc_ref, kx_ref, vx_ref, o_ref, *, n_kv, grp):
    nt = (((1,), (1,)), ((), ()))
    for kv in range(n_kv):
        ks = slice(kv * LANES, (kv + 1) * LANES)
        kc, kx = kc_ref[0, :, ks], kx_ref[0, :, ks]
        vc = jnp.concatenate([vc_ref[0, :, ks], jnp.ones_like(kc)], axis=1)
        vx = jnp.concatenate([vx_ref[0, :, ks], jnp.ones_like(kx)], axis=1)
        for hh in range(grp):
            sl = slice((kv * grp + hh) * LANES, (kv * grp + hh + 1) * LANES)
            q = q_ref[0, :, sl]
            sc = lax.dot_general(q, kc, nt, preferred_element_type=F32)
            sx = lax.dot_general(q, kx, nt, preferred_element_type=F32)
            m = jnp.maximum(jnp.max(sc, axis=-1, keepdims=True), jnp.max(sx, axis=-1, keepdims=True))
            pc = jnp.exp2(sc - m).astype(BF16)
            px = jnp.exp2(sx - m).astype(BF16)
            o = _dot(pc, vc) + _dot(px, vx)
            o_ref[0, :, sl] = (o[:, :LANES] / o[:, LANES:LANES + 1]).astype(BF16)


def _attention(q, kc, vc, kx, vx, n_kv):
    bsz, l, q_w = q.shape
    lc, kv_w = kc.shape[1], kc.shape[2]
    tq = _tile(l, ATTN_Q_TILE)
    tok = lambda b, i: (b, i, 0)
    whole = lambda b, i: (b, 0, 0)
    return pl.pallas_call(
        functools.partial(_attn_kernel, n_kv=n_kv, grp=q_w // kv_w),
        grid=(bsz, l // tq),
        in_specs=[pl.BlockSpec((1, tq, q_w), tok),
                  pl.BlockSpec((1, lc, kv_w), whole),
                  pl.BlockSpec((1, lc, kv_w), whole),
                  pl.BlockSpec((1, l, kv_w), whole),
                  pl.BlockSpec((1, l, kv_w), whole)],
        out_specs=pl.BlockSpec((1, tq, q_w), tok),
        out_shape=jax.ShapeDtypeStruct((bsz, l, q_w), BF16),
        compiler_params=_params("arbitrary", "arbitrary"),
        name="attention",
    )(q, kc, vc, kx, vx)


def _merge_kernel(h_ref, a_ref, g_ref, *refs, n_chunk):
    wga, wgg = refs[:n_chunk], refs[n_chunk:2 * n_chunk]
    wba_ref, wbg_ref, o_ref = refs[2 * n_chunk:]
    h, a, g = h_ref[0], a_ref[0], g_ref[0]
    tn = wga[0].shape[1]
    for j in range(n_chunk):
        cs = slice(j * tn, (j + 1) * tn)
        ga = jax.nn.sigmoid(_dot(h, wga[j][...]))
        gg = jax.nn.sigmoid(_dot(h, wgg[j][...]))
        o_ref[0, :, cs] = (ga * _dot(a, wba_ref[:, cs]) + gg * _dot(g, wbg_ref[:, cs])).astype(BF16)


def _merge(h, attn, gm, w_in_b, w_ba_b, w_bg_b, col_ga, col_gg):
    bsz, l, d = h.shape
    q_w, gm_w = attn.shape[2], gm.shape[2]
    tm = _tile(l, TOKEN_TILE)
    tn = _tile(d, COL_CHUNK)
    assert col_ga % tn == 0 and col_gg % tn == 0
    n_chunk = d // tn
    tok = lambda b, i: (b, i, 0)
    gate_specs = [pl.BlockSpec((d, tn), functools.partial(lambda b, i, c: (0, c), c=(c0 // tn) + j),
                               pipeline_mode=pl.Buffered(1))
                  for c0 in (col_ga, col_gg) for j in range(n_chunk)]
    return pl.pallas_call(
        functools.partial(_merge_kernel, n_chunk=n_chunk),
        grid=(bsz, l // tm),
        in_specs=[pl.BlockSpec((1, tm, d), tok),
                  pl.BlockSpec((1, tm, q_w), tok),
                  pl.BlockSpec((1, tm, gm_w), tok),
                  *gate_specs,
                  pl.BlockSpec((q_w, d), lambda b, i: (0, 0)),
                  pl.BlockSpec((gm_w, d), lambda b, i: (0, 0))],
        out_specs=pl.BlockSpec((1, tm, d), tok),
        out_shape=jax.ShapeDtypeStruct((bsz, l, d), BF16),
        compiler_params=_params("arbitrary", "arbitrary"),
        name="merge",
    )(h, attn, gm, *([w_in_b] * (2 * n_chunk)), w_ba_b, w_bg_b)


R_W0, R_W1 = range(2)
T_E0, T_E1, T_R0_HI, T_R0_LO, T_R1_HI, T_R1_LO = range(6)


def _out_kernel(m_ref, x_ref, mod_ref, gp1_ref, gp2_ref, wo_ref, wr_ref, br_ref,
                x1_ref, h2_ref, ri_ref, rt_ref, cnt_ref, carry_ref, *, n_groups, per_group):
    @pl.when((pl.program_id(0) == 0) & (pl.program_id(1) == 0))
    def _():
        carry_ref[...] = jnp.zeros_like(carry_ref)

    tm = m_ref.shape[1]
    sub = tm
    wr_hi, wr_lo = _split_bf16(wr_ref[...])
    wr_both = jnp.concatenate([wr_hi, wr_lo], axis=1)
    lane = lax.broadcasted_iota(jnp.int32, (sub, LANES), 1)
    row = lax.broadcasted_iota(jnp.int32, (sub, sub), 0)
    col = lax.broadcasted_iota(jnp.int32, (sub, sub), 1)
    tri = jnp.where(col < row, 1.0, 0.0).astype(BF16)
    sel = jnp.where(lax.broadcasted_iota(jnp.int32, (SUBLANES, LANES), 0)
                    == lax.broadcasted_iota(jnp.int32, (SUBLANES, LANES), 1), 1.0, 0.0).astype(BF16)
    neg = jnp.float32(-jnp.inf)
    carry = carry_ref[...]

    for sb in range(tm // sub):
        rs = slice(sb * sub, (sb + 1) * sub)
        mix = _dot(m_ref[0, rs, :], wo_ref[...])
        inv1 = lax.rsqrt(jnp.mean(mix * mix, axis=-1, keepdims=True) + EPS)
        x1 = x_ref[0, rs, :] + (mix * inv1) * (mod_ref[0, 2:3, :] * gp1_ref[...])
        x1_ref[0, rs, :] = x1
        inv2 = lax.rsqrt(jnp.mean(x1 * x1, axis=-1, keepdims=True) + EPS)
        h2 = (x1 * inv2) * (gp2_ref[...] * (1.0 + mod_ref[0, 4:5, :])) + mod_ref[0, 3:4, :]
        h2_ref[0, rs, :] = h2

        h_hi, h_lo = _split_bf16(h2)
        both = _dot(h_hi, wr_both)
        logits = both[:, :LANES] + both[:, LANES:] + _dot(h_lo, wr_hi) + br_ref[...]
        lg = jnp.where(lane < n_groups, logits, neg)
        gmax = jnp.max(lg, axis=-1, keepdims=True)
        p_top = 1.0 / jnp.sum(jnp.exp(lg - gmax), axis=-1, keepdims=True)
        gidx = jnp.min(jnp.where(lg == gmax, lane, LANES), axis=-1, keepdims=True)
        lo = n_groups + gidx * per_group
        le = jnp.where((lane >= lo) & (lane < lo + per_group), logits, neg)
        l1 = jnp.max(le, axis=-1, keepdims=True)
        i1 = jnp.min(jnp.where(le == l1, lane, LANES), axis=-1, keepdims=True)
        le2 = jnp.where(lane == i1, neg, le)
        l2 = jnp.max(le2, axis=-1, keepdims=True)
        i2 = jnp.min(jnp.where(le2 == l2, lane, LANES), axis=-1, keepdims=True)
        r = jnp.exp(l2 - l1)
        w0 = p_top / (1.0 + r)
        w1 = p_top * r / (1.0 + r)

        oh1 = lane == i1
        oh2 = lane == i2
        oh = jnp.where(oh1 | oh2, 1.0, 0.0)
        base = _dot(tri, oh.astype(BF16)) + carry
        rank0 = jnp.sum(jnp.where(oh1, base, 0.0), axis=-1, keepdims=True)
        rank1 = jnp.sum(jnp.where(oh2, base, 0.0), axis=-1, keepdims=True)
        carry = carry + jnp.sum(oh, axis=0, keepdims=True)

        e0 = (i1 - n_groups).astype(F32)
        e1 = (i2 - n_groups).astype(F32)
        rec = jnp.zeros_like(logits)
        for idx, val in ((R_W0, w0), (R_W1, w1)):
            rec = jnp.where(lane == idx, val, rec)
        ri_ref[0, rs, :] = rec

        r0_hi = jnp.floor(rank0 * (1.0 / 256.0))
        r1_hi = jnp.floor(rank1 * (1.0 / 256.0))
        ints = jnp.zeros_like(logits)
        for idx, val in ((T_E0, e0), (T_E1, e1), (T_R0_HI, r0_hi), (T_R0_LO, rank0 - 256.0 * r0_hi),
                         (T_R1_HI, r1_hi), (T_R1_LO, rank1 - 256.0 * r1_hi)):
            ints = jnp.where(lane == idx, val, ints)
        rt = lax.dot_general(sel, ints.astype(BF16), (((1,), (1,)), ((), ())), preferred_element_type=F32)
        rt_ref[:, rs] = rt.astype(jnp.int32)

    carry_ref[...] = carry
    cnt_ref[...] = carry


def _out_route(merged, x, mod, g_post1, g_pre2, w_o_b, w_r, b_r, n_groups, per_group):
    bsz, l, d = x.shape
    tm = _tile(l, TOKEN_TILE)
    tok = lambda b, i: (b, i, 0)
    const2 = lambda b, i: (0, 0)
    return pl.pallas_call(
        functools.partial(_out_kernel, n_groups=n_groups, per_group=per_group),
        grid=(bsz, l // tm),
        in_specs=[pl.BlockSpec((1, tm, d), tok),
                  pl.BlockSpec((1, tm, d), tok),
                  pl.BlockSpec((1, N_MOD, d), lambda b, i: (b, 0, 0)),
                  pl.BlockSpec((1, d), const2),
                  pl.BlockSpec((1, d), const2),
                  pl.BlockSpec((d, d), const2),
                  pl.BlockSpec((d, LANES), const2),
                  pl.BlockSpec((1, LANES), const2)],
        out_specs=[pl.BlockSpec((1, tm, d), tok),
                   pl.BlockSpec((1, tm, d), tok),
                   pl.BlockSpec((1, tm, LANES), tok),
                   pl.BlockSpec((SUBLANES, tm), lambda b, i: (0, b * (l // tm) + i)),
                   pl.BlockSpec((1, LANES), const2)],
        out_shape=[jax.ShapeDtypeStruct((bsz, l, d), F32),
                   jax.ShapeDtypeStruct((bsz, l, d), F32),
                   jax.ShapeDtypeStruct((bsz, l, LANES), F32),
                   jax.ShapeDtypeStruct((SUBLANES, bsz * l), jnp.int32),
                   jax.ShapeDtypeStruct((1, LANES), F32)],
        scratch_shapes=[pltpu.VMEM((1, LANES), F32)],
        compiler_params=_params("arbitrary", "arbitrary"),
        name="out_route",
    )(merged, x, mod, g_post1.reshape(1, d), g_pre2.reshape(1, d), w_o_b, w_r, b_r)


def _dispatch_kernel(pos_ref, zf_ref, h_ref, xs_ref, zbuf, stage, sem, zsem, *, tm, n_c, n_tiles, n_tok, n_steps):
    i = pl.program_id(0)
    tile_rows = EXPERT_TILE * n_c
    prev = (i + 1) % 2

    def zero_copy(t):
        return pltpu.make_async_copy(zbuf, xs_ref.at[pl.ds(pl.multiple_of(t * tile_rows, tile_rows), tile_rows)], zsem)

    @pl.when(i == 0)
    def _():
        zbuf[...] = jnp.zeros_like(zbuf)

        def issue(t, c):
            @pl.when(zf_ref[t] != 0)
            def _():
                zero_copy(t).start()
            return c

        def drain(t, c):
            @pl.when(zf_ref[t] != 0)
            def _():
                zero_copy(t).wait()
            return c

        lax.fori_loop(0, n_tiles, issue, 0)
        lax.fori_loop(0, n_tiles, drain, 0)

    def issue_rows(rb, c):
        for u in range(DMA_UNROLL):
            r = rb * DMA_UNROLL + u
            src = stage.at[prev, pl.ds(pl.multiple_of(r * n_c, n_c), n_c)]
            for k in range(TOP_K):
                p = pos_ref[k * n_tok + (i - 1) * tm + r]
                pltpu.make_async_copy(src, xs_ref.at[pl.ds(pl.multiple_of(p * n_c, n_c), n_c)],
                                      sem).start(priority=k % 2)
        return c

    @pl.when(i >= 1)
    def _():
        lax.fori_loop(0, tm // DMA_UNROLL, issue_rows, 0)

    @pl.when(i < n_steps)
    def _():
        _store_row_tiled(stage.at[i % 2], 0, h_ref[...])

    @pl.when(i >= 1)
    def _():
        for k in range(TOP_K):
            pltpu.make_async_copy(stage.at[prev], xs_ref.at[pl.ds(0, tm * n_c)], sem).wait()


def _dispatch(h2, pos, zflag, n_rows):
    n, d = h2.shape
    n_c = d // LANES
    tm = _tile(n, DISPATCH_TILE)
    n_steps = n // tm
    n_tiles = n_rows // EXPERT_TILE
    return pl.pallas_call(
        functools.partial(_dispatch_kernel, tm=tm, n_c=n_c, n_tiles=n_tiles, n_tok=n, n_steps=n_steps),
        grid_spec=pltpu.PrefetchScalarGridSpec(
            num_scalar_prefetch=2,
            grid=(n_steps + 1,),
            in_specs=[pl.BlockSpec((tm, d), lambda i, p, z: (jnp.minimum(i, n_steps - 1), 0))],
            out_specs=pl.BlockSpec(memory_space=pl.ANY),
            scratch_shapes=[pltpu.VMEM((EXPERT_TILE * n_c, LANES), F32),
                            pltpu.VMEM((2, tm * n_c, LANES), F32),
                            pltpu.SemaphoreType.DMA(()),
                            pltpu.SemaphoreType.DMA(())]),
        out_shape=jax.ShapeDtypeStruct((n_rows * n_c, LANES), F32),
        compiler_params=_params("arbitrary"),
        name="dispatch",
    )(pos, zflag, h2)


def _cast_rows(src_ref, dst_ref, chunk=256):
    rows = dst_ref.shape[0]
    chunk = min(chunk, rows)

    def body(c, carry):
        r0 = pl.multiple_of(c * chunk, chunk)
        dst_ref[pl.ds(r0, chunk), :] = src_ref[pl.ds(r0, chunk), :].astype(BF16)
        return carry

    lax.fori_loop(0, rows // chunk, body, 0)


def _experts_kernel(te_ref, chg_ref, nxt_ref, nt_ref, xs_ref, wg_hbm, wu_hbm, wd_hbm, o_ref,
                    wf_g, wf_u, wf_d, wb_g, wb_u, wb_d, sem):
    t = pl.program_id(0)
    stages = ((wg_hbm, wf_g, wb_g), (wu_hbm, wf_u, wb_u), (wd_hbm, wf_d, wb_d))

    def fetch(e):
        return [pltpu.make_async_copy(src.at[e], dst, sem) for src, dst, _ in stages]

    @pl.when(t < nt_ref[0])
    def _():
        @pl.when(chg_ref[t] != 0)
        def _():
            @pl.when(t == 0)
            def _():
                for cp in fetch(te_ref[0]):
                    cp.start(priority=1)

            for cp in fetch(te_ref[t]):
                cp.wait()
            for _, wf, wb in stages:
                _cast_rows(wf, wb)

            @pl.when(nxt_ref[t] >= 0)
            def _():
                for cp in fetch(nxt_ref[t]):
                    cp.start(priority=1)

        x = _load_row_tiled(xs_ref, EXPERT_TILE, wb_g.shape[0] // LANES).astype(BF16)
        g = _dot(x, wb_g[...])
        u = _dot(x, wb_u[...])
        hid = (g * jax.nn.sigmoid(g) * u).astype(BF16)
        o_ref[...] = _dot(hid, wb_d[...])

    @pl.when(t >= nt_ref[0])
    def _():
        o_ref[...] = jnp.zeros_like(o_ref)


def _experts(xs, w_gate, w_up, w_down, plan):
    d, de = w_gate.shape[1], w_gate.shape[2]
    n_c = d // LANES
    n_rows = xs.shape[0] // n_c
    tile = lambda t, *_: (t, 0)
    hbm = pl.BlockSpec(memory_space=pl.ANY)
    return pl.pallas_call(
        _experts_kernel,
        grid_spec=pltpu.PrefetchScalarGridSpec(
            num_scalar_prefetch=len(plan),
            grid=(n_rows // EXPERT_TILE,),
            in_specs=[pl.BlockSpec((EXPERT_TILE * n_c, LANES), tile), hbm, hbm, hbm],
            out_specs=pl.BlockSpec((EXPERT_TILE, d), tile),
            scratch_shapes=[pltpu.VMEM((d, de), F32), pltpu.VMEM((d, de), F32), pltpu.VMEM((de, d), F32),
                            pltpu.VMEM((d, de), BF16), pltpu.VMEM((d, de), BF16), pltpu.VMEM((de, d), BF16),
                            pltpu.SemaphoreType.DMA(())]),
        out_shape=jax.ShapeDtypeStruct((n_rows, d), F32),
        compiler_params=_params("arbitrary"),
        name="experts",
    )(*plan, xs, w_gate, w_up, w_down)


def _combine_kernel(pos_ref, x1_ref, ri_ref, mod_ref, gp_ref, ys_ref, o_ref, buf, sem, *, tm, n_l, n_tok):
    step = pl.program_id(0) * n_l + pl.program_id(1)
    n_steps = pl.num_programs(0) * n_l
    slot = step % 2

    def issue(s, sl):
        def body(rb, c):
            for u in range(DMA_UNROLL):
                r = rb * DMA_UNROLL + u
                for k in range(TOP_K):
                    p = pos_ref[k * n_tok + s * tm + r]
                    pltpu.make_async_copy(ys_ref.at[pl.ds(p, 1)], buf.at[sl, k, rb, pl.ds(u, 1)],
                                          sem.at[sl]).start(priority=k % 2)
            return c

        lax.fori_loop(0, tm // DMA_UNROLL, body, 0)

    @pl.when(step == 0)
    def _():
        issue(0, 0)

    @pl.when(step + 1 < n_steps)
    def _():
        issue(step + 1, 1 - slot)

    for k in range(TOP_K):
        pltpu.make_async_copy(buf.at[slot, k], buf.at[slot, k], sem.at[slot]).wait()

    ri = ri_ref[0]
    d = o_ref.shape[-1]
    moe = (ri[:, R_W0:R_W0 + 1] * buf[slot, 0].reshape(tm, d)
           + ri[:, R_W1:R_W1 + 1] * buf[slot, 1].reshape(tm, d))
    o_ref[0] = x1_ref[0] + mod_ref[0, 5:6, :] * _rms(moe, gp_ref[...])


def _combine(x1, rinfo, mod, g_post2, ys, pos):
    bsz, l, d = x1.shape
    tm = _tile(l, COMBINE_TILE)
    n_l = l // tm
    tok = lambda b, i, p: (b, i, 0)
    return pl.pallas_call(
        functools.partial(_combine_kernel, tm=tm, n_l=n_l, n_tok=bsz * l),
        grid_spec=pltpu.PrefetchScalarGridSpec(
            num_scalar_prefetch=1,
            grid=(bsz, n_l),
            in_specs=[pl.BlockSpec((1, tm, d), tok),
                      pl.BlockSpec((1, tm, LANES), tok),
                      pl.BlockSpec((1, N_MOD, d), lambda b, i, p: (b, 0, 0)),
                      pl.BlockSpec((1, d), lambda b, i, p: (0, 0)),
                      pl.BlockSpec(memory_space=pl.ANY)],
            out_specs=pl.BlockSpec((1, tm, d), tok),
            scratch_shapes=[pltpu.VMEM((2, TOP_K, tm // DMA_UNROLL, DMA_UNROLL, d), F32),
                            pltpu.SemaphoreType.DMA((2,))]),
        out_shape=jax.ShapeDtypeStruct((bsz, l, d), F32),
        compiler_params=_params("arbitrary", "arbitrary"),
        name="combine",
    )(pos, x1, rinfo, mod, g_post2.reshape(1, d), ys)


def _rope_tables(seq, head_dim):
    axis_dim = head_dim // 2
    t = jnp.arange(seq, dtype=jnp.int32)
    pos = jnp.stack([t // GRID_W, t % GRID_W], axis=-1).astype(F32)
    inv_freq = ROPE_THETA ** (-jnp.arange(0, axis_dim, 2, dtype=F32) / axis_dim)
    ang = pos[:, :, None] * inv_freq
    cos, sin = jnp.cos(ang), jnp.sin(ang)
    cos_t = jnp.concatenate([cos, cos], axis=-1).reshape(seq, head_dim)
    sin_t = jnp.concatenate([-sin, sin], axis=-1).reshape(seq, head_dim)
    return cos_t, sin_t


def _route_plan(rt, cnt, n_groups, n_experts, n_tiles):
    e = rt[T_E0:T_E1 + 1]
    rank = jnp.stack([rt[T_R0_HI] * 256 + rt[T_R0_LO], rt[T_R1_HI] * 256 + rt[T_R1_LO]])
    counts = cnt[0, n_groups:n_groups + n_experts].astype(jnp.int32)
    tiles_e = (counts + EXPERT_TILE - 1) // EXPERT_TILE
    ids = jnp.arange(n_experts, dtype=jnp.int32)
    tile_end = jnp.sum(jnp.where(ids[None, :] <= ids[:, None], tiles_e[None, :], 0), axis=1)
    tile_start = tile_end - tiles_e
    nt = tile_end[-1]
    row0 = tile_start * EXPERT_TILE
    pos = jnp.sum(jnp.where(e[None] == ids[:, None, None], row0[:, None, None], 0), axis=0) + rank
    t = jnp.arange(n_tiles, dtype=jnp.int32)
    owner = lambda q: jnp.sum((tile_end[None, :] <= q[:, None]).astype(jnp.int32), axis=1)
    te = owner(jnp.minimum(t, nt - 1))
    chg = ((t == 0) | (te != owner(jnp.minimum(jnp.maximum(t - 1, 0), nt - 1)))).astype(jnp.int32)
    partial_last = jnp.any((tile_end[None, :] - 1 == t[:, None]) & (counts[None, :] % EXPERT_TILE != 0), axis=1)
    zflag = ((t >= nt) | partial_last).astype(jnp.int32)
    used = tiles_e > 0
    later = used[None, :] & (ids[None, :] > te[:, None])
    nxt = jnp.min(jnp.where(later, ids[None, :], n_experts), axis=1)
    nxt = jnp.where(nxt == n_experts, -1, nxt)
    return pos.reshape(-1), (te, chg, nxt, nt.reshape(1)), zflag


def kernel(x, c, ctx, c_ctx, w_ada, b_ada, g_pre1, g_post1, g_pre2, g_post2, w_in, q_norm, k_norm,
           gm_ln, w_s, b_s, w_ba, w_bg, w_o, w_rg, b_rg, w_re, b_re, w_gate, w_up, w_down):
    bsz, seq, d = x.shape
    depth = w_ada.shape[0]
    head_dim = q_norm.shape[-1]
    q_w, gm_w = w_ba.shape[1], w_bg.shape[1]
    kv_w = (w_in.shape[2] - q_w - 2 * gm_w - 2 * d) // 2
    n_groups, per_group = w_re.shape[2], w_re.shape[3]
    n_experts = n_groups * per_group
    assert head_dim == LANES and w_s.shape[2] == LANES and gm_w // w_s.shape[1] == LANES
    assert n_groups + n_experts <= LANES and seq % GRID_W == 0
    col_q = 2 * kv_w
    col_u, col_v = col_q + q_w, col_q + q_w + gm_w
    col_ga, col_gg = col_v + gm_w, col_v + gm_w + d
    n_tok = bsz * seq
    n_rows = n_tok * TOP_K + n_experts * EXPERT_TILE
    n_tiles = n_rows // EXPERT_TILE

    cos_t, sin_t = _rope_tables(seq, head_dim)
    pad = (-(bsz + 1)) % (2 * SUBLANES)
    cs = jnp.concatenate([c, c_ctx[None, :], jnp.zeros((pad, d), F32)], axis=0)

    for l in range(depth):
        assert l + 1 == depth, "context-stream update for non-final layers is not implemented"
        mod = _ada(cs, w_ada[l], b_ada[l]).reshape(cs.shape[0], N_MOD, d)
        w_in_b = w_in[l].astype(BF16)
        w_ba_b, w_bg_b, w_o_b = w_ba[l].astype(BF16), w_bg[l].astype(BF16), w_o[l].astype(BF16)
        bs_full = jnp.repeat(b_s[l].T, LANES, axis=1)
        w_r = jnp.concatenate([w_rg[l], w_re[l].reshape(d, n_experts),
                               jnp.zeros((d, LANES - n_groups - n_experts), F32)], axis=1)
        b_r = jnp.concatenate([b_rg[l], b_re[l].reshape(n_experts),
                               jnp.zeros((LANES - n_groups - n_experts,), F32)]).reshape(1, LANES)

        kc, vc = _ctx_kv(ctx, mod, bsz, g_pre1[l], w_in_b, k_norm[l], kv_w)
        hx, kx, vx, qx, gm = _inproj(x, mod, g_pre1[l], w_in_b, k_norm[l], q_norm[l], cos_t, sin_t, gm_ln[l],
                                     w_s[l], bs_full, kv_w, q_w, gm_w, head_dim ** -0.5 * LOG2_E)
        attn = _attention(qx, kc, vc, kx, vx, kv_w // head_dim)
        merged = _merge(hx, attn, gm, w_in_b, w_ba_b, w_bg_b, col_ga, col_gg)

        x1, h2, rinfo, rt, cnt = _out_route(merged, x, mod, g_post1[l], g_pre2[l], w_o_b, w_r, b_r,
                                            n_groups, per_group)
        pos, plan, zflag = _route_plan(rt, cnt, n_groups, n_experts, n_tiles)

        xs = _dispatch(h2.reshape(n_tok, d), pos, zflag, n_rows)
        ys = _experts(xs, w_gate[l], w_up[l], w_down[l], plan)
        x = _combine(x1, rinfo, mod, g_post2[l], ys, pos)
    return x
```

```python
import functools

import jax
import jax.numpy as jnp
from jax import lax
from jax.experimental import pallas as pl
from jax.experimental.pallas import tpu as pltpu

GRID_W = 64
ROPE_THETA = 10000.0
EPS = 1e-6
N_MOD = 6
TOP_K = 2
LOG2_E = 1.4426950408889634

LANES = 128
SUBLANES = 8
VMEM_LIMIT_BYTES = 56 * 1024 * 1024

EXPERT_TILE = 256
TOKEN_TILE = 512
ATTN_Q_TILE = 1024
DISPATCH_TILE = 512
COMBINE_TILE = 256
ADA_COL_TILE = 1536
COL_CHUNK = 512
DMA_UNROLL = SUBLANES

F32 = jnp.float32
BF16 = jnp.bfloat16


def _params(*sem):
    return pltpu.CompilerParams(dimension_semantics=sem, vmem_limit_bytes=VMEM_LIMIT_BYTES)


def _dot(a, b):
    return jnp.dot(a, b, preferred_element_type=F32)


def _split_bf16(a):
    hi = a.astype(BF16)
    lo = (a - hi.astype(F32)).astype(BF16)
    return hi, lo


def _dot3(a, w):
    a_hi, a_lo = _split_bf16(a)
    w_hi, w_lo = _split_bf16(w)
    return _dot(a_hi, w_hi) + _dot(a_lo, w_hi) + _dot(a_hi, w_lo)


def _rms(x, g):
    return x * lax.rsqrt(jnp.mean(x * x, axis=-1, keepdims=True) + EPS) * g


def _store_row_tiled(ref, t0, val):
    rows, d = val.shape
    n_c = d // LANES
    for c in range(n_c):
        ref[pl.ds(t0 * n_c + c, rows, stride=n_c), :] = val[:, c * LANES:(c + 1) * LANES]


def _load_row_tiled(ref, rows, n_c):
    return jnp.concatenate([ref[pl.ds(c, rows, stride=n_c), :] for c in range(n_c)], axis=1)


def _tile(n, pref):
    t = min(n, pref)
    while n % t:
        t //= 2
    return t


def _ada_kernel(c_ref, w_ref, b_ref, o_ref):
    c = c_ref[...]
    a = c * jax.nn.sigmoid(c)
    o_ref[...] = _dot3(a, w_ref[...]) + b_ref[...]


def _ada(cs, w, b):
    m, d = cs.shape
    n = w.shape[1]
    tn = _tile(n, ADA_COL_TILE)
    return pl.pallas_call(
        _ada_kernel,
        grid=(n // tn,),
        in_specs=[pl.BlockSpec((m, d), lambda j: (0, 0)),
                  pl.BlockSpec((d, tn), lambda j: (0, j)),
                  pl.BlockSpec((1, tn), lambda j: (0, j))],
        out_specs=pl.BlockSpec((m, tn), lambda j: (0, j)),
        out_shape=jax.ShapeDtypeStruct((m, n), F32),
        compiler_params=_params("arbitrary"),
        name="ada",
    )(cs, w, b.reshape(1, n))


def _swap32(x):
    lane = lax.broadcasted_iota(jnp.int32, x.shape, 1)
    fwd = pltpu.roll(x, LANES - 32, 1)
    bwd = pltpu.roll(x, 32, 1)
    return jnp.where((lane & 32) == 0, fwd, bwd)


def _norm_head(r, gain, cos, sin, scale):
    y = _rms(r, gain)
    if cos is not None:
        y = y * cos + _swap32(y) * sin
    if scale != 1.0:
        y = y * scale
    return y.astype(BF16)


def _ctx_kv_kernel(x_ref, mod_ref, g_ref, w_ref, kn_ref, k_ref, v_ref, *, n_kv):
    x = x_ref[0]
    inv = lax.rsqrt(jnp.mean(x * x, axis=-1, keepdims=True) + EPS)
    h = ((x * inv) * (g_ref[...] * (1.0 + mod_ref[0, 1:2, :])) + mod_ref[0, 0:1, :]).astype(BF16)
    r = _dot(h, w_ref[...])
    for hh in range(n_kv):
        sl = slice(hh * LANES, (hh + 1) * LANES)
        k_ref[0, :, sl] = _norm_head(r[:, sl], kn_ref[...], None, None, 1.0)
    v_ref[0] = r[:, n_kv * LANES:].astype(BF16)


def _ctx_kv(ctx, mod, mod_row, g_pre, w_in_b, k_norm, kv_w):
    bsz, l, d = ctx.shape
    tm = _tile(l, TOKEN_TILE)
    out = jax.ShapeDtypeStruct((bsz, l, kv_w), BF16)
    tok = lambda b, i: (b, i, 0)
    return pl.pallas_call(
        functools.partial(_ctx_kv_kernel, n_kv=kv_w // LANES),
        grid=(bsz, l // tm),
        in_specs=[pl.BlockSpec((1, tm, d), tok),
                  pl.BlockSpec((1, N_MOD, d), lambda b, i: (mod_row, 0, 0)),
                  pl.BlockSpec((1, d), lambda b, i: (0, 0)),
                  pl.BlockSpec((d, 2 * kv_w), lambda b, i: (0, 0)),
                  pl.BlockSpec((1, LANES), lambda b, i: (0, 0))],
        out_specs=[pl.BlockSpec((1, tm, kv_w), tok)] * 2,
        out_shape=[out, out],
        compiler_params=_params("arbitrary", "arbitrary"),
        name="ctx_kv",
    )(ctx, mod, g_pre.reshape(1, d), w_in_b, k_norm.reshape(1, LANES))


def _gelu(x):
    c = 0.7978845608028654
    return x * (0.5 + 0.5 * jnp.tanh(x * (c + (c * 0.044715) * (x * x))))


def _inproj_kernel(x_ref, mod_ref, g_ref, w_ref, kn_ref, qn_ref, cos_ref, sin_ref, ln_ref, ws_ref, bs_ref,
                   h_ref, k_ref, v_ref, q_ref, gm_ref, *, kv_w, q_w, gm_w, scale):
    x = x_ref[0]
    inv = lax.rsqrt(jnp.mean(x * x, axis=-1, keepdims=True) + EPS)
    h = ((x * inv) * (g_ref[...] * (1.0 + mod_ref[0, 1:2, :])) + mod_ref[0, 0:1, :]).astype(BF16)
    h_ref[0] = h
    cos, sin = cos_ref[...], sin_ref[...]
    tm = h.shape[0]

    rk = _dot(h, w_ref[:, 0:kv_w])
    for hh in range(kv_w // LANES):
        sl = slice(hh * LANES, (hh + 1) * LANES)
        k_ref[0, :, sl] = _norm_head(rk[:, sl], kn_ref[...], cos, sin, 1.0)
    v_ref[0] = _dot(h, w_ref[:, kv_w:2 * kv_w]).astype(BF16)

    col_q = 2 * kv_w
    cq = min(COL_CHUNK, q_w)
    for j in range(q_w // cq):
        r = _dot(h, w_ref[:, col_q + j * cq:col_q + (j + 1) * cq])
        for hh in range(cq // LANES):
            sl = slice(hh * LANES, (hh + 1) * LANES)
            q_ref[0, :, j * cq + hh * LANES:j * cq + (hh + 1) * LANES] = _norm_head(
                r[:, sl], qn_ref[...], cos, sin, scale)

    col_u, col_v = col_q + q_w, col_q + q_w + gm_w
    cg = min(COL_CHUNK, gm_w)
    for j in range(gm_w // cg):
        gu = _gelu(_dot(h, w_ref[:, col_u + j * cg:col_u + (j + 1) * cg]))
        gv = _gelu(_dot(h, w_ref[:, col_v + j * cg:col_v + (j + 1) * cg]))
        for g in range(cg // LANES):
            cs = slice(g * LANES, (g + 1) * LANES)
            oc = slice(j * cg + g * LANES, j * cg + (g + 1) * LANES)
            v = gv[:, cs]
            vc = v - jnp.mean(v, axis=-1, keepdims=True)
            vn = vc * lax.rsqrt(jnp.mean(vc * vc, axis=-1, keepdims=True) + EPS) * ln_ref[:, oc]
            vn = vn.astype(BF16)
            w = ws_ref[j * (cg // LANES) + g].astype(BF16)
            for c in range(tm // LANES):
                rs = slice(c * LANES, (c + 1) * LANES)
                s = _dot(w, vn[rs, :]) + bs_ref[:, oc]
                gm_ref[0, rs, oc] = (gu[rs, cs] * s).astype(BF16)


def _inproj(x, mod, g_pre, w_in_b, k_norm, q_norm, cos, sin, gm_ln, w_s, bs_full, kv_w, q_w, gm_w, scale):
    bsz, l, d = x.shape
    tm = _tile(l, TOKEN_TILE)
    n_cols = 2 * kv_w + q_w + 2 * gm_w
    assert tm % LANES == 0
    tok = lambda b, i: (b, i, 0)
    c2 = lambda b, i: (0, 0)
    return pl.pallas_call(
        functools.partial(_inproj_kernel, kv_w=kv_w, q_w=q_w, gm_w=gm_w, scale=scale),
        grid=(bsz, l // tm),
        in_specs=[pl.BlockSpec((1, tm, d), tok),
                  pl.BlockSpec((1, N_MOD, d), lambda b, i: (b, 0, 0)),
                  pl.BlockSpec((1, d), c2),
                  pl.BlockSpec((d, n_cols), c2, pipeline_mode=pl.Buffered(1)),
                  pl.BlockSpec((1, LANES), c2),
                  pl.BlockSpec((1, LANES), c2),
                  pl.BlockSpec((tm, LANES), lambda b, i: (i, 0)),
                  pl.BlockSpec((tm, LANES), lambda b, i: (i, 0)),
                  pl.BlockSpec((1, gm_w), c2),
                  pl.BlockSpec(w_s.shape, lambda b, i: (0, 0, 0)),
                  pl.BlockSpec(bs_full.shape, c2)],
        out_specs=[pl.BlockSpec((1, tm, d), tok),
                   pl.BlockSpec((1, tm, kv_w), tok),
                   pl.BlockSpec((1, tm, kv_w), tok),
                   pl.BlockSpec((1, tm, q_w), tok),
                   pl.BlockSpec((1, tm, gm_w), tok)],
        out_shape=[jax.ShapeDtypeStruct((bsz, l, d), BF16),
                   jax.ShapeDtypeStruct((bsz, l, kv_w), BF16),
                   jax.ShapeDtypeStruct((bsz, l, kv_w), BF16),
                   jax.ShapeDtypeStruct((bsz, l, q_w), BF16),
                   jax.ShapeDtypeStruct((bsz, l, gm_w), BF16)],
        compiler_params=_params("arbitrary", "arbitrary"),
        name="inproj",
    )(x, mod, g_pre.reshape(1, d), w_in_b, k_norm.reshape(1, LANES), q_norm.reshape(1, LANES), cos, sin,
      gm_ln.reshape(1, gm_w), w_s, bs_full)


def _attn_kernel(q_ref, kc_ref, vc_ref, kx_ref, vx_ref, o_ref, *, n_kv, grp):
    nt = (((1,), (1,)), ((), ()))
    for kv in range(n_kv):
        ks = slice(kv * LANES, (kv + 1) * LANES)
        kc, kx = kc_ref[0, :, ks], kx_ref[0, :, ks]
        vc = jnp.concatenate([vc_ref[0, :, ks], jnp.ones_like(kc)], axis=1)
        vx = jnp.concatenate([vx_ref[0, :, ks], jnp.ones_like(kx)], axis=1)
        for hh in range(grp):
            sl = slice((kv * grp + hh) * LANES, (kv * grp + hh + 1) * LANES)
            q = q_ref[0, :, sl]
            sc = lax.dot_general(q, kc, nt, preferred_element_type=F32)
            sx = lax.dot_general(q, kx, nt, preferred_element_type=F32)
            m = jnp.maximum(jnp.max(sc, axis=-1, keepdims=True), jnp.max(sx, axis=-1, keepdims=True))
            pc = jnp.exp2(sc - m).astype(BF16)
            px = jnp.exp2(sx - m).astype(BF16)
            o = _dot(pc, vc) + _dot(px, vx)
            o_ref[0, :, sl] = (o[:, :LANES] / o[:, LANES:LANES + 1]).astype(BF16)


def _attention(q, kc, vc, kx, vx, n_kv):
    bsz, l, q_w = q.shape
    lc, kv_w = kc.shape[1], kc.shape[2]
    tq = _tile(l, ATTN_Q_TILE)
    tok = lambda b, i: (b, i, 0)
    whole = lambda b, i: (b, 0, 0)
    return pl.pallas_call(
        functools.partial(_attn_kernel, n_kv=n_kv, grp=q_w // kv_w),
        grid=(bsz, l // tq),
        in_specs=[pl.BlockSpec((1, tq, q_w), tok),
                  pl.BlockSpec((1, lc, kv_w), whole),
                  pl.BlockSpec((1, lc, kv_w), whole),
                  pl.BlockSpec((1, l, kv_w), whole),
                  pl.BlockSpec((1, l, kv_w), whole)],
        out_specs=pl.BlockSpec((1, tq, q_w), tok),
        out_shape=jax.ShapeDtypeStruct((bsz, l, q_w), BF16),
        compiler_params=_params("arbitrary", "arbitrary"),
        name="attention",
    )(q, kc, vc, kx, vx)


def _merge_kernel(h_ref, a_ref, g_ref, *refs, n_chunk):
    wga, wgg = refs[:n_chunk], refs[n_chunk:2 * n_chunk]
    wba_ref, wbg_ref, o_ref = refs[2 * n_chunk:]
    h, a, g = h_ref[0], a_ref[0], g_ref[0]
    tn = wga[0].shape[1]
    for j in range(n_chunk):
        cs = slice(j * tn, (j + 1) * tn)
        ga = jax.nn.sigmoid(_dot(h, wga[j][...]))
        gg = jax.nn.sigmoid(_dot(h, wgg[j][...]))
        o_ref[0, :, cs] = (ga * _dot(a, wba_ref[:, cs]) + gg * _dot(g, wbg_ref[:, cs])).astype(BF16)


def _merge(h, attn, gm, w_in_b, w_ba_b, w_bg_b, col_ga, col_gg):
    bsz, l, d = h.shape
    q_w, gm_w = attn.shape[2], gm.shape[2]
    tm = _tile(l, TOKEN_TILE)
    tn = _tile(d, COL_CHUNK)
    assert col_ga % tn == 0 and col_gg % tn == 0
    n_chunk = d // tn
    tok = lambda b, i: (b, i, 0)
    gate_specs = [pl.BlockSpec((d, tn), functools.partial(lambda b, i, c: (0, c), c=(c0 // tn) + j),
                               pipeline_mode=pl.Buffered(1))
                  for c0 in (col_ga, col_gg) for j in range(n_chunk)]
    return pl.pallas_call(
        functools.partial(_merge_kernel, n_chunk=n_chunk),
        grid=(bsz, l // tm),
        in_specs=[pl.BlockSpec((1, tm, d), tok),
                  pl.BlockSpec((1, tm, q_w), tok),
                  pl.BlockSpec((1, tm, gm_w), tok),
                  *gate_specs,
                  pl.BlockSpec((q_w, d), lambda b, i: (0, 0)),
                  pl.BlockSpec((gm_w, d), lambda b, i: (0, 0))],
        out_specs=pl.BlockSpec((1, tm, d), tok),
        out_shape=jax.ShapeDtypeStruct((bsz, l, d), BF16),
        compiler_params=_params("arbitrary", "arbitrary"),
        name="merge",
    )(h, attn, gm, *([w_in_b] * (2 * n_chunk)), w_ba_b, w_bg_b)


R_W0, R_W1 = range(2)
T_E0, T_E1, T_R0_HI, T_R0_LO, T_R1_HI, T_R1_LO = range(6)


def _out_kernel(m_ref, x_ref, mod_ref, gp1_ref, gp2_ref, wo_ref, wr_ref, br_ref,
                x1_ref, h2_ref, ri_ref, rt_ref, cnt_ref, carry_ref, *, n_groups, per_group):
    @pl.when((pl.program_id(0) == 0) & (pl.program_id(1) == 0))
    def _():
        carry_ref[...] = jnp.zeros_like(carry_ref)

    tm = m_ref.shape[1]
    sub = tm
    wr_hi, wr_lo = _split_bf16(wr_ref[...])
    wr_both = jnp.concatenate([wr_hi, wr_lo], axis=1)
    lane = lax.broadcasted_iota(jnp.int32, (sub, LANES), 1)
    row = lax.broadcasted_iota(jnp.int32, (sub, sub), 0)
    col = lax.broadcasted_iota(jnp.int32, (sub, sub), 1)
    tri = jnp.where(col < row, 1.0, 0.0).astype(BF16)
    sel = jnp.where(lax.broadcasted_iota(jnp.int32, (SUBLANES, LANES), 0)
                    == lax.broadcasted_iota(jnp.int32, (SUBLANES, LANES), 1), 1.0, 0.0).astype(BF16)
    neg = jnp.float32(-jnp.inf)
    carry = carry_ref[...]

    for sb in range(tm // sub):
        rs = slice(sb * sub, (sb + 1) * sub)
        mix = _dot(m_ref[0, rs, :], wo_ref[...])
        inv1 = lax.rsqrt(jnp.mean(mix * mix, axis=-1, keepdims=True) + EPS)
        x1 = x_ref[0, rs, :] + (mix * inv1) * (mod_ref[0, 2:3, :] * gp1_ref[...])
        x1_ref[0, rs, :] = x1
        inv2 = lax.rsqrt(jnp.mean(x1 * x1, axis=-1, keepdims=True) + EPS)
        h2 = (x1 * inv2) * (gp2_ref[...] * (1.0 + mod_ref[0, 4:5, :])) + mod_ref[0, 3:4, :]
        h2_ref[0, rs, :] = h2

        h_hi, h_lo = _split_bf16(h2)
        both = _dot(h_hi, wr_both)
        logits = both[:, :LANES] + both[:, LANES:] + _dot(h_lo, wr_hi) + br_ref[...]
        lg = jnp.where(lane < n_groups, logits, neg)
        gmax = jnp.max(lg, axis=-1, keepdims=True)
        p_top = 1.0 / jnp.sum(jnp.exp(lg - gmax), axis=-1, keepdims=True)
        gidx = jnp.min(jnp.where(lg == gmax, lane, LANES), axis=-1, keepdims=True)
        lo = n_groups + gidx * per_group
        le = jnp.where((lane >= lo) & (lane < lo + per_group), logits, neg)
        l1 = jnp.max(le, axis=-1, keepdims=True)
        i1 = jnp.min(jnp.where(le == l1, lane, LANES), axis=-1, keepdims=True)
        le2 = jnp.where(lane == i1, neg, le)
        l2 = jnp.max(le2, axis=-1, keepdims=True)
        i2 = jnp.min(jnp.where(le2 == l2, lane, LANES), axis=-1, keepdims=True)
        r = jnp.exp(l2 - l1)
        w0 = p_top / (1.0 + r)
        w1 = p_top * r / (1.0 + r)

        oh1 = lane == i1
        oh2 = lane == i2
        oh = jnp.where(oh1 | oh2, 1.0, 0.0)
        base = _dot(tri, oh.astype(BF16)) + carry
        rank0 = jnp.sum(jnp.where(oh1, base, 0.0), axis=-1, keepdims=True)
        rank1 = jnp.sum(jnp.where(oh2, base, 0.0), axis=-1, keepdims=True)
        carry = carry + jnp.sum(oh, axis=0, keepdims=True)

        e0 = (i1 - n_groups).astype(F32)
        e1 = (i2 - n_groups).astype(F32)
        rec = jnp.zeros_like(logits)
        for idx, val in ((R_W0, w0), (R_W1, w1)):
            rec = jnp.where(lane == idx, val, rec)
        ri_ref[0, rs, :] = rec

        r0_hi = jnp.floor(rank0 * (1.0 / 256.0))
        r1_hi = jnp.floor(rank1 * (1.0 / 256.0))
        ints = jnp.zeros_like(logits)
        for idx, val in ((T_E0, e0), (T_E1, e1), (T_R0_HI, r0_hi), (T_R0_LO, rank0 - 256.0 * r0_hi),
                         (T_R1_HI, r1_hi), (T_R1_LO, rank1 - 256.0 * r1_hi)):
            ints = jnp.where(lane == idx, val, ints)
        rt = lax.dot_general(sel, ints.astype(BF16), (((1,), (1,)), ((), ())), preferred_element_type=F32)
        rt_ref[:, rs] = rt.astype(jnp.int32)

    carry_ref[...] = carry
    cnt_ref[...] = carry


def _out_route(merged, x, mod, g_post1, g_pre2, w_o_b, w_r, b_r, n_groups, per_group):
    bsz, l, d = x.shape
    tm = _tile(l, TOKEN_TILE)
    tok = lambda b, i: (b, i, 0)
    const2 = lambda b, i: (0, 0)
    return pl.pallas_call(
        functools.partial(_out_kernel, n_groups=n_groups, per_group=per_group),
        grid=(bsz, l // tm),
        in_specs=[pl.BlockSpec((1, tm, d), tok),
                  pl.BlockSpec((1, tm, d), tok),
                  pl.BlockSpec((1, N_MOD, d), lambda b, i: (b, 0, 0)),
                  pl.BlockSpec((1, d), const2),
                  pl.BlockSpec((1, d), const2),
                  pl.BlockSpec((d, d), const2),
                  pl.BlockSpec((d, LANES), const2),
                  pl.BlockSpec((1, LANES), const2)],
        out_specs=[pl.BlockSpec((1, tm, d), tok),
                   pl.BlockSpec((1, tm, d), tok),
                   pl.BlockSpec((1, tm, LANES), tok),
                   pl.BlockSpec((SUBLANES, tm), lambda b, i: (0, b * (l // tm) + i)),
                   pl.BlockSpec((1, LANES), const2)],
        out_shape=[jax.ShapeDtypeStruct((bsz, l, d), F32),
                   jax.ShapeDtypeStruct((bsz, l, d), F32),
                   jax.ShapeDtypeStruct((bsz, l, LANES), F32),
                   jax.ShapeDtypeStruct((SUBLANES, bsz * l), jnp.int32),
                   jax.ShapeDtypeStruct((1, LANES), F32)],
        scratch_shapes=[pltpu.VMEM((1, LANES), F32)],
        compiler_params=_params("arbitrary", "arbitrary"),
        name="out_route",
    )(merged, x, mod, g_post1.reshape(1, d), g_pre2.reshape(1, d), w_o_b, w_r, b_r)


def _dispatch_kernel(pos_ref, zf_ref, h_ref, xs_ref, zbuf, stage, sem, zsem, *, tm, n_c, n_tiles, n_tok, n_steps):
    i = pl.program_id(0)
    tile_rows = EXPERT_TILE * n_c
    prev = (i + 1) % 2

    def zero_copy(t):
        return pltpu.make_async_copy(zbuf, xs_ref.at[pl.ds(pl.multiple_of(t * tile_rows, tile_rows), tile_rows)], zsem)

    @pl.when(i == 0)
    def _():
        zbuf[...] = jnp.zeros_like(zbuf)

        def issue(t, c):
            @pl.when(zf_ref[t] != 0)
            def _():
                zero_copy(t).start()
            return c

        def drain(t, c):
            @pl.when(zf_ref[t] != 0)
            def _():
                zero_copy(t).wait()
            return c

        lax.fori_loop(0, n_tiles, issue, 0)
        lax.fori_loop(0, n_tiles, drain, 0)

    def issue_rows(rb, c):
        for u in range(DMA_UNROLL):
            r = rb * DMA_UNROLL + u
            src = stage.at[prev, pl.ds(pl.multiple_of(r * n_c, n_c), n_c)]
            for k in range(TOP_K):
                p = pos_ref[k * n_tok + (i - 1) * tm + r]
                pltpu.make_async_copy(src, xs_ref.at[pl.ds(pl.multiple_of(p * n_c, n_c), n_c)],
                                      sem).start(priority=k % 2)
        return c

    @pl.when(i >= 1)
    def _():
        lax.fori_loop(0, tm // DMA_UNROLL, issue_rows, 0)

    @pl.when(i < n_steps)
    def _():
        _store_row_tiled(stage.at[i % 2], 0, h_ref[...])

    @pl.when(i >= 1)
    def _():
        for k in range(TOP_K):
            pltpu.make_async_copy(stage.at[prev], xs_ref.at[pl.ds(0, tm * n_c)], sem).wait()


def _dispatch(h2, pos, zflag, n_rows):
    n, d = h2.shape
    n_c = d // LANES
    tm = _tile(n, DISPATCH_TILE)
    n_steps = n // tm
    n_tiles = n_rows // EXPERT_TILE
    return pl.pallas_call(
        functools.partial(_dispatch_kernel, tm=tm, n_c=n_c, n_tiles=n_tiles, n_tok=n, n_steps=n_steps),
        grid_spec=pltpu.PrefetchScalarGridSpec(
            num_scalar_prefetch=2,
            grid=(n_steps + 1,),
            in_specs=[pl.BlockSpec((tm, d), lambda i, p, z: (jnp.minimum(i, n_steps - 1), 0))],
            out_specs=pl.BlockSpec(memory_space=pl.ANY),
            scratch_shapes=[pltpu.VMEM((EXPERT_TILE * n_c, LANES), F32),
                            pltpu.VMEM((2, tm * n_c, LANES), F32),
                            pltpu.SemaphoreType.DMA(()),
                            pltpu.SemaphoreType.DMA(())]),
        out_shape=jax.ShapeDtypeStruct((n_rows * n_c, LANES), F32),
        compiler_params=_params("arbitrary"),
        name="dispatch",
    )(pos, zflag, h2)


def _cast_rows(src_ref, dst_ref, chunk=256):
    rows = dst_ref.shape[0]
    chunk = min(chunk, rows)

    def body(c, carry):
        r0 = pl.multiple_of(c * chunk, chunk)
        dst_ref[pl.ds(r0, chunk), :] = src_ref[pl.ds(r0, chunk), :].astype(BF16)
        return carry

    lax.fori_loop(0, rows // chunk, body, 0)


def _experts_kernel(te_ref, chg_ref, nxt_ref, nt_ref, xs_ref, wg_hbm, wu_hbm, wd_hbm, o_ref,
                    wf_g, wf_u, wf_d, wb_g, wb_u, wb_d, sem):
    t = pl.program_id(0)
    stages = ((wg_hbm, wf_g, wb_g), (wu_hbm, wf_u, wb_u), (wd_hbm, wf_d, wb_d))

    def fetch(e):
        return [pltpu.make_async_copy(src.at[e], dst, sem) for src, dst, _ in stages]

    @pl.when(t < nt_ref[0])
    def _():
        @pl.when(chg_ref[t] != 0)
        def _():
            @pl.when(t == 0)
            def _():
                for cp in fetch(te_ref[0]):
                    cp.start(priority=1)

            for cp in fetch(te_ref[t]):
                cp.wait()
            for _, wf, wb in stages:
                _cast_rows(wf, wb)

            @pl.when(nxt_ref[t] >= 0)
            def _():
                for cp in fetch(nxt_ref[t]):
                    cp.start(priority=1)

        x = _load_row_tiled(xs_ref, EXPERT_TILE, wb_g.shape[0] // LANES).astype(BF16)
        g = _dot(x, wb_g[...])
        u = _dot(x, wb_u[...])
        hid = (g * jax.nn.sigmoid(g) * u).astype(BF16)
        o_ref[...] = _dot(hid, wb_d[...])

    @pl.when(t >= nt_ref[0])
    def _():
        o_ref[...] = jnp.zeros_like(o_ref)


def _experts(xs, w_gate, w_up, w_down, plan):
    d, de = w_gate.shape[1], w_gate.shape[2]
    n_c = d // LANES
    n_rows = xs.shape[0] // n_c
    tile = lambda t, *_: (t, 0)
    hbm = pl.BlockSpec(memory_space=pl.ANY)
    return pl.pallas_call(
        _experts_kernel,
        grid_spec=pltpu.PrefetchScalarGridSpec(
            num_scalar_prefetch=len(plan),
            grid=(n_rows // EXPERT_TILE,),
            in_specs=[pl.BlockSpec((EXPERT_TILE * n_c, LANES), tile), hbm, hbm, hbm],
            out_specs=pl.BlockSpec((EXPERT_TILE, d), tile),
            scratch_shapes=[pltpu.VMEM((d, de), F32), pltpu.VMEM((d, de), F32), pltpu.VMEM((de, d), F32),
                            pltpu.VMEM((d, de), BF16), pltpu.VMEM((d, de), BF16), pltpu.VMEM((de, d), BF16),
                            pltpu.SemaphoreType.DMA(())]),
        out_shape=jax.ShapeDtypeStruct((n_rows, d), F32),
        compiler_params=_params("arbitrary"),
        name="experts",
    )(*plan, xs, w_gate, w_up, w_down)


def _combine_kernel(pos_ref, x1_ref, ri_ref, mod_ref, gp_ref, ys_ref, o_ref, buf, sem, *, tm, n_l, n_tok):
    step = pl.program_id(0) * n_l + pl.program_id(1)
    n_steps = pl.num_programs(0) * n_l
    slot = step % 2

    def issue(s, sl):
        def body(pair, c):
            for half in range(2):
                rb = pair * 2 + half
                for u in range(DMA_UNROLL):
                    r = rb * DMA_UNROLL + u
                    for k in range(TOP_K):
                        p = pos_ref[k * n_tok + s * tm + r]
                        pltpu.make_async_copy(ys_ref.at[pl.ds(p, 1)], buf.at[sl, k, rb, pl.ds(u, 1)],
                                              sem.at[sl]).start(priority=k % 2)
            return c

        lax.fori_loop(0, tm // (2 * DMA_UNROLL), body, 0)

    @pl.when(step == 0)
    def _():
        issue(0, 0)

    @pl.when(step + 1 < n_steps)
    def _():
        issue(step + 1, 1 - slot)

    for k in range(TOP_K):
        pltpu.make_async_copy(buf.at[slot, k], buf.at[slot, k], sem.at[slot]).wait()

    ri = ri_ref[0]
    d = o_ref.shape[-1]
    moe = (ri[:, R_W0:R_W0 + 1] * buf[slot, 0].reshape(tm, d)
           + ri[:, R_W1:R_W1 + 1] * buf[slot, 1].reshape(tm, d))
    o_ref[0] = x1_ref[0] + mod_ref[0, 5:6, :] * _rms(moe, gp_ref[...])


def _combine(x1, rinfo, mod, g_post2, ys, pos):
    bsz, l, d = x1.shape
    tm = _tile(l, COMBINE_TILE)
    n_l = l // tm
    tok = lambda b, i, p: (b, i, 0)
    return pl.pallas_call(
        functools.partial(_combine_kernel, tm=tm, n_l=n_l, n_tok=bsz * l),
        grid_spec=pltpu.PrefetchScalarGridSpec(
            num_scalar_prefetch=1,
            grid=(bsz, n_l),
            in_specs=[pl.BlockSpec((1, tm, d), tok),
                      pl.BlockSpec((1, tm, LANES), tok),
                      pl.BlockSpec((1, N_MOD, d), lambda b, i, p: (b, 0, 0)),
                      pl.BlockSpec((1, d), lambda b, i, p: (0, 0)),
                      pl.BlockSpec(memory_space=pl.ANY)],
            out_specs=pl.BlockSpec((1, tm, d), tok),
            scratch_shapes=[pltpu.VMEM((2, TOP_K, tm // DMA_UNROLL, DMA_UNROLL, d), F32),
                            pltpu.SemaphoreType.DMA((2,))]),
        out_shape=jax.ShapeDtypeStruct((bsz, l, d), F32),
        compiler_params=_params("arbitrary", "arbitrary"),
        name="combine",
    )(pos, x1, rinfo, mod, g_post2.reshape(1, d), ys)


def _rope_tables(seq, head_dim):
    axis_dim = head_dim // 2
    t = jnp.arange(seq, dtype=jnp.int32)
    pos = jnp.stack([t // GRID_W, t % GRID_W], axis=-1).astype(F32)
    inv_freq = ROPE_THETA ** (-jnp.arange(0, axis_dim, 2, dtype=F32) / axis_dim)
    ang = pos[:, :, None] * inv_freq
    cos, sin = jnp.cos(ang), jnp.sin(ang)
    cos_t = jnp.concatenate([cos, cos], axis=-1).reshape(seq, head_dim)
    sin_t = jnp.concatenate([-sin, sin], axis=-1).reshape(seq, head_dim)
    return cos_t, sin_t


def _route_plan(rt, cnt, n_groups, n_experts, n_tiles):
    e = rt[T_E0:T_E1 + 1]
    rank = jnp.stack([rt[T_R0_HI] * 256 + rt[T_R0_LO], rt[T_R1_HI] * 256 + rt[T_R1_LO]])
    counts = cnt[0, n_groups:n_groups + n_experts].astype(jnp.int32)
    tiles_e = (counts + EXPERT_TILE - 1) // EXPERT_TILE
    ids = jnp.arange(n_experts, dtype=jnp.int32)
    tile_end = jnp.sum(jnp.where(ids[None, :] <= ids[:, None], tiles_e[None, :], 0), axis=1)
    tile_start = tile_end - tiles_e
    nt = tile_end[-1]
    row0 = tile_start * EXPERT_TILE
    pos = jnp.sum(jnp.where(e[None] == ids[:, None, None], row0[:, None, None], 0), axis=0) + rank
    t = jnp.arange(n_tiles, dtype=jnp.int32)
    owner = lambda q: jnp.sum((tile_end[None, :] <= q[:, None]).astype(jnp.int32), axis=1)
    te = owner(jnp.minimum(t, nt - 1))
    chg = ((t == 0) | (te != owner(jnp.minimum(jnp.maximum(t - 1, 0), nt - 1)))).astype(jnp.int32)
    partial_last = jnp.any((tile_end[None, :] - 1 == t[:, None]) & (counts[None, :] % EXPERT_TILE != 0), axis=1)
    zflag = ((t >= nt) | partial_last).astype(jnp.int32)
    used = tiles_e > 0
    later = used[None, :] & (ids[None, :] > te[:, None])
    nxt = jnp.min(jnp.where(later, ids[None, :], n_experts), axis=1)
    nxt = jnp.where(nxt == n_experts, -1, nxt)
    return pos.reshape(-1), (te, chg, nxt, nt.reshape(1)), zflag


def kernel(x, c, ctx, c_ctx, w_ada, b_ada, g_pre1, g_post1, g_pre2, g_post2, w_in, q_norm, k_norm,
           gm_ln, w_s, b_s, w_ba, w_bg, w_o, w_rg, b_rg, w_re, b_re, w_gate, w_up, w_down):
    bsz, seq, d = x.shape
    depth = w_ada.shape[0]
    head_dim = q_norm.shape[-1]
    q_w, gm_w = w_ba.shape[1], w_bg.shape[1]
    kv_w = (w_in.shape[2] - q_w - 2 * gm_w - 2 * d) // 2
    n_groups, per_group = w_re.shape[2], w_re.shape[3]
    n_experts = n_groups * per_group
    assert head_dim == LANES and w_s.shape[2] == LANES and gm_w // w_s.shape[1] == LANES
    assert n_groups + n_experts <= LANES and seq % GRID_W == 0
    col_q = 2 * kv_w
    col_u, col_v = col_q + q_w, col_q + q_w + gm_w
    col_ga, col_gg = col_v + gm_w, col_v + gm_w + d
    n_tok = bsz * seq
    n_rows = n_tok * TOP_K + n_experts * EXPERT_TILE
    n_tiles = n_rows // EXPERT_TILE

    cos_t, sin_t = _rope_tables(seq, head_dim)
    pad = (-(bsz + 1)) % (2 * SUBLANES)
    cs = jnp.concatenate([c, c_ctx[None, :], jnp.zeros((pad, d), F32)], axis=0)

    for l in range(depth):
        assert l + 1 == depth, "context-stream update for non-final layers is not implemented"
        mod = _ada(cs, w_ada[l], b_ada[l]).reshape(cs.shape[0], N_MOD, d)
        w_in_b = w_in[l].astype(BF16)
        w_ba_b, w_bg_b, w_o_b = w_ba[l].astype(BF16), w_bg[l].astype(BF16), w_o[l].astype(BF16)
        bs_full = jnp.repeat(b_s[l].T, LANES, axis=1)
        w_r = jnp.concatenate([w_rg[l], w_re[l].reshape(d, n_experts),
                               jnp.zeros((d, LANES - n_groups - n_experts), F32)], axis=1)
        b_r = jnp.concatenate([b_rg[l], b_re[l].reshape(n_experts),
                               jnp.zeros((LANES - n_groups - n_experts,), F32)]).reshape(1, LANES)

        kc, vc = _ctx_kv(ctx, mod, bsz, g_pre1[l], w_in_b, k_norm[l], kv_w)
        hx, kx, vx, qx, gm = _inproj(x, mod, g_pre1[l], w_in_b, k_norm[l], q_norm[l], cos_t, sin_t, gm_ln[l],
                                     w_s[l], bs_full, kv_w, q_w, gm_w, head_dim ** -0.5 * LOG2_E)
        attn = _attention(qx, kc, vc, kx, vx, kv_w // head_dim)
        merged = _merge(hx, attn, gm, w_in_b, w_ba_b, w_bg_b, col_ga, col_gg)

        x1, h2, rinfo, rt, cnt = _out_route(merged, x, mod, g_post1[l], g_pre2[l], w_o_b, w_r, b_r,
                                            n_groups, per_group)
        pos, plan, zflag = _route_plan(rt, cnt, n_groups, n_experts, n_tiles)

        xs = _dispatch(h2.reshape(n_tok, d), pos, zflag, n_rows)
        ys = _experts(xs, w_gate[l], w_up[l], w_down[l], plan)
        x = _combine(x1, rinfo, mod, g_post2[l], ys, pos)
    return x
```

```python
import functools

import jax
import jax.numpy as jnp
from jax import lax
from jax.experimental import pallas as pl
from jax.experimental.pallas import tpu as pltpu

GRID_W = 64
ROPE_THETA = 10000.0
EPS = 1e-6
N_MOD = 6
TOP_K = 2
LOG2_E = 1.4426950408889634

LANES = 128
SUBLANES = 8
VMEM_LIMIT_BYTES = 56 * 1024 * 1024

EXPERT_TILE = 256
TOKEN_TILE = 512
ATTN_Q_TILE = 1024
DISPATCH_TILE = 512
COMBINE_TILE = 256
ADA_COL_TILE = 1536
COL_CHUNK = 512
DMA_UNROLL = SUBLANES

F32 = jnp.float32
BF16 = jnp.bfloat16


def _params(*sem):
    return pltpu.CompilerParams(dimension_semantics=sem, vmem_limit_bytes=VMEM_LIMIT_BYTES)


def _dot(a, b):
    return jnp.dot(a, b, preferred_element_type=F32)


def _split_bf16(a):
    hi = a.astype(BF16)
    lo = (a - hi.astype(F32)).astype(BF16)
    return hi, lo


def _dot3(a, w):
    a_hi, a_lo = _split_bf16(a)
    w_hi, w_lo = _split_bf16(w)
    return _dot(a_hi, w_hi) + _dot(a_lo, w_hi) + _dot(a_hi, w_lo)


def _rms(x, g):
    return x * lax.rsqrt(jnp.mean(x * x, axis=-1, keepdims=True) + EPS) * g


def _store_row_tiled(ref, t0, val):
    rows, d = val.shape
    n_c = d // LANES
    for c in range(n_c):
        ref[pl.ds(t0 * n_c + c, rows, stride=n_c), :] = val[:, c * LANES:(c + 1) * LANES]


def _load_row_tiled(ref, rows, n_c):
    return jnp.concatenate([ref[pl.ds(c, rows, stride=n_c), :] for c in range(n_c)], axis=1)


def _tile(n, pref):
    t = min(n, pref)
    while n % t:
        t //= 2
    return t


def _ada_kernel(c_ref, w_ref, b_ref, o_ref):
    c = c_ref[...]
    a = c * jax.nn.sigmoid(c)
    o_ref[...] = _dot3(a, w_ref[...]) + b_ref[...]


def _ada(cs, w, b):
    m, d = cs.shape
    n = w.shape[1]
    tn = _tile(n, ADA_COL_TILE)
    return pl.pallas_call(
        _ada_kernel,
        grid=(n // tn,),
        in_specs=[pl.BlockSpec((m, d), lambda j: (0, 0)),
                  pl.BlockSpec((d, tn), lambda j: (0, j)),
                  pl.BlockSpec((1, tn), lambda j: (0, j))],
        out_specs=pl.BlockSpec((m, tn), lambda j: (0, j)),
        out_shape=jax.ShapeDtypeStruct((m, n), F32),
        compiler_params=_params("arbitrary"),
        name="ada",
    )(cs, w, b.reshape(1, n))


def _swap32(x):
    lane = lax.broadcasted_iota(jnp.int32, x.shape, 1)
    fwd = pltpu.roll(x, LANES - 32, 1)
    bwd = pltpu.roll(x, 32, 1)
    return jnp.where((lane & 32) == 0, fwd, bwd)


def _norm_head(r, gain, cos, sin, scale):
    y = _rms(r, gain)
    if cos is not None:
        y = y * cos + _swap32(y) * sin
    if scale != 1.0:
        y = y * scale
    return y.astype(BF16)


def _ctx_kv_kernel(x_ref, mod_ref, g_ref, w_ref, kn_ref, k_ref, v_ref, *, n_kv):
    x = x_ref[0]
    inv = lax.rsqrt(jnp.mean(x * x, axis=-1, keepdims=True) + EPS)
    h = ((x * inv) * (g_ref[...] * (1.0 + mod_ref[0, 1:2, :])) + mod_ref[0, 0:1, :]).astype(BF16)
    r = _dot(h, w_ref[...])
    for hh in range(n_kv):
        sl = slice(hh * LANES, (hh + 1) * LANES)
        k_ref[0, :, sl] = _norm_head(r[:, sl], kn_ref[...], None, None, 1.0)
    v_ref[0] = r[:, n_kv * LANES:].astype(BF16)


def _ctx_kv(ctx, mod, mod_row, g_pre, w_in_b, k_norm, kv_w):
    bsz, l, d = ctx.shape
    tm = _tile(l, TOKEN_TILE)
    out = jax.ShapeDtypeStruct((bsz, l, kv_w), BF16)
    tok = lambda b, i: (b, i, 0)
    return pl.pallas_call(
        functools.partial(_ctx_kv_kernel, n_kv=kv_w // LANES),
        grid=(bsz, l // tm),
        in_specs=[pl.BlockSpec((1, tm, d), tok),
                  pl.BlockSpec((1, N_MOD, d), lambda b, i: (mod_row, 0, 0)),
                  pl.BlockSpec((1, d), lambda b, i: (0, 0)),
                  pl.BlockSpec((d, 2 * kv_w), lambda b, i: (0, 0)),
                  pl.BlockSpec((1, LANES), lambda b, i: (0, 0))],
        out_specs=[pl.BlockSpec((1, tm, kv_w), tok)] * 2,
        out_shape=[out, out],
        compiler_params=_params("arbitrary", "arbitrary"),
        name="ctx_kv",
    )(ctx, mod, g_pre.reshape(1, d), w_in_b, k_norm.reshape(1, LANES))


def _gelu(x):
    c = 0.7978845608028654
    return x * (0.5 + 0.5 * jnp.tanh(x * (c + (c * 0.044715) * (x * x))))


def _inproj_kernel(x_ref, mod_ref, g_ref, w_ref, kn_ref, qn_ref, cos_ref, sin_ref, ln_ref, ws_ref, bs_ref,
                   h_ref, k_ref, v_ref, q_ref, gm_ref, *, kv_w, q_w, gm_w, scale):
    x = x_ref[0]
    inv = lax.rsqrt(jnp.mean(x * x, axis=-1, keepdims=True) + EPS)
    h = ((x * inv) * (g_ref[...] * (1.0 + mod_ref[0, 1:2, :])) + mod_ref[0, 0:1, :]).astype(BF16)
    h_ref[0] = h
    cos, sin = cos_ref[...], sin_ref[...]
    tm = h.shape[0]

    rk = _dot(h, w_ref[:, 0:kv_w])
    for hh in range(kv_w // LANES):
        sl = slice(hh * LANES, (hh + 1) * LANES)
        k_ref[0, :, sl] = _norm_head(rk[:, sl], kn_ref[...], cos, sin, 1.0)
    v_ref[0] = _dot(h, w_ref[:, kv_w:2 * kv_w]).astype(BF16)

    col_q = 2 * kv_w
    cq = min(COL_CHUNK, q_w)
    for j in range(q_w // cq):
        r = _dot(h, w_ref[:, col_q + j * cq:col_q + (j + 1) * cq])
        for hh in range(cq // LANES):
            sl = slice(hh * LANES, (hh + 1) * LANES)
            q_ref[0, :, j * cq + hh * LANES:j * cq + (hh + 1) * LANES] = _norm_head(
                r[:, sl], qn_ref[...], cos, sin, scale)

    col_u, col_v = col_q + q_w, col_q + q_w + gm_w
    cg = min(COL_CHUNK, gm_w)
    for j in range(gm_w // cg):
        gu = _gelu(_dot(h, w_ref[:, col_u + j * cg:col_u + (j + 1) * cg]))
        gv = _gelu(_dot(h, w_ref[:, col_v + j * cg:col_v + (j + 1) * cg]))
        for g in range(cg // LANES):
            cs = slice(g * LANES, (g + 1) * LANES)
            oc = slice(j * cg + g * LANES, j * cg + (g + 1) * LANES)
            v = gv[:, cs]
            vc = v - jnp.mean(v, axis=-1, keepdims=True)
            vn = vc * lax.rsqrt(jnp.mean(vc * vc, axis=-1, keepdims=True) + EPS) * ln_ref[:, oc]
            vn = vn.astype(BF16)
            w = ws_ref[j * (cg // LANES) + g].astype(BF16)
            for c in range(tm // LANES):
                rs = slice(c * LANES, (c + 1) * LANES)
                s = _dot(w, vn[rs, :]) + bs_ref[:, oc]
                gm_ref[0, rs, oc] = (gu[rs, cs] * s).astype(BF16)


def _inproj(x, mod, g_pre, w_in_b, k_norm, q_norm, cos, sin, gm_ln, w_s, bs_full, kv_w, q_w, gm_w, scale):
    bsz, l, d = x.shape
    tm = _tile(l, TOKEN_TILE)
    n_cols = 2 * kv_w + q_w + 2 * gm_w
    assert tm % LANES == 0
    tok = lambda b, i: (b, i, 0)
    c2 = lambda b, i: (0, 0)
    return pl.pallas_call(
        functools.partial(_inproj_kernel, kv_w=kv_w, q_w=q_w, gm_w=gm_w, scale=scale),
        grid=(bsz, l // tm),
        in_specs=[pl.BlockSpec((1, tm, d), tok),
                  pl.BlockSpec((1, N_MOD, d), lambda b, i: (b, 0, 0)),
                  pl.BlockSpec((1, d), c2),
                  pl.BlockSpec((d, n_cols), c2, pipeline_mode=pl.Buffered(1)),
                  pl.BlockSpec((1, LANES), c2),
                  pl.BlockSpec((1, LANES), c2),
                  pl.BlockSpec((tm, LANES), lambda b, i: (i, 0)),
                  pl.BlockSpec((tm, LANES), lambda b, i: (i, 0)),
                  pl.BlockSpec((1, gm_w), c2),
                  pl.BlockSpec(w_s.shape, lambda b, i: (0, 0, 0)),
                  pl.BlockSpec(bs_full.shape, c2)],
        out_specs=[pl.BlockSpec((1, tm, d), tok),
                   pl.BlockSpec((1, tm, kv_w), tok),
                   pl.BlockSpec((1, tm, kv_w), tok),
                   pl.BlockSpec((1, tm, q_w), tok),
                   pl.BlockSpec((1, tm, gm_w), tok)],
        out_shape=[jax.ShapeDtypeStruct((bsz, l, d), BF16),
                   jax.ShapeDtypeStruct((bsz, l, kv_w), BF16),
                   jax.ShapeDtypeStruct((bsz, l, kv_w), BF16),
                   jax.ShapeDtypeStruct((bsz, l, q_w), BF16),
                   jax.ShapeDtypeStruct((bsz, l, gm_w), BF16)],
        compiler_params=_params("arbitrary", "arbitrary"),
        name="inproj",
    )(x, mod, g_pre.reshape(1, d), w_in_b, k_norm.reshape(1, LANES), q_norm.reshape(1, LANES), cos, sin,
      gm_ln.reshape(1, gm_w), w_s, bs_full)


def _attn_kernel(q_ref, kc_ref, vc_ref, kx_ref, vx_ref, o_ref, *, n_kv, grp):
    nt = (((1,), (1,)), ((), ()))
    for kv in range(n_kv):
        ks = slice(kv * LANES, (kv + 1) * LANES)
        kc, kx = kc_ref[0, :, ks], kx_ref[0, :, ks]
        vc = jnp.concatenate([vc_ref[0, :, ks], jnp.ones_like(kc)], axis=1)
        vx = jnp.concatenate([vx_ref[0, :, ks], jnp.ones_like(kx)], axis=1)
        for hh in range(grp):
            sl = slice((kv * grp + hh) * LANES, (kv * grp + hh + 1) * LANES)
            q = q_ref[0, :, sl]
            sc = lax.dot_general(q, kc, nt, preferred_element_type=F32)
            sx = lax.dot_general(q, kx, nt, preferred_element_type=F32)
            m = jnp.maximum(jnp.max(sc, axis=-1, keepdims=True), jnp.max(sx, axis=-1, keepdims=True))
            pc = jnp.exp2(sc - m).astype(BF16)
            px = jnp.exp2(sx - m).astype(BF16)
            o = _dot(pc, vc) + _dot(px, vx)
            o_ref[0, :, sl] = (o[:, :LANES] / o[:, LANES:LANES + 1]).astype(BF16)


def _attention(q, kc, vc, kx, vx, n_kv):
    bsz, l, q_w = q.shape
    lc, kv_w = kc.shape[1], kc.shape[2]
    tq = _tile(l, ATTN_Q_TILE)
    tok = lambda b, i: (b, i, 0)
    whole = lambda b, i: (b, 0, 0)
    return pl.pallas_call(
        functools.partial(_attn_kernel, n_kv=n_kv, grp=q_w // kv_w),
        grid=(bsz, l // tq),
        in_specs=[pl.BlockSpec((1, tq, q_w), tok),
                  pl.BlockSpec((1, lc, kv_w), whole),
                  pl.BlockSpec((1, lc, kv_w), whole),
                  pl.BlockSpec((1, l, kv_w), whole),
                  pl.BlockSpec((1, l, kv_w), whole)],
        out_specs=pl.BlockSpec((1, tq, q_w), tok),
        out_shape=jax.ShapeDtypeStruct((bsz, l, q_w), BF16),
        compiler_params=_params("arbitrary", "arbitrary"),
        name="attention",
    )(q, kc, vc, kx, vx)


def _merge_kernel(h_ref, a_ref, g_ref, *refs, n_chunk):
    wga, wgg = refs[:n_chunk], refs[n_chunk:2 * n_chunk]
    wba_ref, wbg_ref, o_ref = refs[2 * n_chunk:]
    h, a, g = h_ref[0], a_ref[0], g_ref[0]
    tn = wga[0].shape[1]
    for j in range(n_chunk):
        cs = slice(j * tn, (j + 1) * tn)
        ga = jax.nn.sigmoid(_dot(h, wga[j][...]))
        gg = jax.nn.sigmoid(_dot(h, wgg[j][...]))
        o_ref[0, :, cs] = (ga * _dot(a, wba_ref[:, cs]) + gg * _dot(g, wbg_ref[:, cs])).astype(BF16)


def _merge(h, attn, gm, w_in_b, w_ba_b, w_bg_b, col_ga, col_gg):
    bsz, l, d = h.shape
    q_w, gm_w = attn.shape[2], gm.shape[2]
    tm = _tile(l, TOKEN_TILE)
    tn = _tile(d, COL_CHUNK)
    assert col_ga % tn == 0 and col_gg % tn == 0
    n_chunk = d // tn
    tok = lambda b, i: (b, i, 0)
    gate_specs = [pl.BlockSpec((d, tn), functools.partial(lambda b, i, c: (0, c), c=(c0 // tn) + j),
                               pipeline_mode=pl.Buffered(1))
                  for c0 in (col_ga, col_gg) for j in range(n_chunk)]
    return pl.pallas_call(
        functools.partial(_merge_kernel, n_chunk=n_chunk),
        grid=(bsz, l // tm),
        in_specs=[pl.BlockSpec((1, tm, d), tok),
                  pl.BlockSpec((1, tm, q_w), tok),
                  pl.BlockSpec((1, tm, gm_w), tok),
                  *gate_specs,
                  pl.BlockSpec((q_w, d), lambda b, i: (0, 0)),
                  pl.BlockSpec((gm_w, d), lambda b, i: (0, 0))],
        out_specs=pl.BlockSpec((1, tm, d), tok),
        out_shape=jax.ShapeDtypeStruct((bsz, l, d), BF16),
        compiler_params=_params("arbitrary", "arbitrary"),
        name="merge",
    )(h, attn, gm, *([w_in_b] * (2 * n_chunk)), w_ba_b, w_bg_b)


R_W0, R_W1 = range(2)
T_E0, T_E1, T_R0_HI, T_R0_LO, T_R1_HI, T_R1_LO = range(6)


def _out_kernel(m_ref, x_ref, mod_ref, gp1_ref, gp2_ref, wo_ref, wr_ref, br_ref,
                x1_ref, h2_ref, ri_ref, rt_ref, cnt_ref, carry_ref, *, n_groups, per_group):
    @pl.when((pl.program_id(0) == 0) & (pl.program_id(1) == 0))
    def _():
        carry_ref[...] = jnp.zeros_like(carry_ref)

    tm = m_ref.shape[1]
    sub = tm
    wr_hi, wr_lo = _split_bf16(wr_ref[...])
    wr_both = jnp.concatenate([wr_hi, wr_lo], axis=1)
    lane = lax.broadcasted_iota(jnp.int32, (sub, LANES), 1)
    row = lax.broadcasted_iota(jnp.int32, (sub, sub), 0)
    col = lax.broadcasted_iota(jnp.int32, (sub, sub), 1)
    tri = jnp.where(col < row, 1.0, 0.0).astype(BF16)
    sel = jnp.where(lax.broadcasted_iota(jnp.int32, (SUBLANES, LANES), 0)
                    == lax.broadcasted_iota(jnp.int32, (SUBLANES, LANES), 1), 1.0, 0.0).astype(BF16)
    neg = jnp.float32(-jnp.inf)
    carry = carry_ref[...]

    for sb in range(tm // sub):
        rs = slice(sb * sub, (sb + 1) * sub)
        mix = _dot(m_ref[0, rs, :], wo_ref[...])
        inv1 = lax.rsqrt(jnp.mean(mix * mix, axis=-1, keepdims=True) + EPS)
        x1 = x_ref[0, rs, :] + (mix * inv1) * (mod_ref[0, 2:3, :] * gp1_ref[...])
        x1_ref[0, rs, :] = x1
        inv2 = lax.rsqrt(jnp.mean(x1 * x1, axis=-1, keepdims=True) + EPS)
        h2 = (x1 * inv2) * (gp2_ref[...] * (1.0 + mod_ref[0, 4:5, :])) + mod_ref[0, 3:4, :]
        h2_ref[0, rs, :] = h2

        h_hi, h_lo = _split_bf16(h2)
        both = _dot(h_hi, wr_both)
        logits = both[:, :LANES] + both[:, LANES:] + _dot(h_lo, wr_hi) + br_ref[...]
        lg = jnp.where(lane < n_groups, logits, neg)
        gmax = jnp.max(lg, axis=-1, keepdims=True)
        p_top = 1.0 / jnp.sum(jnp.exp(lg - gmax), axis=-1, keepdims=True)
        gidx = jnp.min(jnp.where(lg == gmax, lane, LANES), axis=-1, keepdims=True)
        lo = n_groups + gidx * per_group
        le = jnp.where((lane >= lo) & (lane < lo + per_group), logits, neg)
        l1 = jnp.max(le, axis=-1, keepdims=True)
        i1 = jnp.min(jnp.where(le == l1, lane, LANES), axis=-1, keepdims=True)
        le2 = jnp.where(lane == i1, neg, le)
        l2 = jnp.max(le2, axis=-1, keepdims=True)
        i2 = jnp.min(jnp.where(le2 == l2, lane, LANES), axis=-1, keepdims=True)
        r = jnp.exp(l2 - l1)
        w0 = p_top / (1.0 + r)
        w1 = p_top * r / (1.0 + r)

        oh1 = lane == i1
        oh2 = lane == i2
        oh = jnp.where(oh1 | oh2, 1.0, 0.0)
        base = _dot(tri, oh.astype(BF16)) + carry
        rank0 = jnp.sum(jnp.where(oh1, base, 0.0), axis=-1, keepdims=True)
        rank1 = jnp.sum(jnp.where(oh2, base, 0.0), axis=-1, keepdims=True)
        carry = carry + jnp.sum(oh, axis=0, keepdims=True)

        e0 = (i1 - n_groups).astype(F32)
        e1 = (i2 - n_groups).astype(F32)
        rec = jnp.zeros_like(logits)
        for idx, val in ((R_W0, w0), (R_W1, w1)):
            rec = jnp.where(lane == idx, val, rec)
        ri_ref[0, rs, :] = rec

        r0_hi = jnp.floor(rank0 * (1.0 / 256.0))
        r1_hi = jnp.floor(rank1 * (1.0 / 256.0))
        ints = jnp.zeros_like(logits)
        for idx, val in ((T_E0, e0), (T_E1, e1), (T_R0_HI, r0_hi), (T_R0_LO, rank0 - 256.0 * r0_hi),
                         (T_R1_HI, r1_hi), (T_R1_LO, rank1 - 256.0 * r1_hi)):
            ints = jnp.where(lane == idx, val, ints)
        rt = lax.dot_general(sel, ints.astype(BF16), (((1,), (1,)), ((), ())), preferred_element_type=F32)
        rt_ref[:, rs] = rt.astype(jnp.int32)

    carry_ref[...] = carry
    cnt_ref[...] = carry


def _out_route(merged, x, mod, g_post1, g_pre2, w_o_b, w_r, b_r, n_groups, per_group):
    bsz, l, d = x.shape
    tm = _tile(l, TOKEN_TILE)
    tok = lambda b, i: (b, i, 0)
    const2 = lambda b, i: (0, 0)
    return pl.pallas_call(
        functools.partial(_out_kernel, n_groups=n_groups, per_group=per_group),
        grid=(bsz, l // tm),
        in_specs=[pl.BlockSpec((1, tm, d), tok),
                  pl.BlockSpec((1, tm, d), tok),
                  pl.BlockSpec((1, N_MOD, d), lambda b, i: (b, 0, 0)),
                  pl.BlockSpec((1, d), const2),
                  pl.BlockSpec((1, d), const2),
                  pl.BlockSpec((d, d), const2),
                  pl.BlockSpec((d, LANES), const2),
                  pl.BlockSpec((1, LANES), const2)],
        out_specs=[pl.BlockSpec((1, tm, d), tok),
                   pl.BlockSpec((1, tm, d), tok),
                   pl.BlockSpec((1, tm, LANES), tok),
                   pl.BlockSpec((SUBLANES, tm), lambda b, i: (0, b * (l // tm) + i)),
                   pl.BlockSpec((1, LANES), const2)],
        out_shape=[jax.ShapeDtypeStruct((bsz, l, d), F32),
                   jax.ShapeDtypeStruct((bsz, l, d), F32),
                   jax.ShapeDtypeStruct((bsz, l, LANES), F32),
                   jax.ShapeDtypeStruct((SUBLANES, bsz * l), jnp.int32),
                   jax.ShapeDtypeStruct((1, LANES), F32)],
        scratch_shapes=[pltpu.VMEM((1, LANES), F32)],
        compiler_params=_params("arbitrary", "arbitrary"),
        name="out_route",
    )(merged, x, mod, g_post1.reshape(1, d), g_pre2.reshape(1, d), w_o_b, w_r, b_r)


def _dispatch_kernel(pos_ref, zf_ref, h_ref, xs_ref, zbuf, stage, sem, zsem, *, tm, n_c, n_tiles, n_tok, n_steps):
    i = pl.program_id(0)
    tile_rows = EXPERT_TILE * n_c
    prev = (i + 1) % 2

    def zero_copy(t):
        return pltpu.make_async_copy(zbuf, xs_ref.at[pl.ds(pl.multiple_of(t * tile_rows, tile_rows), tile_rows)], zsem)

    @pl.when(i == 0)
    def _():
        zbuf[...] = jnp.zeros_like(zbuf)

        def issue(t, c):
            @pl.when(zf_ref[t] != 0)
            def _():
                zero_copy(t).start()
            return c

        def drain(t, c):
            @pl.when(zf_ref[t] != 0)
            def _():
                zero_copy(t).wait()
            return c

        lax.fori_loop(0, n_tiles, issue, 0)
        lax.fori_loop(0, n_tiles, drain, 0)

    def issue_rows(rb, c):
        for u in range(DMA_UNROLL):
            r = rb * DMA_UNROLL + u
            src = stage.at[prev, pl.ds(pl.multiple_of(r * n_c, n_c), n_c)]
            for k in range(TOP_K):
                p = pos_ref[k * n_tok + (i - 1) * tm + r]
                pltpu.make_async_copy(src, xs_ref.at[pl.ds(pl.multiple_of(p * n_c, n_c), n_c)],
                                      sem).start(priority=k % 2)
        return c

    @pl.when(i >= 1)
    def _():
        lax.fori_loop(0, tm // DMA_UNROLL, issue_rows, 0)

    @pl.when(i < n_steps)
    def _():
        _store_row_tiled(stage.at[i % 2], 0, h_ref[...])

    @pl.when(i >= 1)
    def _():
        for k in range(TOP_K):
            pltpu.make_async_copy(stage.at[prev], xs_ref.at[pl.ds(0, tm * n_c)], sem).wait()


def _dispatch(h2, pos, zflag, n_rows):
    n, d = h2.shape
    n_c = d // LANES
    tm = _tile(n, DISPATCH_TILE)
    n_steps = n // tm
    n_tiles = n_rows // EXPERT_TILE
    return pl.pallas_call(
        functools.partial(_dispatch_kernel, tm=tm, n_c=n_c, n_tiles=n_tiles, n_tok=n, n_steps=n_steps),
        grid_spec=pltpu.PrefetchScalarGridSpec(
            num_scalar_prefetch=2,
            grid=(n_steps + 1,),
            in_specs=[pl.BlockSpec((tm, d), lambda i, p, z: (jnp.minimum(i, n_steps - 1), 0))],
            out_specs=pl.BlockSpec(memory_space=pl.ANY),
            scratch_shapes=[pltpu.VMEM((EXPERT_TILE * n_c, LANES), F32),
                            pltpu.VMEM((2, tm * n_c, LANES), F32),
                            pltpu.SemaphoreType.DMA(()),
                            pltpu.SemaphoreType.DMA(())]),
        out_shape=jax.ShapeDtypeStruct((n_rows * n_c, LANES), F32),
        compiler_params=_params("arbitrary"),
        name="dispatch",
    )(pos, zflag, h2)


def _cast_rows(src_ref, dst_ref, chunk=256):
    rows = dst_ref.shape[0]
    chunk = min(chunk, rows)

    def body(c, carry):
        r0 = pl.multiple_of(c * chunk, chunk)
        dst_ref[pl.ds(r0, chunk), :] = src_ref[pl.ds(r0, chunk), :].astype(BF16)
        return carry

    lax.fori_loop(0, rows // chunk, body, 0)


def _experts_kernel(te_ref, chg_ref, nxt_ref, nt_ref, xs_ref, wg_hbm, wu_hbm, wd_hbm, o_ref,
                    wf_g, wf_u, wf_d, wb_g, wb_u, wb_d, sem):
    t = pl.program_id(0)
    stages = ((wg_hbm, wf_g, wb_g), (wu_hbm, wf_u, wb_u), (wd_hbm, wf_d, wb_d))

    def fetch(e):
        return [pltpu.make_async_copy(src.at[e], dst, sem) for src, dst, _ in stages]

    @pl.when(t < nt_ref[0])
    def _():
        @pl.when(chg_ref[t] != 0)
        def _():
            @pl.when(t == 0)
            def _():
                for cp in fetch(te_ref[0]):
                    cp.start(priority=1)

            for cp in fetch(te_ref[t]):
                cp.wait()
            for _, wf, wb in stages:
                _cast_rows(wf, wb)

            @pl.when(nxt_ref[t] >= 0)
            def _():
                for cp in fetch(nxt_ref[t]):
                    cp.start(priority=1)

        x = _load_row_tiled(xs_ref, EXPERT_TILE, wb_g.shape[0] // LANES).astype(BF16)
        g = _dot(x, wb_g[...])
        u = _dot(x, wb_u[...])
        hid = (g * jax.nn.sigmoid(g) * u).astype(BF16)
        o_ref[...] = _dot(hid, wb_d[...])

    @pl.when(t >= nt_ref[0])
    def _():
        o_ref[...] = jnp.zeros_like(o_ref)


def _experts(xs, w_gate, w_up, w_down, plan):
    d, de = w_gate.shape[1], w_gate.shape[2]
    n_c = d // LANES
    n_rows = xs.shape[0] // n_c
    tile = lambda t, *_: (t, 0)
    hbm = pl.BlockSpec(memory_space=pl.ANY)
    return pl.pallas_call(
        _experts_kernel,
        grid_spec=pltpu.PrefetchScalarGridSpec(
            num_scalar_prefetch=len(plan),
            grid=(n_rows // EXPERT_TILE,),
            in_specs=[pl.BlockSpec((EXPERT_TILE * n_c, LANES),
                                   lambda t, te, chg, nxt, nt: (jnp.minimum(t, nt[0] - 1), 0)), hbm, hbm, hbm],
            out_specs=pl.BlockSpec((EXPERT_TILE, d), tile),
            scratch_shapes=[pltpu.VMEM((d, de), F32), pltpu.VMEM((d, de), F32), pltpu.VMEM((de, d), F32),
                            pltpu.VMEM((d, de), BF16), pltpu.VMEM((d, de), BF16), pltpu.VMEM((de, d), BF16),
                            pltpu.SemaphoreType.DMA(())]),
        out_shape=jax.ShapeDtypeStruct((n_rows, d), F32),
        compiler_params=_params("arbitrary"),
        name="experts",
    )(*plan, xs, w_gate, w_up, w_down)


def _combine_kernel(pos_ref, x1_ref, ri_ref, mod_ref, gp_ref, ys_ref, o_ref, buf, sem, *, tm, n_l, n_tok):
    step = pl.program_id(0) * n_l + pl.program_id(1)
    n_steps = pl.num_programs(0) * n_l
    slot = step % 2

    def issue(s, sl):
        def body(rb, c):
            for u in range(DMA_UNROLL):
                r = rb * DMA_UNROLL + u
                for k in range(TOP_K):
                    p = pos_ref[k * n_tok + s * tm + r]
                    pltpu.make_async_copy(ys_ref.at[pl.ds(p, 1)], buf.at[sl, k, rb, pl.ds(u, 1)],
                                          sem.at[sl]).start(priority=k % 2)
            return c

        lax.fori_loop(0, tm // DMA_UNROLL, body, 0)

    @pl.when(step == 0)
    def _():
        issue(0, 0)

    @pl.when(step + 1 < n_steps)
    def _():
        issue(step + 1, 1 - slot)

    for k in range(TOP_K):
        pltpu.make_async_copy(buf.at[slot, k], buf.at[slot, k], sem.at[slot]).wait()

    ri = ri_ref[0]
    d = o_ref.shape[-1]
    moe = (ri[:, R_W0:R_W0 + 1] * buf[slot, 0].reshape(tm, d)
           + ri[:, R_W1:R_W1 + 1] * buf[slot, 1].reshape(tm, d))
    o_ref[0] = x1_ref[0] + mod_ref[0, 5:6, :] * _rms(moe, gp_ref[...])


def _combine(x1, rinfo, mod, g_post2, ys, pos):
    bsz, l, d = x1.shape
    tm = _tile(l, COMBINE_TILE)
    n_l = l // tm
    tok = lambda b, i, p: (b, i, 0)
    return pl.pallas_call(
        functools.partial(_combine_kernel, tm=tm, n_l=n_l, n_tok=bsz * l),
        grid_spec=pltpu.PrefetchScalarGridSpec(
            num_scalar_prefetch=1,
            grid=(bsz, n_l),
            in_specs=[pl.BlockSpec((1, tm, d), tok),
                      pl.BlockSpec((1, tm, LANES), tok),
                      pl.BlockSpec((1, N_MOD, d), lambda b, i, p: (b, 0, 0)),
                      pl.BlockSpec((1, d), lambda b, i, p: (0, 0)),
                      pl.BlockSpec(memory_space=pl.ANY)],
            out_specs=pl.BlockSpec((1, tm, d), tok),
            scratch_shapes=[pltpu.VMEM((2, TOP_K, tm // DMA_UNROLL, DMA_UNROLL, d), F32),
                            pltpu.SemaphoreType.DMA((2,))]),
        out_shape=jax.ShapeDtypeStruct((bsz, l, d), F32),
        compiler_params=_params("arbitrary", "arbitrary"),
        name="combine",
    )(pos, x1, rinfo, mod, g_post2.reshape(1, d), ys)


def _rope_tables(seq, head_dim):
    axis_dim = head_dim // 2
    t = jnp.arange(seq, dtype=jnp.int32)
    pos = jnp.stack([t // GRID_W, t % GRID_W], axis=-1).astype(F32)
    inv_freq = ROPE_THETA ** (-jnp.arange(0, axis_dim, 2, dtype=F32) / axis_dim)
    ang = pos[:, :, None] * inv_freq
    cos, sin = jnp.cos(ang), jnp.sin(ang)
    cos_t = jnp.concatenate([cos, cos], axis=-1).reshape(seq, head_dim)
    sin_t = jnp.concatenate([-sin, sin], axis=-1).reshape(seq, head_dim)
    return cos_t, sin_t


def _route_plan(rt, cnt, n_groups, n_experts, n_tiles):
    e = rt[T_E0:T_E1 + 1]
    rank = jnp.stack([rt[T_R0_HI] * 256 + rt[T_R0_LO], rt[T_R1_HI] * 256 + rt[T_R1_LO]])
    counts = cnt[0, n_groups:n_groups + n_experts].astype(jnp.int32)
    tiles_e = (counts + EXPERT_TILE - 1) // EXPERT_TILE
    ids = jnp.arange(n_experts, dtype=jnp.int32)
    tile_end = jnp.sum(jnp.where(ids[None, :] <= ids[:, None], tiles_e[None, :], 0), axis=1)
    tile_start = tile_end - tiles_e
    nt = tile_end[-1]
    row0 = tile_start * EXPERT_TILE
    pos = jnp.sum(jnp.where(e[None] == ids[:, None, None], row0[:, None, None], 0), axis=0) + rank
    t = jnp.arange(n_tiles, dtype=jnp.int32)
    owner = lambda q: jnp.sum((tile_end[None, :] <= q[:, None]).astype(jnp.int32), axis=1)
    te = owner(jnp.minimum(t, nt - 1))
    chg = ((t == 0) | (te != owner(jnp.minimum(jnp.maximum(t - 1, 0), nt - 1)))).astype(jnp.int32)
    partial_last = jnp.any((tile_end[None, :] - 1 == t[:, None]) & (counts[None, :] % EXPERT_TILE != 0), axis=1)
    zflag = ((t >= nt) | partial_last).astype(jnp.int32)
    used = tiles_e > 0
    later = used[None, :] & (ids[None, :] > te[:, None])
    nxt = jnp.min(jnp.where(later, ids[None, :], n_experts), axis=1)
    nxt = jnp.where(nxt == n_experts, -1, nxt)
    return pos.reshape(-1), (te, chg, nxt, nt.reshape(1)), zflag


def kernel(x, c, ctx, c_ctx, w_ada, b_ada, g_pre1, g_post1, g_pre2, g_post2, w_in, q_norm, k_norm,
           gm_ln, w_s, b_s, w_ba, w_bg, w_o, w_rg, b_rg, w_re, b_re, w_gate, w_up, w_down):
    bsz, seq, d = x.shape
    depth = w_ada.shape[0]
    head_dim = q_norm.shape[-1]
    q_w, gm_w = w_ba.shape[1], w_bg.shape[1]
    kv_w = (w_in.shape[2] - q_w - 2 * gm_w - 2 * d) // 2
    n_groups, per_group = w_re.shape[2], w_re.shape[3]
    n_experts = n_groups * per_group
    assert head_dim == LANES and w_s.shape[2] == LANES and gm_w // w_s.shape[1] == LANES
    assert n_groups + n_experts <= LANES and seq % GRID_W == 0
    col_q = 2 * kv_w
    col_u, col_v = col_q + q_w, col_q + q_w + gm_w
    col_ga, col_gg = col_v + gm_w, col_v + gm_w + d
    n_tok = bsz * seq
    n_rows = n_tok * TOP_K + n_experts * EXPERT_TILE
    n_tiles = n_rows // EXPERT_TILE

    cos_t, sin_t = _rope_tables(seq, head_dim)
    pad = (-(bsz + 1)) % (2 * SUBLANES)
    cs = jnp.concatenate([c, c_ctx[None, :], jnp.zeros((pad, d), F32)], axis=0)

    for l in range(depth):
        assert l + 1 == depth, "context-stream update for non-final layers is not implemented"
        mod = _ada(cs, w_ada[l], b_ada[l]).reshape(cs.shape[0], N_MOD, d)
        w_in_b = w_in[l].astype(BF16)
        w_ba_b, w_bg_b, w_o_b = w_ba[l].astype(BF16), w_bg[l].astype(BF16), w_o[l].astype(BF16)
        bs_full = jnp.repeat(b_s[l].T, LANES, axis=1)
        w_r = jnp.concatenate([w_rg[l], w_re[l].reshape(d, n_experts),
                               jnp.zeros((d, LANES - n_groups - n_experts), F32)], axis=1)
        b_r = jnp.concatenate([b_rg[l], b_re[l].reshape(n_experts),
                               jnp.zeros((LANES - n_groups - n_experts,), F32)]).reshape(1, LANES)

        kc, vc = _ctx_kv(ctx, mod, bsz, g_pre1[l], w_in_b, k_norm[l], kv_w)
        hx, kx, vx, qx, gm = _inproj(x, mod, g_pre1[l], w_in_b, k_norm[l], q_norm[l], cos_t, sin_t, gm_ln[l],
                                     w_s[l], bs_full, kv_w, q_w, gm_w, head_dim ** -0.5 * LOG2_E)
        attn = _attention(qx, kc, vc, kx, vx, kv_w // head_dim)
        merged = _merge(hx, attn, gm, w_in_b, w_ba_b, w_bg_b, col_ga, col_gg)

        x1, h2, rinfo, rt, cnt = _out_route(merged, x, mod, g_post1[l], g_pre2[l], w_o_b, w_r, b_r,
                                            n_groups, per_group)
        pos, plan, zflag = _route_plan(rt, cnt, n_groups, n_experts, n_tiles)

        xs = _dispatch(h2.reshape(n_tok, d), pos, zflag, n_rows)
        ys = _experts(xs, w_gate[l], w_up[l], w_down[l], plan)
        x = _combine(x1, rinfo, mod, g_post2[l], ys, pos)
    return x
```

```python
import functools

import jax
import jax.numpy as jnp
from jax import lax
from jax.experimental import pallas as pl
from jax.experimental.pallas import tpu as pltpu

GRID_W = 64
ROPE_THETA = 10000.0
EPS = 1e-6
N_MOD = 6
TOP_K = 2
LOG2_E = 1.4426950408889634

LANES = 128
SUBLANES = 8
VMEM_LIMIT_BYTES = 56 * 1024 * 1024

EXPERT_TILE = 256
TOKEN_TILE = 512
ATTN_Q_TILE = 1024
DISPATCH_TILE = 512
COMBINE_TILE = 256
ADA_COL_TILE = 1536
COL_CHUNK = 512
DMA_UNROLL = SUBLANES

F32 = jnp.float32
BF16 = jnp.bfloat16


def _params(*sem):
    return pltpu.CompilerParams(dimension_semantics=sem, vmem_limit_bytes=VMEM_LIMIT_BYTES)


def _dot(a, b):
    return jnp.dot(a, b, preferred_element_type=F32)


def _split_bf16(a):
    hi = a.astype(BF16)
    lo = (a - hi.astype(F32)).astype(BF16)
    return hi, lo


def _dot3(a, w):
    a_hi, a_lo = _split_bf16(a)
    w_hi, w_lo = _split_bf16(w)
    return _dot(a_hi, w_hi) + _dot(a_lo, w_hi) + _dot(a_hi, w_lo)


def _rms(x, g):
    return x * lax.rsqrt(jnp.mean(x * x, axis=-1, keepdims=True) + EPS) * g


def _store_row_tiled(ref, t0, val):
    rows, d = val.shape
    n_c = d // LANES
    for c in range(n_c):
        ref[pl.ds(t0 * n_c + c, rows, stride=n_c), :] = val[:, c * LANES:(c + 1) * LANES]


def _load_row_tiled(ref, rows, n_c):
    return jnp.concatenate([ref[pl.ds(c, rows, stride=n_c), :] for c in range(n_c)], axis=1)


def _tile(n, pref):
    t = min(n, pref)
    while n % t:
        t //= 2
    return t


def _ada_kernel(c_ref, w_ref, b_ref, o_ref):
    c = c_ref[...]
    a = c * jax.nn.sigmoid(c)
    o_ref[...] = _dot3(a, w_ref[...]) + b_ref[...]


def _ada(cs, w, b):
    m, d = cs.shape
    n = w.shape[1]
    tn = _tile(n, ADA_COL_TILE)
    return pl.pallas_call(
        _ada_kernel,
        grid=(n // tn,),
        in_specs=[pl.BlockSpec((m, d), lambda j: (0, 0)),
                  pl.BlockSpec((d, tn), lambda j: (0, j)),
                  pl.BlockSpec((1, tn), lambda j: (0, j))],
        out_specs=pl.BlockSpec((m, tn), lambda j: (0, j)),
        out_shape=jax.ShapeDtypeStruct((m, n), F32),
        compiler_params=_params("arbitrary"),
        name="ada",
    )(cs, w, b.reshape(1, n))


def _swap32(x):
    i = lax.broadcasted_iota(jnp.int32, (LANES, LANES), 0)
    j = lax.broadcasted_iota(jnp.int32, (LANES, LANES), 1)
    perm = jnp.where(i == (j ^ 32), 1.0, 0.0).astype(BF16)
    return _dot(x.astype(BF16), perm)


def _norm_head(r, gain, cos, sin, scale):
    y = _rms(r, gain)
    if cos is not None:
        y = y * cos + _swap32(y) * sin
    if scale != 1.0:
        y = y * scale
    return y.astype(BF16)


def _ctx_kv_kernel(x_ref, mod_ref, g_ref, w_ref, kn_ref, k_ref, v_ref, *, n_kv):
    x = x_ref[0]
    inv = lax.rsqrt(jnp.mean(x * x, axis=-1, keepdims=True) + EPS)
    h = ((x * inv) * (g_ref[...] * (1.0 + mod_ref[0, 1:2, :])) + mod_ref[0, 0:1, :]).astype(BF16)
    r = _dot(h, w_ref[...])
    for hh in range(n_kv):
        sl = slice(hh * LANES, (hh + 1) * LANES)
        k_ref[0, :, sl] = _norm_head(r[:, sl], kn_ref[...], None, None, 1.0)
    v_ref[0] = r[:, n_kv * LANES:].astype(BF16)


def _ctx_kv(ctx, mod, mod_row, g_pre, w_in_b, k_norm, kv_w):
    bsz, l, d = ctx.shape
    tm = _tile(l, TOKEN_TILE)
    out = jax.ShapeDtypeStruct((bsz, l, kv_w), BF16)
    tok = lambda b, i: (b, i, 0)
    return pl.pallas_call(
        functools.partial(_ctx_kv_kernel, n_kv=kv_w // LANES),
        grid=(bsz, l // tm),
        in_specs=[pl.BlockSpec((1, tm, d), tok),
                  pl.BlockSpec((1, N_MOD, d), lambda b, i: (mod_row, 0, 0)),
                  pl.BlockSpec((1, d), lambda b, i: (0, 0)),
                  pl.BlockSpec((d, 2 * kv_w), lambda b, i: (0, 0)),
                  pl.BlockSpec((1, LANES), lambda b, i: (0, 0))],
        out_specs=[pl.BlockSpec((1, tm, kv_w), tok)] * 2,
        out_shape=[out, out],
        compiler_params=_params("arbitrary", "arbitrary"),
        name="ctx_kv",
    )(ctx, mod, g_pre.reshape(1, d), w_in_b, k_norm.reshape(1, LANES))


def _gelu(x):
    c = 0.7978845608028654
    return x * (0.5 + 0.5 * jnp.tanh(x * (c + (c * 0.044715) * (x * x))))


def _inproj_kernel(x_ref, mod_ref, g_ref, w_ref, kn_ref, qn_ref, cos_ref, sin_ref, ln_ref, ws_ref, bs_ref,
                   h_ref, k_ref, v_ref, q_ref, gm_ref, *, kv_w, q_w, gm_w, scale):
    x = x_ref[0]
    inv = lax.rsqrt(jnp.mean(x * x, axis=-1, keepdims=True) + EPS)
    h = ((x * inv) * (g_ref[...] * (1.0 + mod_ref[0, 1:2, :])) + mod_ref[0, 0:1, :]).astype(BF16)
    h_ref[0] = h
    cos, sin = cos_ref[...], sin_ref[...]
    tm = h.shape[0]

    rk = _dot(h, w_ref[:, 0:kv_w])
    for hh in range(kv_w // LANES):
        sl = slice(hh * LANES, (hh + 1) * LANES)
        k_ref[0, :, sl] = _norm_head(rk[:, sl], kn_ref[...], cos, sin, 1.0)
    v_ref[0] = _dot(h, w_ref[:, kv_w:2 * kv_w]).astype(BF16)

    col_q = 2 * kv_w
    cq = min(COL_CHUNK, q_w)
    for j in range(q_w // cq):
        r = _dot(h, w_ref[:, col_q + j * cq:col_q + (j + 1) * cq])
        for hh in range(cq // LANES):
            sl = slice(hh * LANES, (hh + 1) * LANES)
            q_ref[0, :, j * cq + hh * LANES:j * cq + (hh + 1) * LANES] = _norm_head(
                r[:, sl], qn_ref[...], cos, sin, scale)

    col_u, col_v = col_q + q_w, col_q + q_w + gm_w
    cg = min(COL_CHUNK, gm_w)
    for j in range(gm_w // cg):
        gu = _gelu(_dot(h, w_ref[:, col_u + j * cg:col_u + (j + 1) * cg]))
        gv = _gelu(_dot(h, w_ref[:, col_v + j * cg:col_v + (j + 1) * cg]))
        for g in range(cg // LANES):
            cs = slice(g * LANES, (g + 1) * LANES)
            oc = slice(j * cg + g * LANES, j * cg + (g + 1) * LANES)
            v = gv[:, cs]
            vc = v - jnp.mean(v, axis=-1, keepdims=True)
            vn = vc * lax.rsqrt(jnp.mean(vc * vc, axis=-1, keepdims=True) + EPS) * ln_ref[:, oc]
            vn = vn.astype(BF16)
            w = ws_ref[j * (cg // LANES) + g].astype(BF16)
            for c in range(tm // LANES):
                rs = slice(c * LANES, (c + 1) * LANES)
                s = _dot(w, vn[rs, :]) + bs_ref[:, oc]
                gm_ref[0, rs, oc] = (gu[rs, cs] * s).astype(BF16)


def _inproj(x, mod, g_pre, w_in_b, k_norm, q_norm, cos, sin, gm_ln, w_s, bs_full, kv_w, q_w, gm_w, scale):
    bsz, l, d = x.shape
    tm = _tile(l, TOKEN_TILE)
    n_cols = 2 * kv_w + q_w + 2 * gm_w
    assert tm % LANES == 0
    tok = lambda b, i: (b, i, 0)
    c2 = lambda b, i: (0, 0)
    return pl.pallas_call(
        functools.partial(_inproj_kernel, kv_w=kv_w, q_w=q_w, gm_w=gm_w, scale=scale),
        grid=(bsz, l // tm),
        in_specs=[pl.BlockSpec((1, tm, d), tok),
                  pl.BlockSpec((1, N_MOD, d), lambda b, i: (b, 0, 0)),
                  pl.BlockSpec((1, d), c2),
                  pl.BlockSpec((d, n_cols), c2, pipeline_mode=pl.Buffered(1)),
                  pl.BlockSpec((1, LANES), c2),
                  pl.BlockSpec((1, LANES), c2),
                  pl.BlockSpec((tm, LANES), lambda b, i: (i, 0)),
                  pl.BlockSpec((tm, LANES), lambda b, i: (i, 0)),
                  pl.BlockSpec((1, gm_w), c2),
                  pl.BlockSpec(w_s.shape, lambda b, i: (0, 0, 0)),
                  pl.BlockSpec(bs_full.shape, c2)],
        out_specs=[pl.BlockSpec((1, tm, d), tok),
                   pl.BlockSpec((1, tm, kv_w), tok),
                   pl.BlockSpec((1, tm, kv_w), tok),
                   pl.BlockSpec((1, tm, q_w), tok),
                   pl.BlockSpec((1, tm, gm_w), tok)],
        out_shape=[jax.ShapeDtypeStruct((bsz, l, d), BF16),
                   jax.ShapeDtypeStruct((bsz, l, kv_w), BF16),
                   jax.ShapeDtypeStruct((bsz, l, kv_w), BF16),
                   jax.ShapeDtypeStruct((bsz, l, q_w), BF16),
                   jax.ShapeDtypeStruct((bsz, l, gm_w), BF16)],
        compiler_params=_params("arbitrary", "arbitrary"),
        name="inproj",
    )(x, mod, g_pre.reshape(1, d), w_in_b, k_norm.reshape(1, LANES), q_norm.reshape(1, LANES), cos, sin,
      gm_ln.reshape(1, gm_w), w_s, bs_full)


def _attn_kernel(q_ref, kc_ref, vc_ref, kx_ref, vx_ref, o_ref, *, n_kv, grp):
    nt = (((1,), (1,)), ((), ()))
    for kv in range(n_kv):
        ks = slice(kv * LANES, (kv + 1) * LANES)
        kc, kx = kc_ref[0, :, ks], kx_ref[0, :, ks]
        vc = jnp.concatenate([vc_ref[0, :, ks], jnp.ones_like(kc)], axis=1)
        vx = jnp.concatenate([vx_ref[0, :, ks], jnp.ones_like(kx)], axis=1)
        for hh in range(grp):
            sl = slice((kv * grp + hh) * LANES, (kv * grp + hh + 1) * LANES)
            q = q_ref[0, :, sl]
            sc = lax.dot_general(q, kc, nt, preferred_element_type=F32)
            sx = lax.dot_general(q, kx, nt, preferred_element_type=F32)
            m = jnp.maximum(jnp.max(sc, axis=-1, keepdims=True), jnp.max(sx, axis=-1, keepdims=True))
            pc = jnp.exp2(sc - m).astype(BF16)
            px = jnp.exp2(sx - m).astype(BF16)
            o = _dot(pc, vc) + _dot(px, vx)
            o_ref[0, :, sl] = (o[:, :LANES] / o[:, LANES:LANES + 1]).astype(BF16)


def _attention(q, kc, vc, kx, vx, n_kv):
    bsz, l, q_w = q.shape
    lc, kv_w = kc.shape[1], kc.shape[2]
    tq = _tile(l, ATTN_Q_TILE)
    tok = lambda b, i: (b, i, 0)
    whole = lambda b, i: (b, 0, 0)
    return pl.pallas_call(
        functools.partial(_attn_kernel, n_kv=n_kv, grp=q_w // kv_w),
        grid=(bsz, l // tq),
        in_specs=[pl.BlockSpec((1, tq, q_w), tok),
                  pl.BlockSpec((1, lc, kv_w), whole),
                  pl.BlockSpec((1, lc, kv_w), whole),
                  pl.BlockSpec((1, l, kv_w), whole),
                  pl.BlockSpec((1, l, kv_w), whole)],
        out_specs=pl.BlockSpec((1, tq, q_w), tok),
        out_shape=jax.ShapeDtypeStruct((bsz, l, q_w), BF16),
        compiler_params=_params("arbitrary", "arbitrary"),
        name="attention",
    )(q, kc, vc, kx, vx)


def _merge_kernel(h_ref, a_ref, g_ref, *refs, n_chunk):
    wga, wgg = refs[:n_chunk], refs[n_chunk:2 * n_chunk]
    wba_ref, wbg_ref, o_ref = refs[2 * n_chunk:]
    h, a, g = h_ref[0], a_ref[0], g_ref[0]
    tn = wga[0].shape[1]
    for j in range(n_chunk):
        cs = slice(j * tn, (j + 1) * tn)
        ga = jax.nn.sigmoid(_dot(h, wga[j][...]))
        gg = jax.nn.sigmoid(_dot(h, wgg[j][...]))
        o_ref[0, :, cs] = (ga * _dot(a, wba_ref[:, cs]) + gg * _dot(g, wbg_ref[:, cs])).astype(BF16)


def _merge(h, attn, gm, w_in_b, w_ba_b, w_bg_b, col_ga, col_gg):
    bsz, l, d = h.shape
    q_w, gm_w = attn.shape[2], gm.shape[2]
    tm = _tile(l, TOKEN_TILE)
    tn = _tile(d, COL_CHUNK)
    assert col_ga % tn == 0 and col_gg % tn == 0
    n_chunk = d // tn
    tok = lambda b, i: (b, i, 0)
    gate_specs = [pl.BlockSpec((d, tn), functools.partial(lambda b, i, c: (0, c), c=(c0 // tn) + j),
                               pipeline_mode=pl.Buffered(1))
                  for c0 in (col_ga, col_gg) for j in range(n_chunk)]
    return pl.pallas_call(
        functools.partial(_merge_kernel, n_chunk=n_chunk),
        grid=(bsz, l // tm),
        in_specs=[pl.BlockSpec((1, tm, d), tok),
                  pl.BlockSpec((1, tm, q_w), tok),
                  pl.BlockSpec((1, tm, gm_w), tok),
                  *gate_specs,
                  pl.BlockSpec((q_w, d), lambda b, i: (0, 0)),
                  pl.BlockSpec((gm_w, d), lambda b, i: (0, 0))],
        out_specs=pl.BlockSpec((1, tm, d), tok),
        out_shape=jax.ShapeDtypeStruct((bsz, l, d), BF16),
        compiler_params=_params("arbitrary", "arbitrary"),
        name="merge",
    )(h, attn, gm, *([w_in_b] * (2 * n_chunk)), w_ba_b, w_bg_b)


R_W0, R_W1 = range(2)
T_E0, T_E1, T_R0_HI, T_R0_LO, T_R1_HI, T_R1_LO = range(6)


def _out_kernel(m_ref, x_ref, mod_ref, gp1_ref, gp2_ref, wo_ref, wr_ref, br_ref,
                x1_ref, h2_ref, ri_ref, rt_ref, cnt_ref, carry_ref, *, n_groups, per_group):
    @pl.when((pl.program_id(0) == 0) & (pl.program_id(1) == 0))
    def _():
        carry_ref[...] = jnp.zeros_like(carry_ref)

    tm = m_ref.shape[1]
    sub = tm
    wr_hi, wr_lo = _split_bf16(wr_ref[...])
    wr_both = jnp.concatenate([wr_hi, wr_lo], axis=1)
    lane = lax.broadcasted_iota(jnp.int32, (sub, LANES), 1)
    row = lax.broadcasted_iota(jnp.int32, (sub, sub), 0)
    col = lax.broadcasted_iota(jnp.int32, (sub, sub), 1)
    tri = jnp.where(col < row, 1.0, 0.0).astype(BF16)
    sel = jnp.where(lax.broadcasted_iota(jnp.int32, (SUBLANES, LANES), 0)
                    == lax.broadcasted_iota(jnp.int32, (SUBLANES, LANES), 1), 1.0, 0.0).astype(BF16)
    neg = jnp.float32(-jnp.inf)
    carry = carry_ref[...]

    for sb in range(tm // sub):
        rs = slice(sb * sub, (sb + 1) * sub)
        mix = _dot(m_ref[0, rs, :], wo_ref[...])
        inv1 = lax.rsqrt(jnp.mean(mix * mix, axis=-1, keepdims=True) + EPS)
        x1 = x_ref[0, rs, :] + (mix * inv1) * (mod_ref[0, 2:3, :] * gp1_ref[...])
        x1_ref[0, rs, :] = x1
        inv2 = lax.rsqrt(jnp.mean(x1 * x1, axis=-1, keepdims=True) + EPS)
        h2 = (x1 * inv2) * (gp2_ref[...] * (1.0 + mod_ref[0, 4:5, :])) + mod_ref[0, 3:4, :]
        h2_ref[0, rs, :] = h2

        h_hi, h_lo = _split_bf16(h2)
        both = _dot(h_hi, wr_both)
        logits = both[:, :LANES] + both[:, LANES:] + _dot(h_lo, wr_hi) + br_ref[...]
        lg = jnp.where(lane < n_groups, logits, neg)
        gmax = jnp.max(lg, axis=-1, keepdims=True)
        p_top = 1.0 / jnp.sum(jnp.exp(lg - gmax), axis=-1, keepdims=True)
        gidx = jnp.min(jnp.where(lg == gmax, lane, LANES), axis=-1, keepdims=True)
        lo = n_groups + gidx * per_group
        le = jnp.where((lane >= lo) & (lane < lo + per_group), logits, neg)
        l1 = jnp.max(le, axis=-1, keepdims=True)
        i1 = jnp.min(jnp.where(le == l1, lane, LANES), axis=-1, keepdims=True)
        le2 = jnp.where(lane == i1, neg, le)
        l2 = jnp.max(le2, axis=-1, keepdims=True)
        i2 = jnp.min(jnp.where(le2 == l2, lane, LANES), axis=-1, keepdims=True)
        r = jnp.exp(l2 - l1)
        w0 = p_top / (1.0 + r)
        w1 = p_top * r / (1.0 + r)

        oh1 = lane == i1
        oh2 = lane == i2
        oh = jnp.where(oh1 | oh2, 1.0, 0.0)
        base = _dot(tri, oh.astype(BF16)) + carry
        rank0 = jnp.sum(jnp.where(oh1, base, 0.0), axis=-1, keepdims=True)
        rank1 = jnp.sum(jnp.where(oh2, base, 0.0), axis=-1, keepdims=True)
        carry = carry + jnp.sum(oh, axis=0, keepdims=True)

        e0 = (i1 - n_groups).astype(F32)
        e1 = (i2 - n_groups).astype(F32)
        rec = jnp.zeros_like(logits)
        for idx, val in ((R_W0, w0), (R_W1, w1)):
            rec = jnp.where(lane == idx, val, rec)
        ri_ref[0, rs, :] = rec

        r0_hi = jnp.floor(rank0 * (1.0 / 256.0))
        r1_hi = jnp.floor(rank1 * (1.0 / 256.0))
        ints = jnp.zeros_like(logits)
        for idx, val in ((T_E0, e0), (T_E1, e1), (T_R0_HI, r0_hi), (T_R0_LO, rank0 - 256.0 * r0_hi),
                         (T_R1_HI, r1_hi), (T_R1_LO, rank1 - 256.0 * r1_hi)):
            ints = jnp.where(lane == idx, val, ints)
        rt = lax.dot_general(sel, ints.astype(BF16), (((1,), (1,)), ((), ())), preferred_element_type=F32)
        rt_ref[:, rs] = rt.astype(jnp.int32)

    carry_ref[...] = carry
    cnt_ref[...] = carry


def _out_route(merged, x, mod, g_post1, g_pre2, w_o_b, w_r, b_r, n_groups, per_group):
    bsz, l, d = x.shape
    tm = _tile(l, TOKEN_TILE)
    tok = lambda b, i: (b, i, 0)
    const2 = lambda b, i: (0, 0)
    return pl.pallas_call(
        functools.partial(_out_kernel, n_groups=n_groups, per_group=per_group),
        grid=(bsz, l // tm),
        in_specs=[pl.BlockSpec((1, tm, d), tok),
                  pl.BlockSpec((1, tm, d), tok),
                  pl.BlockSpec((1, N_MOD, d), lambda b, i: (b, 0, 0)),
                  pl.BlockSpec((1, d), const2),
                  pl.BlockSpec((1, d), const2),
                  pl.BlockSpec((d, d), const2),
                  pl.BlockSpec((d, LANES), const2),
                  pl.BlockSpec((1, LANES), const2)],
        out_specs=[pl.BlockSpec((1, tm, d), tok),
                   pl.BlockSpec((1, tm, d), tok),
                   pl.BlockSpec((1, tm, LANES), tok),
                   pl.BlockSpec((SUBLANES, tm), lambda b, i: (0, b * (l // tm) + i)),
                   pl.BlockSpec((1, LANES), const2)],
        out_shape=[jax.ShapeDtypeStruct((bsz, l, d), F32),
                   jax.ShapeDtypeStruct((bsz, l, d), F32),
                   jax.ShapeDtypeStruct((bsz, l, LANES), F32),
                   jax.ShapeDtypeStruct((SUBLANES, bsz * l), jnp.int32),
                   jax.ShapeDtypeStruct((1, LANES), F32)],
        scratch_shapes=[pltpu.VMEM((1, LANES), F32)],
        compiler_params=_params("arbitrary", "arbitrary"),
        name="out_route",
    )(merged, x, mod, g_post1.reshape(1, d), g_pre2.reshape(1, d), w_o_b, w_r, b_r)


def _dispatch_kernel(pos_ref, zf_ref, h_ref, xs_ref, zbuf, stage, sem, zsem, *, tm, n_c, n_tiles, n_tok, n_steps):
    i = pl.program_id(0)
    tile_rows = EXPERT_TILE * n_c
    prev = (i + 1) % 2

    def zero_copy(t):
        return pltpu.make_async_copy(zbuf, xs_ref.at[pl.ds(pl.multiple_of(t * tile_rows, tile_rows), tile_rows)], zsem)

    @pl.when(i == 0)
    def _():
        zbuf[...] = jnp.zeros_like(zbuf)

        def issue(t, c):
            @pl.when(zf_ref[t] != 0)
            def _():
                zero_copy(t).start()
            return c

        def drain(t, c):
            @pl.when(zf_ref[t] != 0)
            def _():
                zero_copy(t).wait()
            return c

        lax.fori_loop(0, n_tiles, issue, 0)
        lax.fori_loop(0, n_tiles, drain, 0)

    def issue_rows(rb, c):
        for u in range(DMA_UNROLL):
            r = rb * DMA_UNROLL + u
            src = stage.at[prev, pl.ds(pl.multiple_of(r * n_c, n_c), n_c)]
            for k in range(TOP_K):
                p = pos_ref[k * n_tok + (i - 1) * tm + r]
                pltpu.make_async_copy(src, xs_ref.at[pl.ds(pl.multiple_of(p * n_c, n_c), n_c)],
                                      sem).start(priority=k % 2)
        return c

    @pl.when(i >= 1)
    def _():
        lax.fori_loop(0, tm // DMA_UNROLL, issue_rows, 0)

    @pl.when(i < n_steps)
    def _():
        _store_row_tiled(stage.at[i % 2], 0, h_ref[...])

    @pl.when(i >= 1)
    def _():
        for k in range(TOP_K):
            pltpu.make_async_copy(stage.at[prev], xs_ref.at[pl.ds(0, tm * n_c)], sem).wait()


def _dispatch(h2, pos, zflag, n_rows):
    n, d = h2.shape
    n_c = d // LANES
    tm = _tile(n, DISPATCH_TILE)
    n_steps = n // tm
    n_tiles = n_rows // EXPERT_TILE
    return pl.pallas_call(
        functools.partial(_dispatch_kernel, tm=tm, n_c=n_c, n_tiles=n_tiles, n_tok=n, n_steps=n_steps),
        grid_spec=pltpu.PrefetchScalarGridSpec(
            num_scalar_prefetch=2,
            grid=(n_steps + 1,),
            in_specs=[pl.BlockSpec((tm, d), lambda i, p, z: (jnp.minimum(i, n_steps - 1), 0))],
            out_specs=pl.BlockSpec(memory_space=pl.ANY),
            scratch_shapes=[pltpu.VMEM((EXPERT_TILE * n_c, LANES), F32),
                            pltpu.VMEM((2, tm * n_c, LANES), F32),
                            pltpu.SemaphoreType.DMA(()),
                            pltpu.SemaphoreType.DMA(())]),
        out_shape=jax.ShapeDtypeStruct((n_rows * n_c, LANES), F32),
        compiler_params=_params("arbitrary"),
        name="dispatch",
    )(pos, zflag, h2)


def _cast_rows(src_ref, dst_ref, chunk=256):
    rows = dst_ref.shape[0]
    chunk = min(chunk, rows)

    def body(c, carry):
        r0 = pl.multiple_of(c * chunk, chunk)
        dst_ref[pl.ds(r0, chunk), :] = src_ref[pl.ds(r0, chunk), :].astype(BF16)
        return carry

    lax.fori_loop(0, rows // chunk, body, 0)


def _experts_kernel(te_ref, chg_ref, nxt_ref, nt_ref, xs_ref, wg_hbm, wu_hbm, wd_hbm, o_ref,
                    wf_g, wf_u, wf_d, wb_g, wb_u, wb_d, sem):
    t = pl.program_id(0)
    stages = ((wg_hbm, wf_g, wb_g), (wu_hbm, wf_u, wb_u), (wd_hbm, wf_d, wb_d))

    def fetch(e):
        return [pltpu.make_async_copy(src.at[e], dst, sem) for src, dst, _ in stages]

    @pl.when(t < nt_ref[0])
    def _():
        @pl.when(chg_ref[t] != 0)
        def _():
            @pl.when(t == 0)
            def _():
                for cp in fetch(te_ref[0]):
                    cp.start(priority=1)

            for cp in fetch(te_ref[t]):
                cp.wait()
            for _, wf, wb in stages:
                _cast_rows(wf, wb)

            @pl.when(nxt_ref[t] >= 0)
            def _():
                for cp in fetch(nxt_ref[t]):
                    cp.start(priority=1)

        x = _load_row_tiled(xs_ref, EXPERT_TILE, wb_g.shape[0] // LANES).astype(BF16)
        g = _dot(x, wb_g[...])
        u = _dot(x, wb_u[...])
        hid = (g * jax.nn.sigmoid(g) * u).astype(BF16)
        o_ref[...] = _dot(hid, wb_d[...])

    @pl.when(t >= nt_ref[0])
    def _():
        o_ref[...] = jnp.zeros_like(o_ref)


def _experts(xs, w_gate, w_up, w_down, plan):
    d, de = w_gate.shape[1], w_gate.shape[2]
    n_c = d // LANES
    n_rows = xs.shape[0] // n_c
    tile = lambda t, *_: (t, 0)
    hbm = pl.BlockSpec(memory_space=pl.ANY)
    return pl.pallas_call(
        _experts_kernel,
        grid_spec=pltpu.PrefetchScalarGridSpec(
            num_scalar_prefetch=len(plan),
            grid=(n_rows // EXPERT_TILE,),
            in_specs=[pl.BlockSpec((EXPERT_TILE * n_c, LANES),
                                   lambda t, te, chg, nxt, nt: (jnp.minimum(t, nt[0] - 1), 0)), hbm, hbm, hbm],
            out_specs=pl.BlockSpec((EXPERT_TILE, d), tile),
            scratch_shapes=[pltpu.VMEM((d, de), F32), pltpu.VMEM((d, de), F32), pltpu.VMEM((de, d), F32),
                            pltpu.VMEM((d, de), BF16), pltpu.VMEM((d, de), BF16), pltpu.VMEM((de, d), BF16),
                            pltpu.SemaphoreType.DMA(())]),
        out_shape=jax.ShapeDtypeStruct((n_rows, d), F32),
        compiler_params=_params("arbitrary"),
        name="experts",
    )(*plan, xs, w_gate, w_up, w_down)


def _combine_kernel(pos_ref, x1_ref, ri_ref, mod_ref, gp_ref, ys_ref, o_ref, buf, sem, *, tm, n_l, n_tok):
    step = pl.program_id(0) * n_l + pl.program_id(1)
    n_steps = pl.num_programs(0) * n_l
    slot = step % 2

    def issue(s, sl):
        def body(rb, c):
            for u in range(DMA_UNROLL):
                r = rb * DMA_UNROLL + u
                for k in range(TOP_K):
                    p = pos_ref[k * n_tok + s * tm + r]
                    pltpu.make_async_copy(ys_ref.at[pl.ds(p, 1)], buf.at[sl, k, rb, pl.ds(u, 1)],
                                          sem.at[sl]).start(priority=k % 2)
            return c

        lax.fori_loop(0, tm // DMA_UNROLL, body, 0)

    @pl.when(step == 0)
    def _():
        issue(0, 0)

    @pl.when(step + 1 < n_steps)
    def _():
        issue(step + 1, 1 - slot)

    for k in range(TOP_K):
        pltpu.make_async_copy(buf.at[slot, k], buf.at[slot, k], sem.at[slot]).wait()

    ri = ri_ref[0]
    d = o_ref.shape[-1]
    moe = (ri[:, R_W0:R_W0 + 1] * buf[slot, 0].reshape(tm, d)
           + ri[:, R_W1:R_W1 + 1] * buf[slot, 1].reshape(tm, d))
    o_ref[0] = x1_ref[0] + mod_ref[0, 5:6, :] * _rms(moe, gp_ref[...])


def _combine(x1, rinfo, mod, g_post2, ys, pos):
    bsz, l, d = x1.shape
    tm = _tile(l, COMBINE_TILE)
    n_l = l // tm
    tok = lambda b, i, p: (b, i, 0)
    return pl.pallas_call(
        functools.partial(_combine_kernel, tm=tm, n_l=n_l, n_tok=bsz * l),
        grid_spec=pltpu.PrefetchScalarGridSpec(
            num_scalar_prefetch=1,
            grid=(bsz, n_l),
            in_specs=[pl.BlockSpec((1, tm, d), tok),
                      pl.BlockSpec((1, tm, LANES), tok),
                      pl.BlockSpec((1, N_MOD, d), lambda b, i, p: (b, 0, 0)),
                      pl.BlockSpec((1, d), lambda b, i, p: (0, 0)),
                      pl.BlockSpec(memory_space=pl.ANY)],
            out_specs=pl.BlockSpec((1, tm, d), tok),
            scratch_shapes=[pltpu.VMEM((2, TOP_K, tm // DMA_UNROLL, DMA_UNROLL, d), F32),
                            pltpu.SemaphoreType.DMA((2,))]),
        out_shape=jax.ShapeDtypeStruct((bsz, l, d), F32),
        compiler_params=_params("arbitrary", "arbitrary"),
        name="combine",
    )(pos, x1, rinfo, mod, g_post2.reshape(1, d), ys)


def _rope_tables(seq, head_dim):
    axis_dim = head_dim // 2
    t = jnp.arange(seq, dtype=jnp.int32)
    pos = jnp.stack([t // GRID_W, t % GRID_W], axis=-1).astype(F32)
    inv_freq = ROPE_THETA ** (-jnp.arange(0, axis_dim, 2, dtype=F32) / axis_dim)
    ang = pos[:, :, None] * inv_freq
    cos, sin = jnp.cos(ang), jnp.sin(ang)
    cos_t = jnp.concatenate([cos, cos], axis=-1).reshape(seq, head_dim)
    sin_t = jnp.concatenate([-sin, sin], axis=-1).reshape(seq, head_dim)
    return cos_t, sin_t


def _route_plan(rt, cnt, n_groups, n_experts, n_tiles):
    e = rt[T_E0:T_E1 + 1]
    rank = jnp.stack([rt[T_R0_HI] * 256 + rt[T_R0_LO], rt[T_R1_HI] * 256 + rt[T_R1_LO]])
    counts = cnt[0, n_groups:n_groups + n_experts].astype(jnp.int32)
    tiles_e = (counts + EXPERT_TILE - 1) // EXPERT_TILE
    ids = jnp.arange(n_experts, dtype=jnp.int32)
    tile_end = jnp.sum(jnp.where(ids[None, :] <= ids[:, None], tiles_e[None, :], 0), axis=1)
    tile_start = tile_end - tiles_e
    nt = tile_end[-1]
    row0 = tile_start * EXPERT_TILE
    pos = jnp.sum(jnp.where(e[None] == ids[:, None, None], row0[:, None, None], 0), axis=0) + rank
    t = jnp.arange(n_tiles, dtype=jnp.int32)
    owner = lambda q: jnp.sum((tile_end[None, :] <= q[:, None]).astype(jnp.int32), axis=1)
    te = owner(jnp.minimum(t, nt - 1))
    chg = ((t == 0) | (te != owner(jnp.minimum(jnp.maximum(t - 1, 0), nt - 1)))).astype(jnp.int32)
    partial_last = jnp.any((tile_end[None, :] - 1 == t[:, None]) & (counts[None, :] % EXPERT_TILE != 0), axis=1)
    zflag = ((t >= nt) | partial_last).astype(jnp.int32)
    used = tiles_e > 0
    later = used[None, :] & (ids[None, :] > te[:, None])
    nxt = jnp.min(jnp.where(later, ids[None, :], n_experts), axis=1)
    nxt = jnp.where(nxt == n_experts, -1, nxt)
    return pos.reshape(-1), (te, chg, nxt, nt.reshape(1)), zflag


def kernel(x, c, ctx, c_ctx, w_ada, b_ada, g_pre1, g_post1, g_pre2, g_post2, w_in, q_norm, k_norm,
           gm_ln, w_s, b_s, w_ba, w_bg, w_o, w_rg, b_rg, w_re, b_re, w_gate, w_up, w_down):
    bsz, seq, d = x.shape
    depth = w_ada.shape[0]
    head_dim = q_norm.shape[-1]
    q_w, gm_w = w_ba.shape[1], w_bg.shape[1]
    kv_w = (w_in.shape[2] - q_w - 2 * gm_w - 2 * d) // 2
    n_groups, per_group = w_re.shape[2], w_re.shape[3]
    n_experts = n_groups * per_group
    assert head_dim == LANES and w_s.shape[2] == LANES and gm_w // w_s.shape[1] == LANES
    assert n_groups + n_experts <= LANES and seq % GRID_W == 0
    col_q = 2 * kv_w
    col_u, col_v = col_q + q_w, col_q + q_w + gm_w
    col_ga, col_gg = col_v + gm_w, col_v + gm_w + d
    n_tok = bsz * seq
    n_rows = n_tok * TOP_K + n_experts * EXPERT_TILE
    n_tiles = n_rows // EXPERT_TILE

    cos_t, sin_t = _rope_tables(seq, head_dim)
    pad = (-(bsz + 1)) % (2 * SUBLANES)
    cs = jnp.concatenate([c, c_ctx[None, :], jnp.zeros((pad, d), F32)], axis=0)

    for l in range(depth):
        assert l + 1 == depth, "context-stream update for non-final layers is not implemented"
        mod = _ada(cs, w_ada[l], b_ada[l]).reshape(cs.shape[0], N_MOD, d)
        w_in_b = w_in[l].astype(BF16)
        w_ba_b, w_bg_b, w_o_b = w_ba[l].astype(BF16), w_bg[l].astype(BF16), w_o[l].astype(BF16)
        bs_full = jnp.repeat(b_s[l].T, LANES, axis=1)
        w_r = jnp.concatenate([w_rg[l], w_re[l].reshape(d, n_experts),
                               jnp.zeros((d, LANES - n_groups - n_experts), F32)], axis=1)
        b_r = jnp.concatenate([b_rg[l], b_re[l].reshape(n_experts),
                               jnp.zeros((LANES - n_groups - n_experts,), F32)]).reshape(1, LANES)

        kc, vc = _ctx_kv(ctx, mod, bsz, g_pre1[l], w_in_b, k_norm[l], kv_w)
        hx, kx, vx, qx, gm = _inproj(x, mod, g_pre1[l], w_in_b, k_norm[l], q_norm[l], cos_t, sin_t, gm_ln[l],
                                     w_s[l], bs_full, kv_w, q_w, gm_w, head_dim ** -0.5 * LOG2_E)
        attn = _attention(qx, kc, vc, kx, vx, kv_w // head_dim)
        merged = _merge(hx, attn, gm, w_in_b, w_ba_b, w_bg_b, col_ga, col_gg)

        x1, h2, rinfo, rt, cnt = _out_route(merged, x, mod, g_post1[l], g_pre2[l], w_o_b, w_r, b_r,
                                            n_groups, per_group)
        pos, plan, zflag = _route_plan(rt, cnt, n_groups, n_experts, n_tiles)

        xs = _dispatch(h2.reshape(n_tok, d), pos, zflag, n_rows)
        ys = _experts(xs, w_gate[l], w_up[l], w_down[l], plan)
        x = _combine(x1, rinfo, mod, g_post2[l], ys, pos)
    return x
```
